```python
import jax, jax.numpy as jnp
from jax import lax
import numpy as np

D_MODEL = 2048
BATCH = 4
SEQ = 4096
DEPTH = 1

PLE_DIM = 256
R_HEADS = 16
R_HEAD = 64
R_WIDTH = R_HEADS * R_HEAD
DECAY_LORA = 64
AAA_LORA = 64
GATE_LORA = 160
R_GN_EPS = 64e-5
RWKV_COLS = 3 * R_WIDTH + DECAY_LORA + AAA_LORA + GATE_LORA
M_HEADS = 8
M_QK = 64
M_V = 128
M_WIDTH = M_HEADS * M_V
CONV_K = 4
CHUNK = 128
M_NORM_EPS = 1e-6
MLSTM_COLS = 2 * M_HEADS * M_QK + 2 * M_WIDTH + 2 * M_HEADS
GATE_COLS = 2 * D_MODEL
IN_COLS = RWKV_COLS + MLSTM_COLS + GATE_COLS
N_GROUPS = 4
EXPERTS_PER_GROUP = 8
N_EXPERTS = N_GROUPS * EXPERTS_PER_GROUP
TOP_K = 2
D_EXPERT = 512
MOE_BLOCK = 128
LN_EPS = 1e-5
ALPHA = (2 * DEPTH) ** 0.25
BETA = (8 * DEPTH) ** -0.25

kernel_name = 'hybrid_rwkv7_mlstm_hmoe_block'

F32 = jnp.float32


def _split(t, sizes):
    return jnp.split(t, np.cumsum(sizes)[:-1].tolist(), axis=-1)


def _layer_norm(x, w, b):
    xf = x.astype(F32)
    mu = xf.mean(-1, keepdims=True)
    var = jnp.square(xf - mu).mean(-1, keepdims=True)
    return ((xf - mu) * lax.rsqrt(var + LN_EPS) * w + b).astype(x.dtype)


def _head_norm(y, eps):
    mu = y.mean(-1, keepdims=True)
    var = jnp.square(y - mu).mean(-1, keepdims=True)
    return (y - mu) * lax.rsqrt(var + eps)


def _causal_conv(t, w, b):
    seq = t.shape[1]
    tp = jnp.pad(t, ((0, 0), (CONV_K - 1, 0), (0, 0)))
    return b + sum(w[j] * tp[:, j:j + seq] for j in range(CONV_K))


def _rwkv7_scan(r, decay, k, v, kk, a):
    bsz, _, h, n = r.shape

    def step(state, inp):
        r_t, w_t, k_t, v_t, kk_t, a_t = inp
        s_kk = jnp.einsum('bhvk,bhk->bhv', state, kk_t)
        state = (state * w_t[:, :, None, :]
                 - s_kk[..., None] * (kk_t * a_t)[:, :, None, :]
                 + v_t[..., None] * k_t[:, :, None, :])
        return state, jnp.einsum('bhvk,bhk->bhv', state, r_t)

    xs = tuple(jnp.moveaxis(t, 1, 0) for t in (r, decay, k, v, kk, a))
    _, ys = lax.scan(step, jnp.zeros((bsz, h, n, n), F32), xs)
    return jnp.moveaxis(ys, 0, 1)


def _rwkv7_mixer(z, mu, w0, w_w2, a0, w_a2, w_g2, k_k, k_a, r_k, lnx_w, lnx_b):
    bsz, seq, _ = z.shape
    z_prev = jnp.pad(z, ((0, 0), (1, 0), (0, 0)))[:, :-1]
    z = z + mu * (z_prev - z)
    r, k, v, zw, za, zg = _split(z, [R_WIDTH, R_WIDTH, R_WIDTH, DECAY_LORA, AAA_LORA, GATE_LORA])
    w = -jax.nn.softplus(-(w0 + jnp.tanh(zw) @ w_w2).astype(F32)) - 0.5
    decay = jnp.exp(-jnp.exp(w))
    a = jax.nn.sigmoid((a0 + za @ w_a2).astype(F32))
    g = jax.nn.sigmoid(zg) @ w_g2
    heads = lambda t: t.astype(F32).reshape(bsz, seq, R_HEADS, R_HEAD)
    r, k, v, decay, a = heads(r), heads(k), heads(v), heads(decay), heads(a)
    kk = k * k_k.astype(F32).reshape(R_HEADS, R_HEAD)
    kk = kk / jnp.maximum(jnp.sqrt(jnp.sum(kk * kk, -1, keepdims=True)), 1e-12)
    k = k * (1.0 + (a - 1.0) * k_a.astype(F32).reshape(R_HEADS, R_HEAD))
    y = _rwkv7_scan(r, decay, k, v, kk, a)
    y = _head_norm(y, R_GN_EPS).reshape(bsz, seq, R_WIDTH) * lnx_w + lnx_b
    bonus = (jnp.sum(r * k * r_k.astype(F32), -1, keepdims=True) * v).reshape(bsz, seq, R_WIDTH)
    return ((y + bonus) * g).astype(z.dtype)


def _mlstm_chunkwise(q, k, v, i_pre, log_f):
    bsz, h, seq, dqk = q.shape
    dv = v.shape[-1]
    nc = seq // CHUNK

    def chunks(t):
        return jnp.moveaxis(t.reshape(t.shape[:2] + (nc, CHUNK) + t.shape[3:]), 2, 0)

    causal = jnp.tril(jnp.ones((CHUNK, CHUNK), bool))

    def step(carry, inp):
        c_st, n_st, m_st = carry
        q_c, k_c, v_c, i_c, f_c = inp
        a = jnp.cumsum(f_c, -1)
        a_tot = a[..., -1]
        d = jnp.where(causal, a[..., :, None] - a[..., None, :] + i_c[..., None, :], -jnp.inf)
        inter = a + m_st[..., None]
        m_t = jnp.maximum(inter, d.max(-1))
        s = jnp.einsum('bhtd,bhsd->bhts', q_c, k_c) * jnp.exp(d - m_t[..., None])
        ie = jnp.exp(inter - m_t)
        num = (ie[..., None] * jnp.einsum('bhtd,bhde->bhte', q_c, c_st)
               + jnp.einsum('bhts,bhse->bhte', s, v_c))
        den = ie * jnp.einsum('bhtd,bhd->bht', q_c, n_st) + s.sum(-1)
        h_c = num / jnp.maximum(jnp.abs(den), jnp.exp(-m_t))[..., None]
        gl = a_tot[..., None] - a + i_c
        m_new = jnp.maximum(a_tot + m_st, gl.max(-1))
        sc = jnp.exp(a_tot + m_st - m_new)
        ge = jnp.exp(gl - m_new[..., None])
        c_st = sc[..., None, None] * c_st + jnp.einsum('bhs,bhsd,bhse->bhde', ge, k_c, v_c)
        n_st = sc[..., None] * n_st + jnp.einsum('bhs,bhsd->bhd', ge, k_c)
        return (c_st, n_st, m_new), h_c

    init = (jnp.zeros((bsz, h, dqk, dv), F32), jnp.zeros((bsz, h, dqk), F32),
            jnp.full((bsz, h), -jnp.inf, F32))
    _, hs = lax.scan(step, init, tuple(chunks(t) for t in (q, k, v, i_pre, log_f)))
    return jnp.moveaxis(hs, 0, 2).reshape(bsz, h, seq, dv)


def _mlstm_mixer(z, conv_w, conv_b, i_bias, f_bias, mh_w):
    bsz, seq, _ = z.shape
    qk, v, ig, fg, o = _split(z, [2 * M_HEADS * M_QK, M_WIDTH, M_HEADS, M_HEADS, M_WIDTH])
    qk = jax.nn.silu(_causal_conv(qk, conv_w, conv_b))
    q, k = _split(qk, [M_HEADS * M_QK, M_HEADS * M_QK])
    heads = lambda t, d: t.astype(F32).reshape(bsz, seq, M_HEADS, d).transpose(0, 2, 1, 3)
    q, k, v = heads(q, M_QK), heads(k, M_QK) * M_QK ** -0.5, heads(v, M_V)
    i_pre = (ig + i_bias).astype(F32).transpose(0, 2, 1)
    log_f = jax.nn.log_sigmoid((fg + f_bias).astype(F32)).transpose(0, 2, 1)
    h = _mlstm_chunkwise(q, k, v, i_pre, log_f)
    h = _head_norm(h, M_NORM_EPS).transpose(0, 2, 1, 3).reshape(bsz, seq, M_WIDTH) * mh_w
    return (jax.nn.sigmoid(o) * h).astype(z.dtype)


def _hier_moe(xf, w_rg, b_rg, w_re, b_re, w_gate, w_up, w_down):
    n, d = xf.shape
    lg = (xf @ w_rg).astype(F32) + b_rg
    g_sel = jnp.argmax(lg, -1)
    g_w = jnp.take_along_axis(jax.nn.softmax(lg, -1), g_sel[:, None], -1)
    le = ((xf @ w_re).astype(F32) + b_re).reshape(n, N_GROUPS, EXPERTS_PER_GROUP)
    le = jnp.take_along_axis(le, g_sel[:, None, None], 1)[:, 0]
    top_l, top_i = lax.top_k(le, TOP_K)
    wts = jax.nn.softmax(top_l, -1) * g_w
    eid = (g_sel[:, None] * EXPERTS_PER_GROUP + top_i).reshape(-1)
    tok = jnp.repeat(jnp.arange(n, dtype=jnp.int32), TOP_K)
    wt = wts.reshape(-1)
    order = jnp.argsort(eid)
    eid_s, tok_s, wt_s = eid[order], tok[order], wt[order]
    counts = jnp.bincount(eid, length=N_EXPERTS)
    start = jnp.cumsum(counts) - counts
    padded = ((counts + MOE_BLOCK - 1) // MOE_BLOCK) * MOE_BLOCK
    pend = jnp.cumsum(padded)
    pstart = pend - padded
    n_rows = TOP_K * n
    dest = pstart[eid_s] + (jnp.arange(n_rows) - start[eid_s])
    nb = -(-n_rows // MOE_BLOCK) + N_EXPERTS
    rows = nb * MOE_BLOCK
    tok_rows = jnp.zeros((rows,), jnp.int32).at[dest].set(tok_s)
    wt_rows = jnp.zeros((rows,), F32).at[dest].set(wt_s)
    blk_e = jnp.minimum(jnp.searchsorted(pend, jnp.arange(nb) * MOE_BLOCK, side='right'),
                        N_EXPERTS - 1)

    def one_block(args):
        t_idx, e = args
        xb = xf[t_idx]
        hb = jax.nn.silu(xb @ w_gate[e]) * (xb @ w_up[e])
        return hb @ w_down[e]

    out = lax.map(one_block, (tok_rows.reshape(nb, MOE_BLOCK), blk_e)).reshape(rows, d)
    out = out * wt_rows[:, None].astype(out.dtype)
    return jnp.zeros_like(xf).at[tok_rows].add(out)


def setup_inputs(seed: int = 0) -> dict:
    key = jax.random.key(seed)
    ks = iter(jax.random.split(key, 40))
    nrm = lambda shape, s: s * jax.random.normal(next(ks), shape, F32)
    uni = lambda shape, lo, hi: jax.random.uniform(next(ks), shape, F32, lo, hi)
    L, D = DEPTH, D_MODEL
    col_scale = jnp.concatenate([
        jnp.ones((2 * R_WIDTH,), F32), jnp.full((R_WIDTH,), BETA, F32),
        jnp.ones((RWKV_COLS - 3 * R_WIDTH + 2 * M_HEADS * M_QK,), F32),
        jnp.full((M_WIDTH,), BETA, F32),
        jnp.ones((2 * M_HEADS + M_WIDTH + GATE_COLS,), F32)])
    return {
        'x': nrm((BATCH, SEQ, D), 1.0),
        'p': nrm((L, BATCH, SEQ, PLE_DIM), 1.0),
        'w_in': nrm((L, D, IN_COLS), D ** -0.5) * col_scale,
        'mu_shift': uni((L, RWKV_COLS), 0.0, 1.0),
        'w0': uni((L, R_WIDTH), -6.0, -1.0),
        'w_w2': nrm((L, DECAY_LORA, R_WIDTH), 0.5 * DECAY_LORA ** -0.5),
        'a0': nrm((L, R_WIDTH), 0.1),
        'w_a2': nrm((L, AAA_LORA, R_WIDTH), 0.5 * AAA_LORA ** -0.5),
        'w_g2': nrm((L, GATE_LORA, R_WIDTH), GATE_LORA ** -0.5),
        'k_k': 0.85 + nrm((L, R_WIDTH), 0.05),
        'k_a': 1.0 + nrm((L, R_WIDTH), 0.05),
        'r_k': nrm((L, R_HEADS, R_HEAD), 0.1),
        'lnx_w': 1.0 + nrm((L, R_WIDTH), 0.05),
        'lnx_b': nrm((L, R_WIDTH), 0.01),
        'conv_w': nrm((L, CONV_K, 2 * M_HEADS * M_QK), CONV_K ** -0.5),
        'conv_b': nrm((L, 2 * M_HEADS * M_QK), 0.01),
        'i_bias': nrm((L, M_HEADS), 0.1),
        'f_bias': uni((L, M_HEADS), 3.0, 6.0),
        'mh_w': 1.0 + nrm((L, M_WIDTH), 0.05),
        'b_gate': nrm((L, GATE_COLS), 0.01),
        'w_br': nrm((L, R_WIDTH, D), BETA * R_WIDTH ** -0.5),
        'w_bm': nrm((L, M_WIDTH, D), BETA * M_WIDTH ** -0.5),
        'w_out': nrm((L, D, D), BETA * D ** -0.5),
        'ln1_w': 1.0 + nrm((L, D), 0.05),
        'ln1_b': nrm((L, D), 0.01),
        'w_rg': nrm((L, D, N_GROUPS), D ** -0.5),
        'b_rg': nrm((L, N_GROUPS), 0.01),
        'w_re': nrm((L, D, N_EXPERTS), D ** -0.5),
        'b_re': nrm((L, N_EXPERTS), 0.01),
        'w_gate': nrm((L, N_EXPERTS, D, D_EXPERT), BETA * D ** -0.5),
        'w_up': nrm((L, N_EXPERTS, D, D_EXPERT), BETA * D ** -0.5),
        'w_down': nrm((L, N_EXPERTS, D_EXPERT, D), BETA * D_EXPERT ** -0.5),
        'w_pg': nrm((L, D, D), D ** -0.5),
        'w_ple': nrm((L, PLE_DIM, D), BETA * PLE_DIM ** -0.5),
        'ln2_w': 1.0 + nrm((L, D), 0.05),
        'ln2_b': nrm((L, D), 0.01),
    }


def reference(x, p, w_in, mu_shift, w0, w_w2, a0, w_a2, w_g2, k_k, k_a, r_k, lnx_w, lnx_b,
              conv_w, conv_b, i_bias, f_bias, mh_w, b_gate, w_br, w_bm, w_out, ln1_w, ln1_b,
              w_rg, b_rg, w_re, b_re, w_gate, w_up, w_down, w_pg, w_ple, ln2_w, ln2_b):
    bsz, seq, d = x.shape
    for i in range(DEPTH):
        u = x @ w_in[i]
        u_r, u_m, u_g = _split(u, [RWKV_COLS, MLSTM_COLS, GATE_COLS])
        y_r = _rwkv7_mixer(u_r, mu_shift[i], w0[i], w_w2[i], a0[i], w_a2[i], w_g2[i],
                           k_k[i], k_a[i], r_k[i], lnx_w[i], lnx_b[i])
        y_m = _mlstm_mixer(u_m, conv_w[i], conv_b[i], i_bias[i], f_bias[i], mh_w[i])
        g_r, g_m = _split(u_g + b_gate[i], [d, d])
        mix = (jax.nn.sigmoid(g_r) * (y_r @ w_br[i])
               + jax.nn.sigmoid(g_m) * (y_m @ w_bm[i])) @ w_out[i]
        x = _layer_norm(ALPHA * x + mix, ln1_w[i], ln1_b[i])
        moe = _hier_moe(x.reshape(bsz * seq, d), w_rg[i], b_rg[i], w_re[i], b_re[i],
                        w_gate[i], w_up[i], w_down[i]).reshape(bsz, seq, d)
        ple = jax.nn.sigmoid(x @ w_pg[i]) * (p[i] @ w_ple[i])
        x = _layer_norm(ALPHA * x + moe + ple, ln2_w[i], ln2_b[i])
    return x
```

```python
import functools

import jax
import jax.numpy as jnp
from jax import lax
from jax.experimental import pallas as pl
from jax.experimental.pallas import tpu as pltpu

F32 = jnp.float32
BF16 = jnp.bfloat16
I32 = jnp.int32

R_HEADS, R_HEAD = 16, 64
R_WIDTH = R_HEADS * R_HEAD
DECAY_LORA, AAA_LORA, GATE_LORA = 64, 64, 160
LORA_COLS = DECAY_LORA + AAA_LORA + GATE_LORA
LORA_PAD = 512
R_GN_EPS = 64e-5
RWKV_COLS = 3 * R_WIDTH + LORA_COLS
M_HEADS, M_QK, M_V = 8, 64, 128
M_WIDTH = M_HEADS * M_V
CONV_K = 4
CHUNK = 128
M_NORM_EPS = 1e-6
MLSTM_COLS = 2 * M_HEADS * M_QK + 2 * M_WIDTH + 2 * M_HEADS
N_GROUPS, EXPERTS_PER_GROUP = 4, 8
N_EXPERTS = N_GROUPS * EXPERTS_PER_GROUP
TOP_K = 2
MOE_BLOCK = 128
LN_EPS = 1e-5

LANES = 128
SUBLANES = 8
V7X_VMEM_BYTES = 64 * 1024 * 1024
VMEM_LIMIT = 56 * 1024 * 1024

C_RKV = 0
C_MQK = 3072
C_MV = 4096
C_MO = 5120
C_GR = 6144
C_GM = 8192
C_LORA = 10240
C_MG = 10752
C_TOTAL = 10880
PROJ_TN = 2176
PROJ_TM = 512


def _cparams(sem, vmem=VMEM_LIMIT):
    return pltpu.CompilerParams(dimension_semantics=sem, vmem_limit_bytes=vmem)


def _sigmoid(x):
    return 1.0 / (1.0 + jnp.exp(-x))


def _softplus(x):
    return jnp.maximum(x, 0.0) + jnp.log1p(jnp.exp(-jnp.abs(x)))


def _split3(x):
    hi = x.astype(BF16)
    r1 = x - hi.astype(F32)
    mid = r1.astype(BF16)
    lo = (r1 - mid.astype(F32)).astype(BF16)
    return hi, mid, lo


def _dot(a, b):
    return jnp.dot(a, b, preferred_element_type=F32)


def _dot_nt(a, b):
    return lax.dot_general(a, b, (((1,), (1,)), ((), ())), preferred_element_type=F32)


def _dot_tn(a, b):
    return lax.dot_general(a, b, (((0,), (0,)), ((), ())), preferred_element_type=F32)


def _dot_exact_rhs(x, ones_bf16):
    hi, mid, lo = _split3(x)
    return _dot(hi, ones_bf16) + _dot(mid, ones_bf16) + _dot(lo, ones_bf16)


def _block_ones(n, group):
    r = lax.broadcasted_iota(I32, (n, n), 0) // group
    c = lax.broadcasted_iota(I32, (n, n), 1) // group
    return jnp.where(r == c, 1.0, 0.0).astype(BF16)


def _seg_sum(x, group):
    ones = _block_ones(LANES, group)
    slabs = [_dot_exact_rhs(x[:, p * LANES:(p + 1) * LANES], ones) for p in range(x.shape[1] // LANES)]
    return jnp.concatenate(slabs, axis=1)


def _layer_norm(x, w, b):
    mu = jnp.mean(x, axis=-1, keepdims=True)
    xc = x - mu
    var = jnp.mean(xc * xc, axis=-1, keepdims=True)
    return xc * lax.rsqrt(var + LN_EPS) * w + b


def _proj_kernel(x_ref, w_ref, o_ref):
    o_ref[...] = _dot(x_ref[...], w_ref[...])


def _proj_in(x_bf, w_bf):
    m, k = x_bf.shape
    n = w_bf.shape[1]
    return pl.pallas_call(
        _proj_kernel,
        grid=(n // PROJ_TN, m // PROJ_TM),
        in_specs=[pl.BlockSpec((PROJ_TM, k), lambda j, i: (i, 0)),
                  pl.BlockSpec((k, PROJ_TN), lambda j, i: (0, j))],
        out_specs=pl.BlockSpec((PROJ_TM, PROJ_TN), lambda j, i: (i, j)),
        out_shape=jax.ShapeDtypeStruct((m, n), F32),
        compiler_params=_cparams(("parallel", "parallel")),
        name="proj_in",
    )(x_bf, w_bf)


PREP_TM = 256


def _rwkv_prep_kernel(seq, u_ref, l_ref, up_ref, lp_ref, mu_ref, mul_ref, w0_ref, a0_ref, kk_ref, ka_ref,
                      rk_ref, ww_ref, wa_ref, wg_ref,
                      r_o, dec_o, kp_o, v_o, kn_o, beta_o, g_o, bonus_o):
    i = pl.program_id(0)
    tm = u_ref.shape[0]
    first = (i * tm) % seq == 0
    row = lax.broadcasted_iota(I32, (tm, 1), 0)

    def shift(u, prev8):
        prev_row = jnp.where(first, 0.0, prev8[SUBLANES - 1:SUBLANES, :])
        return jnp.where(row == 0, prev_row, pltpu.roll(u, 1, 0))

    u = u_ref[...]
    z = u + mu_ref[...] * (shift(u, up_ref[...]) - u)
    lo = l_ref[...]
    zl = lo + mul_ref[...] * (shift(lo, lp_ref[...]) - lo)

    r = z[:, 0:R_WIDTH]
    k = z[:, R_WIDTH:2 * R_WIDTH]
    v = z[:, 2 * R_WIDTH:3 * R_WIDTH]
    w_pre = w0_ref[...] + _dot(jnp.tanh(zl).astype(BF16), ww_ref[...])
    w = -_softplus(-w_pre) - 0.5
    dec = jnp.exp(-jnp.exp(w))
    a = _sigmoid(a0_ref[...] + _dot(zl.astype(BF16), wa_ref[...]))
    g = _dot(_sigmoid(zl).astype(BF16), wg_ref[...])

    kk = k * kk_ref[...]
    nrm = jnp.sqrt(_seg_sum(kk * kk, R_HEAD))
    kn = kk / jnp.maximum(nrm, 1e-12)
    kp = k * (1.0 + (a - 1.0) * ka_ref[...])
    bonus = _seg_sum(r * kp * rk_ref[...], R_HEAD) * v

    r_o[...] = r
    dec_o[...] = dec
    kp_o[...] = kp
    v_o[...] = v
    kn_o[...] = kn
    beta_o[...] = kn * a
    g_o[...] = g
    bonus_o[...] = bonus


def _rwkv_prep(u, seq, mu, w0, w_w2, a0, w_a2, w_g2, k_k, k_a, r_k):
    n = u.shape[0]
    tm = PREP_TM
    mu_rkv = mu[None, :3 * R_WIDTH]
    mu_lora = jnp.zeros((1, LORA_PAD), F32).at[0, :LORA_COLS].set(mu[3 * R_WIDTH:])
    ww = jnp.zeros((LORA_PAD, R_WIDTH), BF16).at[0:DECAY_LORA].set(w_w2.astype(BF16))
    wa = jnp.zeros((LORA_PAD, R_WIDTH), BF16).at[DECAY_LORA:DECAY_LORA + AAA_LORA].set(w_a2.astype(BF16))
    wg = jnp.zeros((LORA_PAD, R_WIDTH), BF16).at[DECAY_LORA + AAA_LORA:LORA_COLS].set(w_g2.astype(BF16))
    row = lambda t: t.reshape(1, R_WIDTH)
    prev_blk = lambda i: jnp.maximum(i * (tm // SUBLANES) - 1, 0)
    const = lambda shape: pl.BlockSpec(shape, lambda i: (0, 0))
    nat = jax.ShapeDtypeStruct((n, R_WIDTH), F32)
    return pl.pallas_call(
        functools.partial(_rwkv_prep_kernel, seq),
        grid=(n // tm,),
        in_specs=[pl.BlockSpec((tm, 3 * R_WIDTH), lambda i: (i, C_RKV // (3 * R_WIDTH))),
                  pl.BlockSpec((tm, LORA_PAD), lambda i: (i, C_LORA // LORA_PAD)),
                  pl.BlockSpec((SUBLANES, 3 * R_WIDTH), lambda i: (prev_blk(i), C_RKV // (3 * R_WIDTH))),
                  pl.BlockSpec((SUBLANES, LORA_PAD), lambda i: (prev_blk(i), C_LORA // LORA_PAD)),
                  const((1, 3 * R_WIDTH)), const((1, LORA_PAD)),
                  const((1, R_WIDTH)), const((1, R_WIDTH)), const((1, R_WIDTH)), const((1, R_WIDTH)),
                  const((1, R_WIDTH)),
                  const((LORA_PAD, R_WIDTH)), const((LORA_PAD, R_WIDTH)), const((LORA_PAD, R_WIDTH))],
        out_specs=[pl.BlockSpec((tm, R_WIDTH), lambda i: (i, 0))] * 8,
        out_shape=[nat] * 8,
        compiler_params=_cparams(("parallel",)),
        name="rwkv_prep",
    )(u, u, u, u, mu_rkv, mu_lora, row(w0), row(a0), row(k_k), row(k_a), row(r_k), ww, wa, wg)


SCAN_TB = 32
SCAN_V = R_HEAD // 2
SCAN_ACC = 4


def _rwkv_scan_kernel(r_ref, dec_ref, kp_ref, kn_ref, beta_ref, v_ref, y_ref, s_ref):
    @pl.when(pl.program_id(0) == 0)
    def _():
        s_ref[...] = jnp.zeros(s_ref.shape, F32)

    def step(t, carry):
        v_t = v_ref[t]
        acc = [None] * SCAN_ACC
        for k in range(R_HEAD):
            term = s_ref[k] * kn_ref[t, k:k + 1, :]
            acc[k % SCAN_ACC] = term if acc[k % SCAN_ACC] is None else acc[k % SCAN_ACC] + term
        s_kk = (acc[0] + acc[1]) + (acc[2] + acc[3])
        acc = [None] * SCAN_ACC
        for k in range(R_HEAD):
            s_new = (s_ref[k] * dec_ref[t, k:k + 1, :]
                     + (v_t * kp_ref[t, k:k + 1, :] - s_kk * beta_ref[t, k:k + 1, :]))
            s_ref[k] = s_new
            term = s_new * r_ref[t, k:k + 1, :]
            acc[k % SCAN_ACC] = term if acc[k % SCAN_ACC] is None else acc[k % SCAN_ACC] + term
        y_ref[t] = (acc[0] + acc[1]) + (acc[2] + acc[3])
        return carry

    lax.fori_loop(0, r_ref.shape[0], step, 0)


def _rwkv_scan(r, dec, kp, kn, beta, v):
    t = r.shape[0]
    kspec = pl.BlockSpec((SCAN_TB, R_HEAD, LANES), lambda i: (i, 0, 0))
    vspec = pl.BlockSpec((SCAN_TB, SCAN_V, LANES), lambda i: (i, 0, 0))
    return pl.pallas_call(
        _rwkv_scan_kernel,
        grid=(t // SCAN_TB,),
        in_specs=[kspec] * 5 + [vspec],
        out_specs=vspec,
        out_shape=jax.ShapeDtypeStruct((t, SCAN_V, LANES), F32),
        scratch_shapes=[pltpu.VMEM((R_HEAD, SCAN_V, LANES), F32)],
        compiler_params=_cparams(("arbitrary",)),
        name="rwkv_scan",
    )(r, dec, kp, kn, beta, v)


def _to_scan_k(a, bsz, seq):
    a = a.reshape(bsz, seq, R_HEADS, R_HEAD).transpose(1, 3, 0, 2).reshape(seq, R_HEAD, bsz * R_HEADS)
    return jnp.concatenate([a, a], axis=-1)


def _to_scan_v(a, bsz, seq):
    a = a.reshape(bsz, seq, R_HEADS, 2, SCAN_V).transpose(1, 4, 3, 0, 2)
    return a.reshape(seq, SCAN_V, 2 * bsz * R_HEADS)


def _from_scan_v(y, bsz, seq):
    y = y.reshape(seq, SCAN_V, 2, bsz, R_HEADS).transpose(3, 0, 4, 2, 1)
    return y.reshape(bsz * seq, R_WIDTH)


M_AUG = 2 * M_V


def _mlstm_kernel(qk_ref, v_ref, o_ref, g_ref, cw_ref, cb_ref, gb_ref, mh_ref, y_ref,
                  ext_ref, c_ref, m_ref):
    c_idx = pl.program_id(1)
    L = CHUNK

    @pl.when(c_idx == 0)
    def _():
        ext_ref[0:SUBLANES, :] = jnp.zeros((SUBLANES, ext_ref.shape[1]), F32)
        c_ref[...] = jnp.zeros(c_ref.shape, F32)
        m_ref[...] = jnp.full(m_ref.shape, -jnp.inf, F32)

    ext_ref[SUBLANES:SUBLANES + L, :] = qk_ref[...]
    conv = cb_ref[...]
    for j in range(CONV_K):
        off = SUBLANES - (CONV_K - 1) + j
        conv = conv + cw_ref[j:j + 1, :] * ext_ref[off:off + L, :]
    ext_ref[0:SUBLANES, :] = ext_ref[L:L + SUBLANES, :]
    qk = conv * _sigmoid(conv)
    q_all = qk[:, :M_HEADS * M_QK].astype(BF16)
    k_all = qk[:, M_HEADS * M_QK:] * (M_QK ** -0.5)

    gpre = g_ref[...] + gb_ref[...]
    lane = lax.broadcasted_iota(I32, (L, LANES), 1)
    gcols = jnp.where(lane < M_HEADS, gpre, -_softplus(-gpre))
    grows = gcols.T
    ti = lax.broadcasted_iota(I32, (L, L), 0)
    si = lax.broadcasted_iota(I32, (L, L), 1)
    causal = si <= ti
    ltri = jnp.where(causal, 1.0, 0.0).astype(BF16)
    utri = jnp.where(ti <= si, 1.0, 0.0).astype(BF16)
    acols = _dot_exact_rhs_left(ltri, gcols)
    arows = _dot_exact_rhs(grows, utri)
    ones_col = jnp.where(lane == 0, 1.0, 0.0).astype(BF16)

    for h in range(M_HEADS):
        a_col = acols[:, M_HEADS + h:M_HEADS + h + 1]
        a_row = arows[M_HEADS + h:M_HEADS + h + 1, :]
        i_col = gcols[:, h:h + 1]
        i_row = grows[h:h + 1, :]
        m_st = m_ref[h:h + 1, 0:1]
        d = jnp.where(causal, a_col - a_row + i_row, -jnp.inf)
        inter = a_col + m_st
        m_t = jnp.maximum(inter, jnp.max(d, axis=-1, keepdims=True))
        q = q_all[:, h * M_QK:(h + 1) * M_QK]
        k = k_all[:, h * M_QK:(h + 1) * M_QK]
        v_aug = jnp.concatenate([v_ref[:, h * M_V:(h + 1) * M_V].astype(BF16), ones_col], axis=1)
        s = _dot_nt(q, k.astype(BF16)) * jnp.exp(d - m_t)
        ie = jnp.exp(inter - m_t)
        tot = ie * _dot(q, c_ref[h].astype(BF16)) + _dot(s.astype(BF16), v_aug)
        num = tot[:, :M_V]
        den = tot[:, M_V:M_V + 1]
        h_c = num / jnp.maximum(jnp.abs(den), jnp.exp(-m_t))

        a_tot = a_col[L - 1:L, :]
        gl = a_tot - a_col + i_col
        m_new = jnp.maximum(a_tot + m_st, jnp.max(gl, axis=0, keepdims=True))
        sc = jnp.exp(a_tot + m_st - m_new)
        ge = jnp.exp(gl - m_new)
        c_ref[h] = sc * c_ref[h] + _dot_tn((k * ge).astype(BF16), v_aug)
        m_ref[h:h + 1, :] = jnp.broadcast_to(m_new, (1, LANES))

        mu = jnp.mean(h_c, axis=-1, keepdims=True)
        hc = h_c - mu
        var = jnp.mean(hc * hc, axis=-1, keepdims=True)
        hn = hc * lax.rsqrt(var + M_NORM_EPS) * mh_ref[:, h * M_V:(h + 1) * M_V]
        y_ref[:, h * M_V:(h + 1) * M_V] = _sigmoid(o_ref[:, h * M_V:(h + 1) * M_V]) * hn


def _dot_exact_rhs_left(ones_bf16, x):
    hi, mid, lo = _split3(x)
    return _dot(ones_bf16, hi) + _dot(ones_bf16, mid) + _dot(ones_bf16, lo)


def _mlstm(u, bsz, seq, conv_w, conv_b, i_bias, f_bias, mh_w):
    nc = seq // CHUNK
    w = M_WIDTH
    gbias = jnp.zeros((1, LANES), F32).at[0, :M_HEADS].set(i_bias).at[0, M_HEADS:2 * M_HEADS].set(f_bias)
    blk = lambda col: pl.BlockSpec((CHUNK, w), lambda b, c: (b * nc + c, col // w))
    const = lambda shape: pl.BlockSpec(shape, lambda b, c: (0, 0))
    return pl.pallas_call(
        _mlstm_kernel,
        grid=(bsz, nc),
        in_specs=[blk(C_MQK), blk(C_MV), blk(C_MO),
                  pl.BlockSpec((CHUNK, LANES), lambda b, c: (b * nc + c, C_MG // LANES)),
                  const((CONV_K, w)), const((1, w)), const((1, LANES)), const((1, w))],
        out_specs=pl.BlockSpec((CHUNK, w), lambda b, c: (b * nc + c, 0)),
        out_shape=jax.ShapeDtypeStruct((bsz * seq, w), F32),
        scratch_shapes=[pltpu.VMEM((CHUNK + SUBLANES, w), F32),
                        pltpu.VMEM((M_HEADS, M_QK, M_AUG), F32),
                        pltpu.VMEM((M_HEADS, LANES), F32)],
        compiler_params=_cparams(("parallel", "arbitrary")),
        name="mlstm",
    )(u, u, u, u, conv_w, conv_b[None, :], gbias, mh_w[None, :])


MERGE_TM = 256
C_ROUTE_G = N_EXPERTS


def _merge_kernel(alpha, ys_ref, bonus_ref, g_ref, ym_ref, gr_ref, gm_ref, x_ref,
                  lnxw_ref, lnxb_ref, bgr_ref, bgm_ref, wbr_ref, wbm_ref, wout_ref, l1w_ref, l1b_ref,
                  wrh_ref, wrl_ref, br_ref, x1_o, ids_o, wts_o):
    ys = ys_ref[...]
    mu = _seg_sum(ys, R_HEAD) * (1.0 / R_HEAD)
    yc = ys - mu
    var = _seg_sum(yc * yc, R_HEAD) * (1.0 / R_HEAD)
    y = yc * lax.rsqrt(var + R_GN_EPS) * lnxw_ref[...] + lnxb_ref[...]
    y_r = (y + bonus_ref[...]) * g_ref[...]
    br = _dot(y_r.astype(BF16), wbr_ref[...])
    bm = _dot(ym_ref[...].astype(BF16), wbm_ref[...])
    mix_in = _sigmoid(gr_ref[...] + bgr_ref[...]) * br + _sigmoid(gm_ref[...] + bgm_ref[...]) * bm
    mix = _dot(mix_in.astype(BF16), wout_ref[...])
    x1 = _layer_norm(alpha * x_ref[...] + mix, l1w_ref[...], l1b_ref[...])
    x1_o[...] = x1

    xh = x1.astype(BF16)
    xl = (x1 - xh.astype(F32)).astype(BF16)
    logits = (_dot(xh, wrh_ref[...]) + (_dot(xh, wrl_ref[...]) + _dot(xl, wrh_ref[...]))) + br_ref[...]
    tm = logits.shape[0]
    lane_i = lax.broadcasted_iota(I32, (tm, LANES), 1)
    lane = lane_i.astype(F32)
    group_of_lane = (lane_i // EXPERTS_PER_GROUP).astype(F32)
    big = float(LANES)
    neg = -jnp.inf
    lg = jnp.where((lane_i >= C_ROUTE_G) & (lane_i < C_ROUTE_G + N_GROUPS), logits, neg)
    gmax = jnp.max(lg, axis=-1, keepdims=True)
    gsel = jnp.min(jnp.where(lg == gmax, lane - C_ROUTE_G, big), axis=-1, keepdims=True)
    g_w = 1.0 / jnp.sum(jnp.exp(lg - gmax), axis=-1, keepdims=True)
    le = jnp.where((lane_i < N_EXPERTS) & (group_of_lane == gsel), logits, neg)
    m1 = jnp.max(le, axis=-1, keepdims=True)
    i1 = jnp.min(jnp.where(le == m1, lane, big), axis=-1, keepdims=True)
    le2 = jnp.where(lane == i1, neg, le)
    m2 = jnp.max(le2, axis=-1, keepdims=True)
    i2 = jnp.min(jnp.where(le2 == m2, lane, big), axis=-1, keepdims=True)
    e2 = jnp.exp(m2 - m1)
    w1 = g_w / (1.0 + e2)
    w2 = g_w * e2 / (1.0 + e2)
    ids_o[...] = jnp.where(lane_i == 0, i1, jnp.where(lane_i == 1, i2, 0.0)).astype(I32)
    wts_o[...] = jnp.where(lane_i == 0, w1, jnp.where(lane_i == 1, w2, 0.0))


def _merge(alpha, ys, bonus, g, ym, u, x, lnx_w, lnx_b, b_gate, w_br, w_bm, w_out, ln1_w, ln1_b,
           w_rg, b_rg, w_re, b_re):
    n, d = x.shape
    tm = MERGE_TM
    wr = jnp.zeros((d, LANES), F32).at[:, :N_EXPERTS].set(w_re).at[:, C_ROUTE_G:C_ROUTE_G + N_GROUPS].set(w_rg)
    wr_hi = wr.astype(BF16)
    wr_lo = (wr - wr_hi.astype(F32)).astype(BF16)
    b_r = jnp.zeros((1, LANES), F32).at[0, :N_EXPERTS].set(b_re).at[0, C_ROUTE_G:C_ROUTE_G + N_GROUPS].set(b_rg)
    tile = lambda w_: pl.BlockSpec((tm, w_), lambda i: (i, 0))
    const = lambda shape: pl.BlockSpec(shape, lambda i: (0, 0))
    return pl.pallas_call(
        functools.partial(_merge_kernel, alpha),
        grid=(n // tm,),
        in_specs=[tile(R_WIDTH), tile(R_WIDTH), tile(R_WIDTH), tile(M_WIDTH),
                  pl.BlockSpec((tm, d), lambda i: (i, C_GR // d)),
                  pl.BlockSpec((tm, d), lambda i: (i, C_GM // d)),
                  tile(d),
                  const((1, R_WIDTH)), const((1, R_WIDTH)), const((1, d)), const((1, d)),
                  const((R_WIDTH, d)), const((M_WIDTH, d)), const((d, d)), const((1, d)), const((1, d)),
                  const((d, LANES)), const((d, LANES)), const((1, LANES))],
        out_specs=[tile(d), tile(LANES), tile(LANES)],
        out_shape=[jax.ShapeDtypeStruct((n, d), F32), jax.ShapeDtypeStruct((n, LANES), I32),
                   jax.ShapeDtypeStruct((n, LANES), F32)],
        compiler_params=_cparams(("parallel",)),
        name="merge_ln1_router",
    )(ys, bonus, g, ym, u, u, x, lnx_w[None, :], lnx_b[None, :], b_gate[None, :d], b_gate[None, d:],
      w_br.astype(BF16), w_bm.astype(BF16), w_out.astype(BF16), ln1_w[None, :], ln1_b[None, :],
      wr_hi, wr_lo, b_r)


def _moe_kernel(blk_e, tok, slot, nvalid, x_hbm, wg_ref, wu_ref, wd_ref, out_hbm, xbuf, ybuf, sem_in, sem_out):
    i = pl.program_id(0)
    bm = xbuf.shape[0]
    nv = nvalid[i]

    def gather(r):
        return pltpu.make_async_copy(x_hbm.at[pl.ds(tok[i * bm + r], 1)], xbuf.at[pl.ds(r, 1)], sem_in)

    def scatter(r):
        return pltpu.make_async_copy(ybuf.at[pl.ds(r, 1)], out_hbm.at[pl.ds(slot[i * bm + r], 1)], sem_out)

    def each_row(count, fn):
        def body(r, carry):
            fn(r)
            return carry
        lax.fori_loop(0, count, body, 0)

    @pl.when(nv > 0)
    def _():
        each_row(bm, lambda r: gather(r).start())
        each_row(bm, lambda r: gather(r).wait())
        xb = xbuf[...].astype(BF16)
        gate = _dot(xb, wg_ref[0])
        hb = gate * _sigmoid(gate) * _dot(xb, wu_ref[0])
        ybuf[...] = _dot(hb.astype(BF16), wd_ref[0])
        each_row(nv, lambda r: scatter(r).start())
        each_row(nv, lambda r: scatter(r).wait())


def _moe_experts(x1, ids, w_gate, w_up, w_down):
    n, d = x1.shape
    de = w_gate.shape[-1]
    bm = MOE_BLOCK
    n_rows = TOP_K * n
    nb = -(-n_rows // bm) + N_EXPERTS
    eid = ids[:, :TOP_K].reshape(-1)
    onehot = (eid[:, None] == jnp.arange(N_EXPERTS, dtype=I32)[None, :]).astype(I32)
    csum = jnp.cumsum(onehot, axis=0)
    rank = jnp.take_along_axis(csum, eid[:, None], axis=1)[:, 0] - 1
    counts = csum[-1]
    padded = ((counts + bm - 1) // bm) * bm
    pend = jnp.cumsum(padded)
    pstart = pend - padded
    dest = pstart[eid] + rank
    tok_rows = jnp.zeros((nb * bm,), I32).at[dest].set(jnp.arange(n_rows, dtype=I32) // TOP_K)
    slot_rows = jnp.zeros((nb * bm,), I32).at[dest].set(jnp.arange(n_rows, dtype=I32))
    blk_start = jnp.arange(nb, dtype=I32) * bm
    blk_e = jnp.minimum(jnp.searchsorted(pend, blk_start, side='right'), N_EXPERTS - 1).astype(I32)
    nvalid = jnp.clip(counts[blk_e] - (blk_start - pstart[blk_e]), 0, bm).astype(I32)

    wspec = lambda shape: pl.BlockSpec(shape, lambda i, be, tk, sl, nu: (be[i], 0, 0))
    return pl.pallas_call(
        _moe_kernel,
        grid_spec=pltpu.PrefetchScalarGridSpec(
            num_scalar_prefetch=4,
            grid=(nb,),
            in_specs=[pl.BlockSpec(memory_space=pl.ANY),
                      wspec((1, d, de)), wspec((1, d, de)), wspec((1, de, d))],
            out_specs=pl.BlockSpec(memory_space=pl.ANY),
            scratch_shapes=[pltpu.VMEM((bm, d), F32), pltpu.VMEM((bm, d), F32),
                            pltpu.SemaphoreType.DMA, pltpu.SemaphoreType.DMA]),
        out_shape=jax.ShapeDtypeStruct((n_rows, d), F32),
        compiler_params=_cparams(("arbitrary",)),
        name="moe_experts",
    )(blk_e, tok_rows, slot_rows, nvalid, x1, w_gate.astype(BF16), w_up.astype(BF16), w_down.astype(BF16))


FINAL_TM = 256


def _final_kernel(alpha, x1_ref, y_ref, wts_ref, p_ref, wpg_ref, wple_ref, l2w_ref, l2b_ref, o_ref):
    x1 = x1_ref[...]
    d = x1.shape[1]
    ple = _sigmoid(_dot(x1.astype(BF16), wpg_ref[...])) * _dot(p_ref[...], wple_ref[...])
    moe = y_ref[:, :d] * wts_ref[:, 0:1] + y_ref[:, d:] * wts_ref[:, 1:2]
    o_ref[...] = _layer_norm(alpha * x1 + moe + ple, l2w_ref[...], l2b_ref[...])


def _final(alpha, x1, y2, wts, p_bf, w_pg, w_ple, ln2_w, ln2_b):
    n, d = x1.shape
    tm = FINAL_TM
    tile = lambda w_: pl.BlockSpec((tm, w_), lambda i: (i, 0))
    const = lambda shape: pl.BlockSpec(shape, lambda i: (0, 0))
    return pl.pallas_call(
        functools.partial(_final_kernel, alpha),
        grid=(n // tm,),
        in_specs=[tile(d), tile(TOP_K * d), tile(LANES), tile(p_bf.shape[1]),
                  const((d, d)), const((p_bf.shape[1], d)), const((1, d)), const((1, d))],
        out_specs=tile(d),
        out_shape=jax.ShapeDtypeStruct((n, d), F32),
        compiler_params=_cparams(("parallel",)),
        name="final_ln2",
    )(x1, y2, wts, p_bf, w_pg.astype(BF16), w_ple.astype(BF16), ln2_w[None, :], ln2_b[None, :])


def _regroup_w_in(w):
    m0 = RWKV_COLS
    g0 = RWKV_COLS + MLSTM_COLS
    mqk = 2 * M_HEADS * M_QK
    pad = lambda c: jnp.zeros((w.shape[0], c), w.dtype)
    parts = [w[:, 0:3 * R_WIDTH],
             w[:, m0:m0 + mqk],
             w[:, m0 + mqk:m0 + mqk + M_WIDTH],
             w[:, m0 + mqk + M_WIDTH + 2 * M_HEADS:m0 + MLSTM_COLS],
             w[:, g0:],
             w[:, 3 * R_WIDTH:RWKV_COLS], pad(LORA_PAD - LORA_COLS),
             w[:, m0 + mqk + M_WIDTH:m0 + mqk + M_WIDTH + 2 * M_HEADS], pad(LANES - 2 * M_HEADS)]
    out = jnp.concatenate(parts, axis=1)
    assert out.shape[1] == C_TOTAL
    return out


def kernel(x, p, w_in, mu_shift, w0, w_w2, a0, w_a2, w_g2, k_k, k_a, r_k, lnx_w, lnx_b, conv_w, conv_b,
           i_bias, f_bias, mh_w, b_gate, w_br, w_bm, w_out, ln1_w, ln1_b, w_rg, b_rg, w_re, b_re,
           w_gate, w_up, w_down, w_pg, w_ple, ln2_w, ln2_b):
    bsz, seq, d = x.shape
    depth = w_in.shape[0]
    assert bsz * R_HEADS * 2 == LANES and seq % PROJ_TM == 0 and seq % CHUNK == 0
    alpha = (2 * depth) ** 0.25
    n = bsz * seq
    xf = x.reshape(n, d)
    for i in range(depth):
        u = _proj_in(xf.astype(BF16), _regroup_w_in(w_in[i]).astype(BF16))
        r, dec, kp, v, kn, beta, g, bonus = _rwkv_prep(
            u, seq, mu_shift[i], w0[i], w_w2[i], a0[i], w_a2[i], w_g2[i], k_k[i], k_a[i], r_k[i])
        ks = lambda a: _to_scan_k(a, bsz, seq)
        y_scan = _rwkv_scan(ks(r), ks(dec), ks(kp), ks(kn), ks(beta), _to_scan_v(v, bsz, seq))
        ys = _from_scan_v(y_scan, bsz, seq)
        ym = _mlstm(u, bsz, seq, conv_w[i], conv_b[i], i_bias[i], f_bias[i], mh_w[i])
        x1, ids, wts = _merge(alpha, ys, bonus, g, ym, u, xf, lnx_w[i], lnx_b[i], b_gate[i], w_br[i], w_bm[i],
                              w_out[i], ln1_w[i], ln1_b[i], w_rg[i], b_rg[i], w_re[i], b_re[i])
        y2 = _moe_experts(x1, ids, w_gate[i], w_up[i], w_down[i])
        y2 = y2.reshape(n, TOP_K * d)
        xf = _final(alpha, x1, y2, wts, p[i].reshape(n, -1).astype(BF16), w_pg[i], w_ple[i], ln2_w[i], ln2_b[i])
    return xf.reshape(bsz, seq, d)
```

```python
import functools

import jax
import jax.numpy as jnp
from jax import lax
from jax.experimental import pallas as pl
from jax.experimental.pallas import tpu as pltpu

F32 = jnp.float32
BF16 = jnp.bfloat16
I32 = jnp.int32

R_HEADS, R_HEAD = 16, 64
R_WIDTH = R_HEADS * R_HEAD
DECAY_LORA, AAA_LORA, GATE_LORA = 64, 64, 160
LORA_COLS = DECAY_LORA + AAA_LORA + GATE_LORA
LORA_PAD = 512
R_GN_EPS = 64e-5
RWKV_COLS = 3 * R_WIDTH + LORA_COLS
M_HEADS, M_QK, M_V = 8, 64, 128
M_WIDTH = M_HEADS * M_V
CONV_K = 4
CHUNK = 128
M_NORM_EPS = 1e-6
MLSTM_COLS = 2 * M_HEADS * M_QK + 2 * M_WIDTH + 2 * M_HEADS
N_GROUPS, EXPERTS_PER_GROUP = 4, 8
N_EXPERTS = N_GROUPS * EXPERTS_PER_GROUP
TOP_K = 2
MOE_BLOCK = 128
LN_EPS = 1e-5

LANES = 128
SUBLANES = 8
V7X_VMEM_BYTES = 64 * 1024 * 1024
VMEM_LIMIT = 56 * 1024 * 1024

C_RKV = 0
C_MQK = 3072
C_MV = 4096
C_MO = 5120
C_GR = 6144
C_GM = 8192
C_LORA = 10240
C_MG = 10752
C_TOTAL = 10880
PROJ_TN = 2176
PROJ_TM = 512


def _cparams(sem, vmem=VMEM_LIMIT):
    return pltpu.CompilerParams(dimension_semantics=sem, vmem_limit_bytes=vmem)


def _sigmoid(x):
    return 1.0 / (1.0 + jnp.exp(-x))


def _softplus(x):
    return jnp.maximum(x, 0.0) + jnp.log1p(jnp.exp(-jnp.abs(x)))


def _split3(x):
    hi = x.astype(BF16)
    r1 = x - hi.astype(F32)
    mid = r1.astype(BF16)
    lo = (r1 - mid.astype(F32)).astype(BF16)
    return hi, mid, lo


def _dot(a, b):
    return jnp.dot(a, b, preferred_element_type=F32)


def _dot_nt(a, b):
    return lax.dot_general(a, b, (((1,), (1,)), ((), ())), preferred_element_type=F32)


def _dot_tn(a, b):
    return lax.dot_general(a, b, (((0,), (0,)), ((), ())), preferred_element_type=F32)


def _dot_exact_rhs(x, ones_bf16):
    hi, mid, lo = _split3(x)
    return _dot(hi, ones_bf16) + _dot(mid, ones_bf16) + _dot(lo, ones_bf16)


def _block_ones(n, group):
    r = lax.broadcasted_iota(I32, (n, n), 0) // group
    c = lax.broadcasted_iota(I32, (n, n), 1) // group
    return jnp.where(r == c, 1.0, 0.0).astype(BF16)


def _seg_sum(x, group):
    ones = _block_ones(LANES, group)
    slabs = [_dot_exact_rhs(x[:, p * LANES:(p + 1) * LANES], ones) for p in range(x.shape[1] // LANES)]
    return jnp.concatenate(slabs, axis=1)


def _layer_norm(x, w, b):
    mu = jnp.mean(x, axis=-1, keepdims=True)
    xc = x - mu
    var = jnp.mean(xc * xc, axis=-1, keepdims=True)
    return xc * lax.rsqrt(var + LN_EPS) * w + b


def _proj_kernel(x_ref, w_ref, o_ref):
    o_ref[...] = _dot(x_ref[...], w_ref[...])


def _proj_in(x_bf, w_bf):
    m, k = x_bf.shape
    n = w_bf.shape[1]
    return pl.pallas_call(
        _proj_kernel,
        grid=(n // PROJ_TN, m // PROJ_TM),
        in_specs=[pl.BlockSpec((PROJ_TM, k), lambda j, i: (i, 0)),
                  pl.BlockSpec((k, PROJ_TN), lambda j, i: (0, j))],
        out_specs=pl.BlockSpec((PROJ_TM, PROJ_TN), lambda j, i: (i, j)),
        out_shape=jax.ShapeDtypeStruct((m, n), F32),
        compiler_params=_cparams(("parallel", "parallel")),
        name="proj_in",
    )(x_bf, w_bf)


PREP_TM = 256


def _rwkv_prep_kernel(seq, u_ref, l_ref, up_ref, lp_ref, mu_ref, mul_ref, w0_ref, a0_ref, kk_ref, ka_ref,
                      rk_ref, ww_ref, wa_ref, wg_ref,
                      r_o, dec_o, kp_o, v_o, kn_o, beta_o, g_o, bonus_o):
    i = pl.program_id(0)
    tm = u_ref.shape[0]
    first = (i * tm) % seq == 0
    row = lax.broadcasted_iota(I32, (tm, 1), 0)

    def shift(u, prev8):
        prev_row = jnp.where(first, 0.0, prev8[SUBLANES - 1:SUBLANES, :])
        return jnp.where(row == 0, prev_row, pltpu.roll(u, 1, 0))

    u = u_ref[...]
    z = u + mu_ref[...] * (shift(u, up_ref[...]) - u)
    lo = l_ref[...]
    zl = lo + mul_ref[...] * (shift(lo, lp_ref[...]) - lo)

    r = z[:, 0:R_WIDTH]
    k = z[:, R_WIDTH:2 * R_WIDTH]
    v = z[:, 2 * R_WIDTH:3 * R_WIDTH]
    w_pre = w0_ref[...] + _dot(jnp.tanh(zl).astype(BF16), ww_ref[...])
    w = -_softplus(-w_pre) - 0.5
    dec = jnp.exp(-jnp.exp(w))
    a = _sigmoid(a0_ref[...] + _dot(zl.astype(BF16), wa_ref[...]))
    g = _dot(_sigmoid(zl).astype(BF16), wg_ref[...])

    kk = k * kk_ref[...]
    nrm = jnp.sqrt(_seg_sum(kk * kk, R_HEAD))
    kn = kk / jnp.maximum(nrm, 1e-12)
    kp = k * (1.0 + (a - 1.0) * ka_ref[...])
    bonus = _seg_sum(r * kp * rk_ref[...], R_HEAD) * v

    r_o[...] = r
    dec_o[...] = dec
    kp_o[...] = kp
    v_o[...] = v
    kn_o[...] = kn
    beta_o[...] = kn * a
    g_o[...] = g
    bonus_o[...] = bonus


def _rwkv_prep(u, seq, mu, w0, w_w2, a0, w_a2, w_g2, k_k, k_a, r_k):
    n = u.shape[0]
    tm = PREP_TM
    mu_rkv = mu[None, :3 * R_WIDTH]
    mu_lora = jnp.zeros((1, LORA_PAD), F32).at[0, :LORA_COLS].set(mu[3 * R_WIDTH:])
    ww = jnp.zeros((LORA_PAD, R_WIDTH), BF16).at[0:DECAY_LORA].set(w_w2.astype(BF16))
    wa = jnp.zeros((LORA_PAD, R_WIDTH), BF16).at[DECAY_LORA:DECAY_LORA + AAA_LORA].set(w_a2.astype(BF16))
    wg = jnp.zeros((LORA_PAD, R_WIDTH), BF16).at[DECAY_LORA + AAA_LORA:LORA_COLS].set(w_g2.astype(BF16))
    row = lambda t: t.reshape(1, R_WIDTH)
    prev_blk = lambda i: jnp.maximum(i * (tm // SUBLANES) - 1, 0)
    const = lambda shape: pl.BlockSpec(shape, lambda i: (0, 0))
    nat = jax.ShapeDtypeStruct((n, R_WIDTH), F32)
    return pl.pallas_call(
        functools.partial(_rwkv_prep_kernel, seq),
        grid=(n // tm,),
        in_specs=[pl.BlockSpec((tm, 3 * R_WIDTH), lambda i: (i, C_RKV // (3 * R_WIDTH))),
                  pl.BlockSpec((tm, LORA_PAD), lambda i: (i, C_LORA // LORA_PAD)),
                  pl.BlockSpec((SUBLANES, 3 * R_WIDTH), lambda i: (prev_blk(i), C_RKV // (3 * R_WIDTH))),
                  pl.BlockSpec((SUBLANES, LORA_PAD), lambda i: (prev_blk(i), C_LORA // LORA_PAD)),
                  const((1, 3 * R_WIDTH)), const((1, LORA_PAD)),
                  const((1, R_WIDTH)), const((1, R_WIDTH)), const((1, R_WIDTH)), const((1, R_WIDTH)),
                  const((1, R_WIDTH)),
                  const((LORA_PAD, R_WIDTH)), const((LORA_PAD, R_WIDTH)), const((LORA_PAD, R_WIDTH))],
        out_specs=[pl.BlockSpec((tm, R_WIDTH), lambda i: (i, 0))] * 8,
        out_shape=[nat] * 8,
        compiler_params=_cparams(("parallel",)),
        name="rwkv_prep",
    )(u, u, u, u, mu_rkv, mu_lora, row(w0), row(a0), row(k_k), row(k_a), row(r_k), ww, wa, wg)


SCAN_TB = 32
SCAN_V = R_HEAD // 2
SCAN_ACC = 4


def _rwkv_scan_kernel(r_ref, dec_ref, kp_ref, kn_ref, beta_ref, v_ref, y_ref, s_ref):
    @pl.when(pl.program_id(0) == 0)
    def _():
        s_ref[...] = jnp.zeros(s_ref.shape, F32)

    tb = r_ref.shape[0]

    def total(acc):
        return (acc[0] + acc[1]) + (acc[2] + acc[3])

    def add_term(acc, k, term):
        acc[k % SCAN_ACC] = term if acc[k % SCAN_ACC] is None else acc[k % SCAN_ACC] + term

    acc = [None] * SCAN_ACC
    for k in range(R_HEAD):
        add_term(acc, k, s_ref[k] * kn_ref[0, k:k + 1, :])

    def step(t, s_kk):
        v_t = v_ref[t]
        t_next = jnp.minimum(t + 1, tb - 1)
        acc_y = [None] * SCAN_ACC
        acc_s = [None] * SCAN_ACC
        for k in range(R_HEAD):
            s_new = (s_ref[k] * dec_ref[t, k:k + 1, :]
                     + (v_t * kp_ref[t, k:k + 1, :] - s_kk * beta_ref[t, k:k + 1, :]))
            s_ref[k] = s_new
            add_term(acc_y, k, s_new * r_ref[t, k:k + 1, :])
            add_term(acc_s, k, s_new * kn_ref[t_next, k:k + 1, :])
        y_ref[t] = total(acc_y)
        return total(acc_s)

    lax.fori_loop(0, tb, step, total(acc))


def _rwkv_scan(r, dec, kp, kn, beta, v):
    t = r.shape[0]
    kspec = pl.BlockSpec((SCAN_TB, R_HEAD, LANES), lambda i: (i, 0, 0))
    vspec = pl.BlockSpec((SCAN_TB, SCAN_V, LANES), lambda i: (i, 0, 0))
    return pl.pallas_call(
        _rwkv_scan_kernel,
        grid=(t // SCAN_TB,),
        in_specs=[kspec] * 5 + [vspec],
        out_specs=vspec,
        out_shape=jax.ShapeDtypeStruct((t, SCAN_V, LANES), F32),
        scratch_shapes=[pltpu.VMEM((R_HEAD, SCAN_V, LANES), F32)],
        compiler_params=_cparams(("arbitrary",)),
        name="rwkv_scan",
    )(r, dec, kp, kn, beta, v)


def _to_scan_k(a, bsz, seq):
    a = a.reshape(bsz, seq, R_HEADS, R_HEAD).transpose(1, 3, 0, 2).reshape(seq, R_HEAD, bsz * R_HEADS)
    return jnp.concatenate([a, a], axis=-1)


def _to_scan_v(a, bsz, seq):
    a = a.reshape(bsz, seq, R_HEADS, 2, SCAN_V).transpose(1, 4, 3, 0, 2)
    return a.reshape(seq, SCAN_V, 2 * bsz * R_HEADS)


def _from_scan_v(y, bsz, seq):
    y = y.reshape(seq, SCAN_V, 2, bsz, R_HEADS).transpose(3, 0, 4, 2, 1)
    return y.reshape(bsz * seq, R_WIDTH)


M_AUG = 2 * M_V


def _mlstm_kernel(qk_ref, v_ref, o_ref, g_ref, cw_ref, cb_ref, gb_ref, mh_ref, y_ref,
                  ext_ref, c_ref, m_ref):
    c_idx = pl.program_id(1)
    L = CHUNK

    @pl.when(c_idx == 0)
    def _():
        ext_ref[0:SUBLANES, :] = jnp.zeros((SUBLANES, ext_ref.shape[1]), F32)
        c_ref[...] = jnp.zeros(c_ref.shape, F32)
        m_ref[...] = jnp.full(m_ref.shape, -jnp.inf, F32)

    ext_ref[SUBLANES:SUBLANES + L, :] = qk_ref[...]
    conv = cb_ref[...]
    for j in range(CONV_K):
        off = SUBLANES - (CONV_K - 1) + j
        conv = conv + cw_ref[j:j + 1, :] * ext_ref[off:off + L, :]
    ext_ref[0:SUBLANES, :] = ext_ref[L:L + SUBLANES, :]
    qk = conv * _sigmoid(conv)
    q_all = qk[:, :M_HEADS * M_QK].astype(BF16)
    k_all = qk[:, M_HEADS * M_QK:] * (M_QK ** -0.5)

    gpre = g_ref[...] + gb_ref[...]
    lane = lax.broadcasted_iota(I32, (L, LANES), 1)
    gcols = jnp.where(lane < M_HEADS, gpre, -_softplus(-gpre))
    grows = gcols.T
    ti = lax.broadcasted_iota(I32, (L, L), 0)
    si = lax.broadcasted_iota(I32, (L, L), 1)
    causal = si <= ti
    ltri = jnp.where(causal, 1.0, 0.0).astype(BF16)
    utri = jnp.where(ti <= si, 1.0, 0.0).astype(BF16)
    acols = _dot_exact_rhs_left(ltri, gcols)
    arows = _dot_exact_rhs(grows, utri)
    ones_col = jnp.where(lane == 0, 1.0, 0.0).astype(BF16)

    for h in range(M_HEADS):
        a_col = acols[:, M_HEADS + h:M_HEADS + h + 1]
        a_row = arows[M_HEADS + h:M_HEADS + h + 1, :]
        i_col = gcols[:, h:h + 1]
        i_row = grows[h:h + 1, :]
        m_st = m_ref[h:h + 1, 0:1]
        d = jnp.where(causal, a_col - a_row + i_row, -jnp.inf)
        inter = a_col + m_st
        m_t = jnp.maximum(inter, jnp.max(d, axis=-1, keepdims=True))
        q = q_all[:, h * M_QK:(h + 1) * M_QK]
        k = k_all[:, h * M_QK:(h + 1) * M_QK]
        v_aug = jnp.concatenate([v_ref[:, h * M_V:(h + 1) * M_V].astype(BF16), ones_col], axis=1)
        s = _dot_nt(q, k.astype(BF16)) * jnp.exp(d - m_t)
        ie = jnp.exp(inter - m_t)
        tot = ie * _dot(q, c_ref[h].astype(BF16)) + _dot(s.astype(BF16), v_aug)
        num = tot[:, :M_V]
        den = tot[:, M_V:M_V + 1]
        h_c = num / jnp.maximum(jnp.abs(den), jnp.exp(-m_t))

        a_tot = a_col[L - 1:L, :]
        gl = a_tot - a_col + i_col
        m_new = jnp.maximum(a_tot + m_st, jnp.max(gl, axis=0, keepdims=True))
        sc = jnp.exp(a_tot + m_st - m_new)
        ge = jnp.exp(gl - m_new)
        c_ref[h] = sc * c_ref[h] + _dot_tn((k * ge).astype(BF16), v_aug)
        m_ref[h:h + 1, :] = jnp.broadcast_to(m_new, (1, LANES))

        mu = jnp.mean(h_c, axis=-1, keepdims=True)
        hc = h_c - mu
        var = jnp.mean(hc * hc, axis=-1, keepdims=True)
        hn = hc * lax.rsqrt(var + M_NORM_EPS) * mh_ref[:, h * M_V:(h + 1) * M_V]
        y_ref[:, h * M_V:(h + 1) * M_V] = _sigmoid(o_ref[:, h * M_V:(h + 1) * M_V]) * hn


def _dot_exact_rhs_left(ones_bf16, x):
    hi, mid, lo = _split3(x)
    return _dot(ones_bf16, hi) + _dot(ones_bf16, mid) + _dot(ones_bf16, lo)


def _mlstm(u, bsz, seq, conv_w, conv_b, i_bias, f_bias, mh_w):
    nc = seq // CHUNK
    w = M_WIDTH
    gbias = jnp.zeros((1, LANES), F32).at[0, :M_HEADS].set(i_bias).at[0, M_HEADS:2 * M_HEADS].set(f_bias)
    blk = lambda col: pl.BlockSpec((CHUNK, w), lambda b, c: (b * nc + c, col // w))
    const = lambda shape: pl.BlockSpec(shape, lambda b, c: (0, 0))
    return pl.pallas_call(
        _mlstm_kernel,
        grid=(bsz, nc),
        in_specs=[blk(C_MQK), blk(C_MV), blk(C_MO),
                  pl.BlockSpec((CHUNK, LANES), lambda b, c: (b * nc + c, C_MG // LANES)),
                  const((CONV_K, w)), const((1, w)), const((1, LANES)), const((1, w))],
        out_specs=pl.BlockSpec((CHUNK, w), lambda b, c: (b * nc + c, 0)),
        out_shape=jax.ShapeDtypeStruct((bsz * seq, w), F32),
        scratch_shapes=[pltpu.VMEM((CHUNK + SUBLANES, w), F32),
                        pltpu.VMEM((M_HEADS, M_QK, M_AUG), F32),
                        pltpu.VMEM((M_HEADS, LANES), F32)],
        compiler_params=_cparams(("parallel", "arbitrary")),
        name="mlstm",
    )(u, u, u, u, conv_w, conv_b[None, :], gbias, mh_w[None, :])


MERGE_TM = 256
C_ROUTE_G = N_EXPERTS


def _merge_kernel(alpha, ys_ref, bonus_ref, g_ref, ym_ref, gr_ref, gm_ref, x_ref,
                  lnxw_ref, lnxb_ref, bgr_ref, bgm_ref, wbr_ref, wbm_ref, wout_ref, l1w_ref, l1b_ref,
                  wrh_ref, wrl_ref, br_ref, x1_o, ids_o, wts_o, cnt_o):
    ys = ys_ref[...]
    mu = _seg_sum(ys, R_HEAD) * (1.0 / R_HEAD)
    yc = ys - mu
    var = _seg_sum(yc * yc, R_HEAD) * (1.0 / R_HEAD)
    y = yc * lax.rsqrt(var + R_GN_EPS) * lnxw_ref[...] + lnxb_ref[...]
    y_r = (y + bonus_ref[...]) * g_ref[...]
    br = _dot(y_r.astype(BF16), wbr_ref[...])
    bm = _dot(ym_ref[...].astype(BF16), wbm_ref[...])
    mix_in = _sigmoid(gr_ref[...] + bgr_ref[...]) * br + _sigmoid(gm_ref[...] + bgm_ref[...]) * bm
    mix = _dot(mix_in.astype(BF16), wout_ref[...])
    x1 = _layer_norm(alpha * x_ref[...] + mix, l1w_ref[...], l1b_ref[...])
    x1_o[...] = x1

    xh = x1.astype(BF16)
    xl = (x1 - xh.astype(F32)).astype(BF16)
    logits = (_dot(xh, wrh_ref[...]) + (_dot(xh, wrl_ref[...]) + _dot(xl, wrh_ref[...]))) + br_ref[...]
    tm = logits.shape[0]
    lane_i = lax.broadcasted_iota(I32, (tm, LANES), 1)
    lane = lane_i.astype(F32)
    group_of_lane = (lane_i // EXPERTS_PER_GROUP).astype(F32)
    big = float(LANES)
    neg = -jnp.inf
    lg = jnp.where((lane_i >= C_ROUTE_G) & (lane_i < C_ROUTE_G + N_GROUPS), logits, neg)
    gmax = jnp.max(lg, axis=-1, keepdims=True)
    gsel = jnp.min(jnp.where(lg == gmax, lane - C_ROUTE_G, big), axis=-1, keepdims=True)
    g_w = 1.0 / jnp.sum(jnp.exp(lg - gmax), axis=-1, keepdims=True)
    le = jnp.where((lane_i < N_EXPERTS) & (group_of_lane == gsel), logits, neg)
    m1 = jnp.max(le, axis=-1, keepdims=True)
    i1 = jnp.min(jnp.where(le == m1, lane, big), axis=-1, keepdims=True)
    le2 = jnp.where(lane == i1, neg, le)
    m2 = jnp.max(le2, axis=-1, keepdims=True)
    i2 = jnp.min(jnp.where(le2 == m2, lane, big), axis=-1, keepdims=True)
    e2 = jnp.exp(m2 - m1)
    w1 = g_w / (1.0 + e2)
    w2 = g_w * e2 / (1.0 + e2)
    wts_o[...] = jnp.where(lane_i == 0, w1, jnp.where(lane_i == 1, w2, 0.0))

    oh1 = jnp.where(lane == i1, 1.0, 0.0)
    oh2 = jnp.where(lane == i2, 1.0, 0.0)
    ri = lax.broadcasted_iota(I32, (tm, tm), 0)
    ci = lax.broadcasted_iota(I32, (tm, tm), 1)
    lstrict = jnp.where(ci < ri, 1.0, 0.0).astype(BF16)
    tot1 = jnp.sum(oh1, axis=0, keepdims=True)
    tot2 = jnp.sum(oh2, axis=0, keepdims=True)
    rank1 = jnp.sum(_dot(lstrict, oh1.astype(BF16)) * oh1, axis=-1, keepdims=True)
    rank2 = jnp.sum((_dot(lstrict, oh2.astype(BF16)) + tot1) * oh2, axis=-1, keepdims=True)
    ids = jnp.where(lane_i == 0, i1, jnp.where(lane_i == 1, i2, jnp.where(lane_i == 2, rank1,
                                                                          jnp.where(lane_i == 3, rank2, 0.0))))
    ids_o[...] = ids.astype(I32)
    cnt_o[...] = jnp.broadcast_to(tot1 + tot2, cnt_o.shape).astype(I32)


def _merge(alpha, ys, bonus, g, ym, u, x, lnx_w, lnx_b, b_gate, w_br, w_bm, w_out, ln1_w, ln1_b,
           w_rg, b_rg, w_re, b_re):
    n, d = x.shape
    tm = MERGE_TM
    wr = jnp.zeros((d, LANES), F32).at[:, :N_EXPERTS].set(w_re).at[:, C_ROUTE_G:C_ROUTE_G + N_GROUPS].set(w_rg)
    wr_hi = wr.astype(BF16)
    wr_lo = (wr - wr_hi.astype(F32)).astype(BF16)
    b_r = jnp.zeros((1, LANES), F32).at[0, :N_EXPERTS].set(b_re).at[0, C_ROUTE_G:C_ROUTE_G + N_GROUPS].set(b_rg)
    tile = lambda w_: pl.BlockSpec((tm, w_), lambda i: (i, 0))
    const = lambda shape: pl.BlockSpec(shape, lambda i: (0, 0))
    return pl.pallas_call(
        functools.partial(_merge_kernel, alpha),
        grid=(n // tm,),
        in_specs=[tile(R_WIDTH), tile(R_WIDTH), tile(R_WIDTH), tile(M_WIDTH),
                  pl.BlockSpec((tm, d), lambda i: (i, C_GR // d)),
                  pl.BlockSpec((tm, d), lambda i: (i, C_GM // d)),
                  tile(d),
                  const((1, R_WIDTH)), const((1, R_WIDTH)), const((1, d)), const((1, d)),
                  const((R_WIDTH, d)), const((M_WIDTH, d)), const((d, d)), const((1, d)), const((1, d)),
                  const((d, LANES)), const((d, LANES)), const((1, LANES))],
        out_specs=[tile(d), tile(LANES), tile(LANES), pl.BlockSpec((SUBLANES, LANES), lambda i: (i, 0))],
        out_shape=[jax.ShapeDtypeStruct((n, d), F32), jax.ShapeDtypeStruct((n, LANES), I32),
                   jax.ShapeDtypeStruct((n, LANES), F32),
                   jax.ShapeDtypeStruct((n // tm * SUBLANES, LANES), I32)],
        compiler_params=_cparams(("parallel",)),
        name="merge_ln1_router",
    )(ys, bonus, g, ym, u, u, x, lnx_w[None, :], lnx_b[None, :], b_gate[None, :d], b_gate[None, d:],
      w_br.astype(BF16), w_bm.astype(BF16), w_out.astype(BF16), ln1_w[None, :], ln1_b[None, :],
      wr_hi, wr_lo, b_r)


MOE_CH = 256
DISPATCH_TM = 256


def _each(count, fn):
    def body(r, carry):
        fn(r)
        return carry
    lax.fori_loop(0, count, body, 0)


def _zero_unused_tail(pstart, nchunks, zeros_ref, out_hbm, sem):
    ch = zeros_ref.shape[0]
    used = (pstart[N_EXPERTS - 1] + nchunks[N_EXPERTS - 1] * ch) // ch
    total = out_hbm.shape[0] // ch

    def chunk(c):
        return pltpu.make_async_copy(zeros_ref, out_hbm.at[pl.ds(pl.multiple_of(c * ch, ch), ch)], sem)

    def over_tail(fn):
        def body(c, carry):
            fn(c)
            return carry
        lax.fori_loop(used, total, body, 0)

    over_tail(lambda c: chunk(c).start())
    over_tail(lambda c: chunk(c).wait())


def _moe_dispatch_kernel(dest, pstart, nchunks, x_ref, xs_hbm, zbuf, sem_z, sem):
    i = pl.program_id(0)
    tm = x_ref.shape[0]
    ch = zbuf.shape[0]

    @pl.when(i == 0)
    def _():
        zbuf[...] = jnp.zeros(zbuf.shape, F32)

        def tail(e):
            row0 = pl.multiple_of(pstart[e] + (nchunks[e] - 1) * ch, ch)
            return pltpu.make_async_copy(zbuf, xs_hbm.at[pl.ds(row0, ch)], sem_z)

        def start(e):
            @pl.when(nchunks[e] > 0)
            def _():
                tail(e).start()

        def wait(e):
            @pl.when(nchunks[e] > 0)
            def _():
                tail(e).wait()

        _each(N_EXPERTS, start)
        _each(N_EXPERTS, wait)
        _zero_unused_tail(pstart, nchunks, zbuf, xs_hbm, sem_z)

    def row(r, j):
        return pltpu.make_async_copy(x_ref.at[pl.ds(r, 1)],
                                     xs_hbm.at[pl.ds(dest[(i * tm + r) * TOP_K + j], 1)], sem)

    for j in range(TOP_K):
        _each(tm, lambda r: row(r, j).start())
    for j in range(TOP_K):
        _each(tm, lambda r: row(r, j).wait())


def _moe_dispatch(x1, dest, pstart, nchunks, rows_pad):
    n, d = x1.shape
    tm = DISPATCH_TM
    return pl.pallas_call(
        _moe_dispatch_kernel,
        grid_spec=pltpu.PrefetchScalarGridSpec(
            num_scalar_prefetch=3,
            grid=(n // tm,),
            in_specs=[pl.BlockSpec((tm, d), lambda i, *_: (i, 0))],
            out_specs=pl.BlockSpec(memory_space=pl.ANY),
            scratch_shapes=[pltpu.VMEM((MOE_CH, d), F32), pltpu.SemaphoreType.DMA, pltpu.SemaphoreType.DMA]),
        out_shape=jax.ShapeDtypeStruct((rows_pad, d), F32),
        compiler_params=_cparams(("arbitrary",)),
        name="moe_dispatch",
    )(dest, pstart, nchunks, x1)


def _moe_expert_kernel(pstart, nchunks, xs_hbm, wg_ref, wu_ref, wd_ref, ys_hbm,
                       xbuf, ybuf, wgb, wub, wdb, sem_in, sem_out):
    e = pl.program_id(0)
    ch = xbuf.shape[1]
    nc = nchunks[e]
    base = pstart[e]

    def rows(c):
        return pl.ds(pl.multiple_of(base + c * ch, ch), ch)

    def load(c, slot):
        return pltpu.make_async_copy(xs_hbm.at[rows(c)], xbuf.at[slot], sem_in.at[slot])

    def store(c, slot):
        return pltpu.make_async_copy(ybuf.at[slot], ys_hbm.at[rows(c)], sem_out.at[slot])

    @pl.when(nc > 0)
    def _():
        load(0, 0).start()
        wgb[...] = wg_ref[0].astype(BF16)
        wub[...] = wu_ref[0].astype(BF16)
        wdb[...] = wd_ref[0].astype(BF16)

        def chunk(c, carry):
            slot = c % 2

            @pl.when(c + 1 < nc)
            def _():
                load(c + 1, 1 - slot).start()

            load(c, slot).wait()

            @pl.when(c >= 2)
            def _():
                store(c - 2, slot).wait()

            xb = xbuf[slot].astype(BF16)
            gate = _dot(xb, wgb[...])
            hb = gate * _sigmoid(gate) * _dot(xb, wub[...])
            ybuf[slot] = _dot(hb.astype(BF16), wdb[...])
            store(c, slot).start()
            return carry

        lax.fori_loop(0, nc, chunk, 0)

        @pl.when(nc >= 2)
        def _():
            store(nc - 2, nc % 2).wait()

        store(nc - 1, (nc - 1) % 2).wait()

    @pl.when(e == pl.num_programs(0) - 1)
    def _():
        ybuf[0] = jnp.zeros(ybuf.shape[1:], F32)
        _zero_unused_tail(pstart, nchunks, ybuf.at[0], ys_hbm, sem_out.at[0])


def _moe_experts(xs, pstart, nchunks, w_gate, w_up, w_down):
    rows_pad, d = xs.shape
    de = w_gate.shape[-1]
    wspec = lambda shape: pl.BlockSpec(shape, lambda e, *_: (e, 0, 0))
    return pl.pallas_call(
        _moe_expert_kernel,
        grid_spec=pltpu.PrefetchScalarGridSpec(
            num_scalar_prefetch=2,
            grid=(N_EXPERTS,),
            in_specs=[pl.BlockSpec(memory_space=pl.ANY),
                      wspec((1, d, de)), wspec((1, d, de)), wspec((1, de, d))],
            out_specs=pl.BlockSpec(memory_space=pl.ANY),
            scratch_shapes=[pltpu.VMEM((2, MOE_CH, d), F32), pltpu.VMEM((2, MOE_CH, d), F32),
                            pltpu.VMEM((d, de), BF16), pltpu.VMEM((d, de), BF16), pltpu.VMEM((de, d), BF16),
                            pltpu.SemaphoreType.DMA((2,)), pltpu.SemaphoreType.DMA((2,))]),
        out_shape=jax.ShapeDtypeStruct((rows_pad, d), F32),
        compiler_params=_cparams(("arbitrary",)),
        name="moe_experts",
    )(pstart, nchunks, xs, w_gate, w_up, w_down)


def _moe_plan(ids, cnt, tm):
    n = ids.shape[0]
    tile_cnt = cnt[::SUBLANES, :N_EXPERTS]
    counts = jnp.sum(tile_cnt, axis=0)
    padded = ((counts + MOE_CH - 1) // MOE_CH) * MOE_CH
    pstart = jnp.cumsum(padded) - padded
    tile_base = pstart[None, :] + jnp.cumsum(tile_cnt, axis=0) - tile_cnt
    tile_of_tok = jnp.arange(n, dtype=I32) // tm
    eid = ids[:, 0:TOP_K]
    rank = ids[:, TOP_K:2 * TOP_K]
    dest = tile_base[tile_of_tok[:, None], eid] + rank
    return dest.reshape(-1).astype(I32), pstart.astype(I32), (padded // MOE_CH).astype(I32)


FINAL_TM = 256


def _final_kernel(alpha, dest, x1_ref, wts_ref, p_ref, wpg_ref, wple_ref, l2w_ref, l2b_ref, ys_hbm, o_ref,
                  ybuf, sem):
    i = pl.program_id(0)
    tm = x1_ref.shape[0]

    def row(r, j):
        return pltpu.make_async_copy(ys_hbm.at[pl.ds(dest[(i * tm + r) * TOP_K + j], 1)],
                                     ybuf.at[j, pl.ds(r, 1)], sem)

    for j in range(TOP_K):
        _each(tm, lambda r: row(r, j).start())
    x1 = x1_ref[...]
    ple = _sigmoid(_dot(x1.astype(BF16), wpg_ref[...])) * _dot(p_ref[...], wple_ref[...])
    for j in range(TOP_K):
        _each(tm, lambda r: row(r, j).wait())
    moe = ybuf[0] * wts_ref[:, 0:1] + ybuf[1] * wts_ref[:, 1:2]
    o_ref[...] = _layer_norm(alpha * x1 + moe + ple, l2w_ref[...], l2b_ref[...])


def _final(alpha, dest, x1, ys, wts, p_bf, w_pg, w_ple, ln2_w, ln2_b):
    n, d = x1.shape
    tm = FINAL_TM
    tile = lambda w_: pl.BlockSpec((tm, w_), lambda i, *_: (i, 0))
    const = lambda shape: pl.BlockSpec(shape, lambda i, *_: (0, 0))
    return pl.pallas_call(
        functools.partial(_final_kernel, alpha),
        grid_spec=pltpu.PrefetchScalarGridSpec(
            num_scalar_prefetch=1,
            grid=(n // tm,),
            in_specs=[tile(d), tile(LANES), tile(p_bf.shape[1]),
                      const((d, d)), const((p_bf.shape[1], d)), const((1, d)), const((1, d)),
                      pl.BlockSpec(memory_space=pl.ANY)],
            out_specs=tile(d),
            scratch_shapes=[pltpu.VMEM((TOP_K, tm, d), F32), pltpu.SemaphoreType.DMA]),
        out_shape=jax.ShapeDtypeStruct((n, d), F32),
        compiler_params=_cparams(("arbitrary",)),
        name="final_ln2",
    )(dest, x1, wts, p_bf, w_pg.astype(BF16), w_ple.astype(BF16), ln2_w[None, :], ln2_b[None, :], ys)


def _regroup_w_in(w):
    m0 = RWKV_COLS
    g0 = RWKV_COLS + MLSTM_COLS
    mqk = 2 * M_HEADS * M_QK
    pad = lambda c: jnp.zeros((w.shape[0], c), w.dtype)
    parts = [w[:, 0:3 * R_WIDTH],
             w[:, m0:m0 + mqk],
             w[:, m0 + mqk:m0 + mqk + M_WIDTH],
             w[:, m0 + mqk + M_WIDTH + 2 * M_HEADS:m0 + MLSTM_COLS],
             w[:, g0:],
             w[:, 3 * R_WIDTH:RWKV_COLS], pad(LORA_PAD - LORA_COLS),
             w[:, m0 + mqk + M_WIDTH:m0 + mqk + M_WIDTH + 2 * M_HEADS], pad(LANES - 2 * M_HEADS)]
    out = jnp.concatenate(parts, axis=1)
    assert out.shape[1] == C_TOTAL
    return out


def kernel(x, p, w_in, mu_shift, w0, w_w2, a0, w_a2, w_g2, k_k, k_a, r_k, lnx_w, lnx_b, conv_w, conv_b,
           i_bias, f_bias, mh_w, b_gate, w_br, w_bm, w_out, ln1_w, ln1_b, w_rg, b_rg, w_re, b_re,
           w_gate, w_up, w_down, w_pg, w_ple, ln2_w, ln2_b):
    bsz, seq, d = x.shape
    depth = w_in.shape[0]
    assert bsz * R_HEADS * 2 == LANES and seq % PROJ_TM == 0 and seq % CHUNK == 0
    alpha = (2 * depth) ** 0.25
    n = bsz * seq
    xf = x.reshape(n, d)
    for i in range(depth):
        u = _proj_in(xf.astype(BF16), _regroup_w_in(w_in[i]).astype(BF16))
        r, dec, kp, v, kn, beta, g, bonus = _rwkv_prep(
            u, seq, mu_shift[i], w0[i], w_w2[i], a0[i], w_a2[i], w_g2[i], k_k[i], k_a[i], r_k[i])
        ks = lambda a: _to_scan_k(a, bsz, seq)
        y_scan = _rwkv_scan(ks(r), ks(dec), ks(kp), ks(kn), ks(beta), _to_scan_v(v, bsz, seq))
        ys = _from_scan_v(y_scan, bsz, seq)
        ym = _mlstm(u, bsz, seq, conv_w[i], conv_b[i], i_bias[i], f_bias[i], mh_w[i])
        x1, ids, wts, cnt = _merge(alpha, ys, bonus, g, ym, u, xf, lnx_w[i], lnx_b[i], b_gate[i], w_br[i],
                                   w_bm[i], w_out[i], ln1_w[i], ln1_b[i], w_rg[i], b_rg[i], w_re[i], b_re[i])
        dest, pstart, nchunks = _moe_plan(ids, cnt, MERGE_TM)
        rows_pad = TOP_K * n + N_EXPERTS * MOE_CH
        xs = _moe_dispatch(x1, dest, pstart, nchunks, rows_pad)
        ys_sorted = _moe_experts(xs, pstart, nchunks, w_gate[i], w_up[i], w_down[i])
        xf = _final(alpha, dest, x1, ys_sorted, wts, p[i].reshape(n, -1).astype(BF16), w_pg[i], w_ple[i],
                    ln2_w[i], ln2_b[i])
    return xf.reshape(bsz, seq, d)
```

```python
import functools

import jax
import jax.numpy as jnp
from jax import lax
from jax.experimental import pallas as pl
from jax.experimental.pallas import tpu as pltpu

F32 = jnp.float32
BF16 = jnp.bfloat16
I32 = jnp.int32

R_HEADS, R_HEAD = 16, 64
R_WIDTH = R_HEADS * R_HEAD
DECAY_LORA, AAA_LORA, GATE_LORA = 64, 64, 160
LORA_COLS = DECAY_LORA + AAA_LORA + GATE_LORA
LORA_PAD = 512
R_GN_EPS = 64e-5
RWKV_COLS = 3 * R_WIDTH + LORA_COLS
M_HEADS, M_QK, M_V = 8, 64, 128
M_WIDTH = M_HEADS * M_V
CONV_K = 4
CHUNK = 128
M_NORM_EPS = 1e-6
MLSTM_COLS = 2 * M_HEADS * M_QK + 2 * M_WIDTH + 2 * M_HEADS
N_GROUPS, EXPERTS_PER_GROUP = 4, 8
N_EXPERTS = N_GROUPS * EXPERTS_PER_GROUP
TOP_K = 2
MOE_BLOCK = 128
LN_EPS = 1e-5

LANES = 128
SUBLANES = 8
V7X_VMEM_BYTES = 64 * 1024 * 1024
VMEM_LIMIT = 56 * 1024 * 1024

C_RKV = 0
C_MQK = 3072
C_MV = 4096
C_MO = 5120
C_GR = 6144
C_GM = 8192
C_LORA = 10240
C_MG = 10752
C_TOTAL = 10880
PROJ_TN = 2176
PROJ_TM = 512


def _cparams(sem, vmem=VMEM_LIMIT):
    return pltpu.CompilerParams(dimension_semantics=sem, vmem_limit_bytes=vmem)


def _sigmoid(x):
    return 1.0 / (1.0 + jnp.exp(-x))


def _softplus(x):
    return jnp.maximum(x, 0.0) + jnp.log1p(jnp.exp(-jnp.abs(x)))


def _split3(x):
    hi = x.astype(BF16)
    r1 = x - hi.astype(F32)
    mid = r1.astype(BF16)
    lo = (r1 - mid.astype(F32)).astype(BF16)
    return hi, mid, lo


def _dot(a, b):
    return jnp.dot(a, b, preferred_element_type=F32)


def _dot_nt(a, b):
    return lax.dot_general(a, b, (((1,), (1,)), ((), ())), preferred_element_type=F32)


def _dot_tn(a, b):
    return lax.dot_general(a, b, (((0,), (0,)), ((), ())), preferred_element_type=F32)


def _dot_exact_rhs(x, ones_bf16):
    hi, mid, lo = _split3(x)
    return _dot(hi, ones_bf16) + _dot(mid, ones_bf16) + _dot(lo, ones_bf16)


def _block_ones(n, group):
    r = lax.broadcasted_iota(I32, (n, n), 0) // group
    c = lax.broadcasted_iota(I32, (n, n), 1) // group
    return jnp.where(r == c, 1.0, 0.0).astype(BF16)


def _seg_sum(x, group):
    ones = _block_ones(LANES, group)
    slabs = [_dot_exact_rhs(x[:, p * LANES:(p + 1) * LANES], ones) for p in range(x.shape[1] // LANES)]
    return jnp.concatenate(slabs, axis=1)


def _layer_norm(x, w, b):
    mu = jnp.mean(x, axis=-1, keepdims=True)
    xc = x - mu
    var = jnp.mean(xc * xc, axis=-1, keepdims=True)
    return xc * lax.rsqrt(var + LN_EPS) * w + b


def _proj_kernel(x_ref, w_ref, o_ref):
    o_ref[...] = _dot(x_ref[...], w_ref[...])


def _proj_in(x_bf, w_bf):
    m, k = x_bf.shape
    n = w_bf.shape[1]
    return pl.pallas_call(
        _proj_kernel,
        grid=(n // PROJ_TN, m // PROJ_TM),
        in_specs=[pl.BlockSpec((PROJ_TM, k), lambda j, i: (i, 0)),
                  pl.BlockSpec((k, PROJ_TN), lambda j, i: (0, j))],
        out_specs=pl.BlockSpec((PROJ_TM, PROJ_TN), lambda j, i: (i, j)),
        out_shape=jax.ShapeDtypeStruct((m, n), F32),
        compiler_params=_cparams(("parallel", "parallel")),
        name="proj_in",
    )(x_bf, w_bf)


PREP_TM = 256


def _rwkv_prep_kernel(seq, u_ref, l_ref, up_ref, lp_ref, mu_ref, mul_ref, w0_ref, a0_ref, kk_ref, ka_ref,
                      rk_ref, ww_ref, wa_ref, wg_ref,
                      r_o, dec_o, kp_o, v_o, kn_o, beta_o, g_o, bonus_o):
    i = pl.program_id(0)
    tm = u_ref.shape[0]
    first = (i * tm) % seq == 0
    row = lax.broadcasted_iota(I32, (tm, 1), 0)

    def shift(u, prev8):
        prev_row = jnp.where(first, 0.0, prev8[SUBLANES - 1:SUBLANES, :])
        return jnp.where(row == 0, prev_row, pltpu.roll(u, 1, 0))

    u = u_ref[...]
    z = u + mu_ref[...] * (shift(u, up_ref[...]) - u)
    lo = l_ref[...]
    zl = lo + mul_ref[...] * (shift(lo, lp_ref[...]) - lo)

    r = z[:, 0:R_WIDTH]
    k = z[:, R_WIDTH:2 * R_WIDTH]
    v = z[:, 2 * R_WIDTH:3 * R_WIDTH]
    w_pre = w0_ref[...] + _dot(jnp.tanh(zl).astype(BF16), ww_ref[...])
    w = -_softplus(-w_pre) - 0.5
    dec = jnp.exp(-jnp.exp(w))
    a = _sigmoid(a0_ref[...] + _dot(zl.astype(BF16), wa_ref[...]))
    g = _dot(_sigmoid(zl).astype(BF16), wg_ref[...])

    kk = k * kk_ref[...]
    nrm = jnp.sqrt(_seg_sum(kk * kk, R_HEAD))
    kn = kk / jnp.maximum(nrm, 1e-12)
    kp = k * (1.0 + (a - 1.0) * ka_ref[...])
    bonus = _seg_sum(r * kp * rk_ref[...], R_HEAD) * v

    r_o[...] = r
    dec_o[...] = dec
    kp_o[...] = kp
    v_o[...] = v
    kn_o[...] = kn
    beta_o[...] = kn * a
    g_o[...] = g
    bonus_o[...] = bonus


def _rwkv_prep(u, seq, mu, w0, w_w2, a0, w_a2, w_g2, k_k, k_a, r_k):
    n = u.shape[0]
    tm = PREP_TM
    mu_rkv = mu[None, :3 * R_WIDTH]
    mu_lora = jnp.zeros((1, LORA_PAD), F32).at[0, :LORA_COLS].set(mu[3 * R_WIDTH:])
    ww = jnp.zeros((LORA_PAD, R_WIDTH), BF16).at[0:DECAY_LORA].set(w_w2.astype(BF16))
    wa = jnp.zeros((LORA_PAD, R_WIDTH), BF16).at[DECAY_LORA:DECAY_LORA + AAA_LORA].set(w_a2.astype(BF16))
    wg = jnp.zeros((LORA_PAD, R_WIDTH), BF16).at[DECAY_LORA + AAA_LORA:LORA_COLS].set(w_g2.astype(BF16))
    row = lambda t: t.reshape(1, R_WIDTH)
    prev_blk = lambda i: jnp.maximum(i * (tm // SUBLANES) - 1, 0)
    const = lambda shape: pl.BlockSpec(shape, lambda i: (0, 0))
    nat = jax.ShapeDtypeStruct((n, R_WIDTH), F32)
    return pl.pallas_call(
        functools.partial(_rwkv_prep_kernel, seq),
        grid=(n // tm,),
        in_specs=[pl.BlockSpec((tm, 3 * R_WIDTH), lambda i: (i, C_RKV // (3 * R_WIDTH))),
                  pl.BlockSpec((tm, LORA_PAD), lambda i: (i, C_LORA // LORA_PAD)),
                  pl.BlockSpec((SUBLANES, 3 * R_WIDTH), lambda i: (prev_blk(i), C_RKV // (3 * R_WIDTH))),
                  pl.BlockSpec((SUBLANES, LORA_PAD), lambda i: (prev_blk(i), C_LORA // LORA_PAD)),
                  const((1, 3 * R_WIDTH)), const((1, LORA_PAD)),
                  const((1, R_WIDTH)), const((1, R_WIDTH)), const((1, R_WIDTH)), const((1, R_WIDTH)),
                  const((1, R_WIDTH)),
                  const((LORA_PAD, R_WIDTH)), const((LORA_PAD, R_WIDTH)), const((LORA_PAD, R_WIDTH))],
        out_specs=[pl.BlockSpec((tm, R_WIDTH), lambda i: (i, 0))] * 8,
        out_shape=[nat] * 8,
        compiler_params=_cparams(("parallel",)),
        name="rwkv_prep",
    )(u, u, u, u, mu_rkv, mu_lora, row(w0), row(a0), row(k_k), row(k_a), row(r_k), ww, wa, wg)


SCAN_TB = 32
SCAN_V = R_HEAD // 2
SCAN_ACC = 4


def _rwkv_scan_kernel(r_ref, dec_ref, kp_ref, kn_ref, beta_ref, v_ref, y_ref, s_ref):
    @pl.when(pl.program_id(0) == 0)
    def _():
        s_ref[...] = jnp.zeros(s_ref.shape, F32)

    tb = r_ref.shape[0]

    def total(acc):
        return (acc[0] + acc[1]) + (acc[2] + acc[3])

    def add_term(acc, k, term):
        acc[k % SCAN_ACC] = term if acc[k % SCAN_ACC] is None else acc[k % SCAN_ACC] + term

    acc = [None] * SCAN_ACC
    for k in range(R_HEAD):
        add_term(acc, k, s_ref[k] * kn_ref[0, k:k + 1, :])

    def step(t, s_kk):
        v_t = v_ref[t]
        t_next = jnp.minimum(t + 1, tb - 1)
        acc_y = [None] * SCAN_ACC
        acc_s = [None] * SCAN_ACC
        for k in range(R_HEAD):
            s_new = (s_ref[k] * dec_ref[t, k:k + 1, :]
                     + (v_t * kp_ref[t, k:k + 1, :] - s_kk * beta_ref[t, k:k + 1, :]))
            s_ref[k] = s_new
            add_term(acc_y, k, s_new * r_ref[t, k:k + 1, :])
            add_term(acc_s, k, s_new * kn_ref[t_next, k:k + 1, :])
        y_ref[t] = total(acc_y)
        return total(acc_s)

    lax.fori_loop(0, tb, step, total(acc))


def _rwkv_scan(r, dec, kp, kn, beta, v):
    t = r.shape[0]
    kspec = pl.BlockSpec((SCAN_TB, R_HEAD, LANES), lambda i: (i, 0, 0))
    vspec = pl.BlockSpec((SCAN_TB, SCAN_V, LANES), lambda i: (i, 0, 0))
    return pl.pallas_call(
        _rwkv_scan_kernel,
        grid=(t // SCAN_TB,),
        in_specs=[kspec] * 5 + [vspec],
        out_specs=vspec,
        out_shape=jax.ShapeDtypeStruct((t, SCAN_V, LANES), F32),
        scratch_shapes=[pltpu.VMEM((R_HEAD, SCAN_V, LANES), F32)],
        compiler_params=_cparams(("arbitrary",)),
        name="rwkv_scan",
    )(r, dec, kp, kn, beta, v)


def _to_scan_k(a, bsz, seq):
    a = a.reshape(bsz, seq, R_HEADS, R_HEAD).transpose(1, 3, 0, 2)[:, :, None]
    return jnp.broadcast_to(a, (seq, R_HEAD, 2, bsz, R_HEADS)).reshape(seq, R_HEAD, 2 * bsz * R_HEADS)


def _to_scan_v(a, bsz, seq):
    a = a.reshape(bsz, seq, R_HEADS, 2, SCAN_V).transpose(1, 4, 3, 0, 2)
    return a.reshape(seq, SCAN_V, 2 * bsz * R_HEADS)


def _from_scan_v(y, bsz, seq):
    y = y.reshape(seq, SCAN_V, 2, bsz, R_HEADS).transpose(3, 0, 4, 2, 1)
    return y.reshape(bsz * seq, R_WIDTH)


M_AUG = 2 * M_V


def _mlstm_kernel(qk_ref, v_ref, o_ref, g_ref, cw_ref, cb_ref, gb_ref, mh_ref, y_ref,
                  ext_ref, c_ref, m_ref):
    c_idx = pl.program_id(1)
    L = CHUNK

    @pl.when(c_idx == 0)
    def _():
        ext_ref[0:SUBLANES, :] = jnp.zeros((SUBLANES, ext_ref.shape[1]), F32)
        c_ref[...] = jnp.zeros(c_ref.shape, F32)
        m_ref[...] = jnp.full(m_ref.shape, -jnp.inf, F32)

    ext_ref[SUBLANES:SUBLANES + L, :] = qk_ref[...]
    conv = cb_ref[...]
    for j in range(CONV_K):
        off = SUBLANES - (CONV_K - 1) + j
        conv = conv + cw_ref[j:j + 1, :] * ext_ref[off:off + L, :]
    ext_ref[0:SUBLANES, :] = ext_ref[L:L + SUBLANES, :]
    qk = conv * _sigmoid(conv)
    q_all = qk[:, :M_HEADS * M_QK].astype(BF16)
    k_all = qk[:, M_HEADS * M_QK:] * (M_QK ** -0.5)

    gpre = g_ref[...] + gb_ref[...]
    lane = lax.broadcasted_iota(I32, (L, LANES), 1)
    gcols = jnp.where(lane < M_HEADS, gpre, -_softplus(-gpre))
    grows = gcols.T
    ti = lax.broadcasted_iota(I32, (L, L), 0)
    si = lax.broadcasted_iota(I32, (L, L), 1)
    causal = si <= ti
    ltri = jnp.where(causal, 1.0, 0.0).astype(BF16)
    utri = jnp.where(ti <= si, 1.0, 0.0).astype(BF16)
    acols = _dot_exact_rhs_left(ltri, gcols)
    arows = _dot_exact_rhs(grows, utri)
    ones_col = jnp.where(lane == 0, 1.0, 0.0).astype(BF16)

    for h in range(M_HEADS):
        a_col = acols[:, M_HEADS + h:M_HEADS + h + 1]
        a_row = arows[M_HEADS + h:M_HEADS + h + 1, :]
        i_col = gcols[:, h:h + 1]
        i_row = grows[h:h + 1, :]
        m_st = m_ref[h:h + 1, 0:1]
        d = jnp.where(causal, a_col - a_row + i_row, -jnp.inf)
        inter = a_col + m_st
        m_t = jnp.maximum(inter, jnp.max(d, axis=-1, keepdims=True))
        q = q_all[:, h * M_QK:(h + 1) * M_QK]
        k = k_all[:, h * M_QK:(h + 1) * M_QK]
        v_aug = jnp.concatenate([v_ref[:, h * M_V:(h + 1) * M_V].astype(BF16), ones_col], axis=1)
        s = _dot_nt(q, k.astype(BF16)) * jnp.exp(d - m_t)
        ie = jnp.exp(inter - m_t)
        tot = ie * _dot(q, c_ref[h].astype(BF16)) + _dot(s.astype(BF16), v_aug)
        num = tot[:, :M_V]
        den = tot[:, M_V:M_V + 1]
        h_c = num / jnp.maximum(jnp.abs(den), jnp.exp(-m_t))

        a_tot = a_col[L - 1:L, :]
        gl = a_tot - a_col + i_col
        m_new = jnp.maximum(a_tot + m_st, jnp.max(gl, axis=0, keepdims=True))
        sc = jnp.exp(a_tot + m_st - m_new)
        ge = jnp.exp(gl - m_new)
        c_ref[h] = sc * c_ref[h] + _dot_tn((k * ge).astype(BF16), v_aug)
        m_ref[h:h + 1, :] = jnp.broadcast_to(m_new, (1, LANES))

        mu = jnp.mean(h_c, axis=-1, keepdims=True)
        hc = h_c - mu
        var = jnp.mean(hc * hc, axis=-1, keepdims=True)
        hn = hc * lax.rsqrt(var + M_NORM_EPS) * mh_ref[:, h * M_V:(h + 1) * M_V]
        y_ref[:, h * M_V:(h + 1) * M_V] = _sigmoid(o_ref[:, h * M_V:(h + 1) * M_V]) * hn


def _dot_exact_rhs_left(ones_bf16, x):
    hi, mid, lo = _split3(x)
    return _dot(ones_bf16, hi) + _dot(ones_bf16, mid) + _dot(ones_bf16, lo)


def _mlstm(u, bsz, seq, conv_w, conv_b, i_bias, f_bias, mh_w):
    nc = seq // CHUNK
    w = M_WIDTH
    gbias = jnp.zeros((1, LANES), F32).at[0, :M_HEADS].set(i_bias).at[0, M_HEADS:2 * M_HEADS].set(f_bias)
    blk = lambda col: pl.BlockSpec((CHUNK, w), lambda b, c: (b * nc + c, col // w))
    const = lambda shape: pl.BlockSpec(shape, lambda b, c: (0, 0))
    return pl.pallas_call(
        _mlstm_kernel,
        grid=(bsz, nc),
        in_specs=[blk(C_MQK), blk(C_MV), blk(C_MO),
                  pl.BlockSpec((CHUNK, LANES), lambda b, c: (b * nc + c, C_MG // LANES)),
                  const((CONV_K, w)), const((1, w)), const((1, LANES)), const((1, w))],
        out_specs=pl.BlockSpec((CHUNK, w), lambda b, c: (b * nc + c, 0)),
        out_shape=jax.ShapeDtypeStruct((bsz * seq, w), F32),
        scratch_shapes=[pltpu.VMEM((CHUNK + SUBLANES, w), F32),
                        pltpu.VMEM((M_HEADS, M_QK, M_AUG), F32),
                        pltpu.VMEM((M_HEADS, LANES), F32)],
        compiler_params=_cparams(("parallel", "arbitrary")),
        name="mlstm",
    )(u, u, u, u, conv_w, conv_b[None, :], gbias, mh_w[None, :])


MERGE_TM = 256
C_ROUTE_G = N_EXPERTS


def _merge_kernel(alpha, ys_ref, bonus_ref, g_ref, ym_ref, gr_ref, gm_ref, x_ref,
                  lnxw_ref, lnxb_ref, bgr_ref, bgm_ref, wbr_ref, wbm_ref, wout_ref, l1w_ref, l1b_ref,
                  wrh_ref, wrl_ref, br_ref, x1_o, ids_o, wts_o, cnt_o):
    ys = ys_ref[...]
    mu = _seg_sum(ys, R_HEAD) * (1.0 / R_HEAD)
    yc = ys - mu
    var = _seg_sum(yc * yc, R_HEAD) * (1.0 / R_HEAD)
    y = yc * lax.rsqrt(var + R_GN_EPS) * lnxw_ref[...] + lnxb_ref[...]
    y_r = (y + bonus_ref[...]) * g_ref[...]
    br = _dot(y_r.astype(BF16), wbr_ref[...])
    bm = _dot(ym_ref[...].astype(BF16), wbm_ref[...])
    mix_in = _sigmoid(gr_ref[...] + bgr_ref[...]) * br + _sigmoid(gm_ref[...] + bgm_ref[...]) * bm
    mix = _dot(mix_in.astype(BF16), wout_ref[...])
    x1 = _layer_norm(alpha * x_ref[...] + mix, l1w_ref[...], l1b_ref[...])
    x1_o[...] = x1

    xh = x1.astype(BF16)
    xl = (x1 - xh.astype(F32)).astype(BF16)
    logits = (_dot(xh, wrh_ref[...]) + (_dot(xh, wrl_ref[...]) + _dot(xl, wrh_ref[...]))) + br_ref[...]
    tm = logits.shape[0]
    lane_i = lax.broadcasted_iota(I32, (tm, LANES), 1)
    lane = lane_i.astype(F32)
    group_of_lane = (lane_i // EXPERTS_PER_GROUP).astype(F32)
    big = float(LANES)
    neg = -jnp.inf
    lg = jnp.where((lane_i >= C_ROUTE_G) & (lane_i < C_ROUTE_G + N_GROUPS), logits, neg)
    gmax = jnp.max(lg, axis=-1, keepdims=True)
    gsel = jnp.min(jnp.where(lg == gmax, lane - C_ROUTE_G, big), axis=-1, keepdims=True)
    g_w = 1.0 / jnp.sum(jnp.exp(lg - gmax), axis=-1, keepdims=True)
    le = jnp.where((lane_i < N_EXPERTS) & (group_of_lane == gsel), logits, neg)
    m1 = jnp.max(le, axis=-1, keepdims=True)
    i1 = jnp.min(jnp.where(le == m1, lane, big), axis=-1, keepdims=True)
    le2 = jnp.where(lane == i1, neg, le)
    m2 = jnp.max(le2, axis=-1, keepdims=True)
    i2 = jnp.min(jnp.where(le2 == m2, lane, big), axis=-1, keepdims=True)
    e2 = jnp.exp(m2 - m1)
    w1 = g_w / (1.0 + e2)
    w2 = g_w * e2 / (1.0 + e2)
    wts_o[...] = jnp.where(lane_i == 0, w1, jnp.where(lane_i == 1, w2, 0.0))

    oh1 = jnp.where(lane == i1, 1.0, 0.0)
    oh2 = jnp.where(lane == i2, 1.0, 0.0)
    ri = lax.broadcasted_iota(I32, (tm, tm), 0)
    ci = lax.broadcasted_iota(I32, (tm, tm), 1)
    lstrict = jnp.where(ci < ri, 1.0, 0.0).astype(BF16)
    tot1 = jnp.sum(oh1, axis=0, keepdims=True)
    tot2 = jnp.sum(oh2, axis=0, keepdims=True)
    rank1 = jnp.sum(_dot(lstrict, oh1.astype(BF16)) * oh1, axis=-1, keepdims=True)
    rank2 = jnp.sum((_dot(lstrict, oh2.astype(BF16)) + tot1) * oh2, axis=-1, keepdims=True)
    ids = jnp.where(lane_i == 0, i1, jnp.where(lane_i == 1, i2, jnp.where(lane_i == 2, rank1,
                                                                          jnp.where(lane_i == 3, rank2, 0.0))))
    ids_o[...] = ids.astype(I32)
    cnt_o[...] = jnp.broadcast_to(tot1 + tot2, cnt_o.shape).astype(I32)


def _merge(alpha, ys, bonus, g, ym, u, x, lnx_w, lnx_b, b_gate, w_br, w_bm, w_out, ln1_w, ln1_b,
           w_rg, b_rg, w_re, b_re):
    n, d = x.shape
    tm = MERGE_TM
    wr = jnp.zeros((d, LANES), F32).at[:, :N_EXPERTS].set(w_re).at[:, C_ROUTE_G:C_ROUTE_G + N_GROUPS].set(w_rg)
    wr_hi = wr.astype(BF16)
    wr_lo = (wr - wr_hi.astype(F32)).astype(BF16)
    b_r = jnp.zeros((1, LANES), F32).at[0, :N_EXPERTS].set(b_re).at[0, C_ROUTE_G:C_ROUTE_G + N_GROUPS].set(b_rg)
    tile = lambda w_: pl.BlockSpec((tm, w_), lambda i: (i, 0))
    const = lambda shape: pl.BlockSpec(shape, lambda i: (0, 0))
    return pl.pallas_call(
        functools.partial(_merge_kernel, alpha),
        grid=(n // tm,),
        in_specs=[tile(R_WIDTH), tile(R_WIDTH), tile(R_WIDTH), tile(M_WIDTH),
                  pl.BlockSpec((tm, d), lambda i: (i, C_GR // d)),
                  pl.BlockSpec((tm, d), lambda i: (i, C_GM // d)),
                  tile(d),
                  const((1, R_WIDTH)), const((1, R_WIDTH)), const((1, d)), const((1, d)),
                  const((R_WIDTH, d)), const((M_WIDTH, d)), const((d, d)), const((1, d)), const((1, d)),
                  const((d, LANES)), const((d, LANES)), const((1, LANES))],
        out_specs=[tile(d), tile(LANES), tile(LANES), pl.BlockSpec((SUBLANES, LANES), lambda i: (i, 0))],
        out_shape=[jax.ShapeDtypeStruct((n, d), F32), jax.ShapeDtypeStruct((n, LANES), I32),
                   jax.ShapeDtypeStruct((n, LANES), F32),
                   jax.ShapeDtypeStruct((n // tm * SUBLANES, LANES), I32)],
        compiler_params=_cparams(("parallel",)),
        name="merge_ln1_router",
    )(ys, bonus, g, ym, u, u, x, lnx_w[None, :], lnx_b[None, :], b_gate[None, :d], b_gate[None, d:],
      w_br.astype(BF16), w_bm.astype(BF16), w_out.astype(BF16), ln1_w[None, :], ln1_b[None, :],
      wr_hi, wr_lo, b_r)


MOE_CH = 256
DISPATCH_TM = 256


ROW_DMA_UNROLL = 8


def _each(count, fn, unroll=1):
    def body(r, carry):
        fn(r)
        return carry
    lax.fori_loop(0, count, body, 0, unroll=unroll)


def _each_choice(fn):
    def both(r):
        for j in range(TOP_K):
            fn(r, j)
    return both


def _zero_unused_tail(pstart, nchunks, zeros_ref, out_hbm, sem):
    ch = zeros_ref.shape[0]
    used = (pstart[N_EXPERTS - 1] + nchunks[N_EXPERTS - 1] * ch) // ch
    total = out_hbm.shape[0] // ch

    def chunk(c):
        return pltpu.make_async_copy(zeros_ref, out_hbm.at[pl.ds(pl.multiple_of(c * ch, ch), ch)], sem)

    def over_tail(fn):
        def body(c, carry):
            fn(c)
            return carry
        lax.fori_loop(used, total, body, 0)

    over_tail(lambda c: chunk(c).start())
    over_tail(lambda c: chunk(c).wait())


def _moe_dispatch_kernel(dest, pstart, nchunks, x_ref, xs_hbm, zbuf, sem_z, sem):
    i = pl.program_id(0)
    tm = x_ref.shape[0]
    ch = zbuf.shape[0]

    @pl.when(i == 0)
    def _():
        zbuf[...] = jnp.zeros(zbuf.shape, F32)

        def tail(e):
            row0 = pl.multiple_of(pstart[e] + (nchunks[e] - 1) * ch, ch)
            return pltpu.make_async_copy(zbuf, xs_hbm.at[pl.ds(row0, ch)], sem_z)

        def start(e):
            @pl.when(nchunks[e] > 0)
            def _():
                tail(e).start()

        def wait(e):
            @pl.when(nchunks[e] > 0)
            def _():
                tail(e).wait()

        _each(N_EXPERTS, start)
        _each(N_EXPERTS, wait)
        _zero_unused_tail(pstart, nchunks, zbuf, xs_hbm, sem_z)

    def row(r, j):
        return pltpu.make_async_copy(x_ref.at[pl.ds(r, 1)],
                                     xs_hbm.at[pl.ds(dest[(i * tm + r) * TOP_K + j], 1)], sem)

    _each(tm, _each_choice(lambda r, j: row(r, j).start()), ROW_DMA_UNROLL)
    _each(tm, _each_choice(lambda r, j: row(r, j).wait()), ROW_DMA_UNROLL)


def _moe_dispatch(x1, dest, pstart, nchunks, rows_pad):
    n, d = x1.shape
    tm = DISPATCH_TM
    return pl.pallas_call(
        _moe_dispatch_kernel,
        grid_spec=pltpu.PrefetchScalarGridSpec(
            num_scalar_prefetch=3,
            grid=(n // tm,),
            in_specs=[pl.BlockSpec((tm, d), lambda i, *_: (i, 0))],
            out_specs=pl.BlockSpec(memory_space=pl.ANY),
            scratch_shapes=[pltpu.VMEM((MOE_CH, d), F32), pltpu.SemaphoreType.DMA, pltpu.SemaphoreType.DMA]),
        out_shape=jax.ShapeDtypeStruct((rows_pad, d), F32),
        compiler_params=_cparams(("arbitrary",)),
        name="moe_dispatch",
    )(dest, pstart, nchunks, x1)


def _moe_expert_kernel(pstart, nchunks, xs_hbm, wg_ref, wu_ref, wd_ref, ys_hbm,
                       xbuf, ybuf, wgb, wub, wdb, sem_in, sem_out):
    e = pl.program_id(0)
    ch = xbuf.shape[1]
    nc = nchunks[e]
    base = pstart[e]

    def rows(c):
        return pl.ds(pl.multiple_of(base + c * ch, ch), ch)

    def load(c, slot):
        return pltpu.make_async_copy(xs_hbm.at[rows(c)], xbuf.at[slot], sem_in.at[slot])

    def store(c, slot):
        return pltpu.make_async_copy(ybuf.at[slot], ys_hbm.at[rows(c)], sem_out.at[slot])

    @pl.when(nc > 0)
    def _():
        load(0, 0).start()
        wgb[...] = wg_ref[0].astype(BF16)
        wub[...] = wu_ref[0].astype(BF16)
        wdb[...] = wd_ref[0].astype(BF16)

        def chunk(c, carry):
            slot = c % 2

            @pl.when(c + 1 < nc)
            def _():
                load(c + 1, 1 - slot).start()

            load(c, slot).wait()

            @pl.when(c >= 2)
            def _():
                store(c - 2, slot).wait()

            xb = xbuf[slot].astype(BF16)
            gate = _dot(xb, wgb[...])
            hb = gate * _sigmoid(gate) * _dot(xb, wub[...])
            ybuf[slot] = _dot(hb.astype(BF16), wdb[...])
            store(c, slot).start()
            return carry

        lax.fori_loop(0, nc, chunk, 0)

        @pl.when(nc >= 2)
        def _():
            store(nc - 2, nc % 2).wait()

        store(nc - 1, (nc - 1) % 2).wait()

    @pl.when(e == pl.num_programs(0) - 1)
    def _():
        ybuf[0] = jnp.zeros(ybuf.shape[1:], F32)
        _zero_unused_tail(pstart, nchunks, ybuf.at[0], ys_hbm, sem_out.at[0])


def _moe_experts(xs, pstart, nchunks, w_gate, w_up, w_down):
    rows_pad, d = xs.shape
    de = w_gate.shape[-1]
    wspec = lambda shape: pl.BlockSpec(shape, lambda e, *_: (e, 0, 0))
    return pl.pallas_call(
        _moe_expert_kernel,
        grid_spec=pltpu.PrefetchScalarGridSpec(
            num_scalar_prefetch=2,
            grid=(N_EXPERTS,),
            in_specs=[pl.BlockSpec(memory_space=pl.ANY),
                      wspec((1, d, de)), wspec((1, d, de)), wspec((1, de, d))],
            out_specs=pl.BlockSpec(memory_space=pl.ANY),
            scratch_shapes=[pltpu.VMEM((2, MOE_CH, d), F32), pltpu.VMEM((2, MOE_CH, d), F32),
                            pltpu.VMEM((d, de), BF16), pltpu.VMEM((d, de), BF16), pltpu.VMEM((de, d), BF16),
                            pltpu.SemaphoreType.DMA((2,)), pltpu.SemaphoreType.DMA((2,))]),
        out_shape=jax.ShapeDtypeStruct((rows_pad, d), F32),
        compiler_params=_cparams(("arbitrary",)),
        name="moe_experts",
    )(pstart, nchunks, xs, w_gate, w_up, w_down)


def _moe_plan(ids, cnt, tm):
    n = ids.shape[0]
    tile_cnt = cnt[::SUBLANES, :N_EXPERTS]
    counts = jnp.sum(tile_cnt, axis=0)
    padded = ((counts + MOE_CH - 1) // MOE_CH) * MOE_CH
    pstart = jnp.cumsum(padded) - padded
    tile_base = pstart[None, :] + jnp.cumsum(tile_cnt, axis=0) - tile_cnt
    eid = ids[:, 0:TOP_K]
    rank = ids[:, TOP_K:2 * TOP_K]
    base_of_tok = jnp.repeat(tile_base, tm, axis=0)[:, None, :]
    chosen = eid[:, :, None] == jnp.arange(N_EXPERTS, dtype=I32)[None, None, :]
    dest = jnp.sum(jnp.where(chosen, base_of_tok, 0), axis=-1) + rank
    return dest.reshape(-1).astype(I32), pstart.astype(I32), (padded // MOE_CH).astype(I32)


FINAL_TM = 256


def _final_kernel(alpha, dest, x1_ref, wts_ref, p_ref, wpg_ref, wple_ref, l2w_ref, l2b_ref, ys_hbm, o_ref,
                  ybuf, sem):
    i = pl.program_id(0)
    tm = x1_ref.shape[0]

    def row(r, j):
        return pltpu.make_async_copy(ys_hbm.at[pl.ds(dest[(i * tm + r) * TOP_K + j], 1)],
                                     ybuf.at[j, pl.ds(r, 1)], sem)

    _each(tm, _each_choice(lambda r, j: row(r, j).start()), ROW_DMA_UNROLL)
    x1 = x1_ref[...]
    ple = _sigmoid(_dot(x1.astype(BF16), wpg_ref[...])) * _dot(p_ref[...], wple_ref[...])
    _each(tm, _each_choice(lambda r, j: row(r, j).wait()), ROW_DMA_UNROLL)
    moe = ybuf[0] * wts_ref[:, 0:1] + ybuf[1] * wts_ref[:, 1:2]
    o_ref[...] = _layer_norm(alpha * x1 + moe + ple, l2w_ref[...], l2b_ref[...])


def _final(alpha, dest, x1, ys, wts, p_bf, w_pg, w_ple, ln2_w, ln2_b):
    n, d = x1.shape
    tm = FINAL_TM
    tile = lambda w_: pl.BlockSpec((tm, w_), lambda i, *_: (i, 0))
    const = lambda shape: pl.BlockSpec(shape, lambda i, *_: (0, 0))
    return pl.pallas_call(
        functools.partial(_final_kernel, alpha),
        grid_spec=pltpu.PrefetchScalarGridSpec(
            num_scalar_prefetch=1,
            grid=(n // tm,),
            in_specs=[tile(d), tile(LANES), tile(p_bf.shape[1]),
                      const((d, d)), const((p_bf.shape[1], d)), const((1, d)), const((1, d)),
                      pl.BlockSpec(memory_space=pl.ANY)],
            out_specs=tile(d),
            scratch_shapes=[pltpu.VMEM((TOP_K, tm, d), F32), pltpu.SemaphoreType.DMA]),
        out_shape=jax.ShapeDtypeStruct((n, d), F32),
        compiler_params=_cparams(("arbitrary",)),
        name="final_ln2",
    )(dest, x1, wts, p_bf, w_pg.astype(BF16), w_ple.astype(BF16), ln2_w[None, :], ln2_b[None, :], ys)


def _regroup_w_in(w):
    m0 = RWKV_COLS
    g0 = RWKV_COLS + MLSTM_COLS
    mqk = 2 * M_HEADS * M_QK
    pad = lambda c: jnp.zeros((w.shape[0], c), w.dtype)
    parts = [w[:, 0:3 * R_WIDTH],
             w[:, m0:m0 + mqk],
             w[:, m0 + mqk:m0 + mqk + M_WIDTH],
             w[:, m0 + mqk + M_WIDTH + 2 * M_HEADS:m0 + MLSTM_COLS],
             w[:, g0:],
             w[:, 3 * R_WIDTH:RWKV_COLS], pad(LORA_PAD - LORA_COLS),
             w[:, m0 + mqk + M_WIDTH:m0 + mqk + M_WIDTH + 2 * M_HEADS], pad(LANES - 2 * M_HEADS)]
    out = jnp.concatenate(parts, axis=1)
    assert out.shape[1] == C_TOTAL
    return out


def kernel(x, p, w_in, mu_shift, w0, w_w2, a0, w_a2, w_g2, k_k, k_a, r_k, lnx_w, lnx_b, conv_w, conv_b,
           i_bias, f_bias, mh_w, b_gate, w_br, w_bm, w_out, ln1_w, ln1_b, w_rg, b_rg, w_re, b_re,
           w_gate, w_up, w_down, w_pg, w_ple, ln2_w, ln2_b):
    bsz, seq, d = x.shape
    depth = w_in.shape[0]
    assert bsz * R_HEADS * 2 == LANES and seq % PROJ_TM == 0 and seq % CHUNK == 0
    alpha = (2 * depth) ** 0.25
    n = bsz * seq
    xf = x.reshape(n, d)
    for i in range(depth):
        u = _proj_in(xf.astype(BF16), _regroup_w_in(w_in[i]).astype(BF16))
        r, dec, kp, v, kn, beta, g, bonus = _rwkv_prep(
            u, seq, mu_shift[i], w0[i], w_w2[i], a0[i], w_a2[i], w_g2[i], k_k[i], k_a[i], r_k[i])
        ks = lambda a: _to_scan_k(a, bsz, seq)
        y_scan = _rwkv_scan(ks(r), ks(dec), ks(kp), ks(kn), ks(beta), _to_scan_v(v, bsz, seq))
        ys = _from_scan_v(y_scan, bsz, seq)
        ym = _mlstm(u, bsz, seq, conv_w[i], conv_b[i], i_bias[i], f_bias[i], mh_w[i])
        x1, ids, wts, cnt = _merge(alpha, ys, bonus, g, ym, u, xf, lnx_w[i], lnx_b[i], b_gate[i], w_br[i],
                                   w_bm[i], w_out[i], ln1_w[i], ln1_b[i], w_rg[i], b_rg[i], w_re[i], b_re[i])
        dest, pstart, nchunks = _moe_plan(ids, cnt, MERGE_TM)
        rows_pad = TOP_K * n + N_EXPERTS * MOE_CH
        xs = _moe_dispatch(x1, dest, pstart, nchunks, rows_pad)
        ys_sorted = _moe_experts(xs, pstart, nchunks, w_gate[i], w_up[i], w_down[i])
        xf = _final(alpha, dest, x1, ys_sorted, wts, p[i].reshape(n, -1).astype(BF16), w_pg[i], w_ple[i],
                    ln2_w[i], ln2_b[i])
    return xf.reshape(bsz, seq, d)
```

```python
import functools

import jax
import jax.numpy as jnp
from jax import lax
from jax.experimental import pallas as pl
from jax.experimental.pallas import tpu as pltpu

F32 = jnp.float32
BF16 = jnp.bfloat16
I32 = jnp.int32

R_HEADS, R_HEAD = 16, 64
R_WIDTH = R_HEADS * R_HEAD
DECAY_LORA, AAA_LORA, GATE_LORA = 64, 64, 160
LORA_COLS = DECAY_LORA + AAA_LORA + GATE_LORA
LORA_PAD = 512
R_GN_EPS = 64e-5
RWKV_COLS = 3 * R_WIDTH + LORA_COLS
M_HEADS, M_QK, M_V = 8, 64, 128
M_WIDTH = M_HEADS * M_V
CONV_K = 4
CHUNK = 128
M_NORM_EPS = 1e-6
MLSTM_COLS = 2 * M_HEADS * M_QK + 2 * M_WIDTH + 2 * M_HEADS
N_GROUPS, EXPERTS_PER_GROUP = 4, 8
N_EXPERTS = N_GROUPS * EXPERTS_PER_GROUP
TOP_K = 2
MOE_BLOCK = 128
LN_EPS = 1e-5

LANES = 128
SUBLANES = 8
V7X_VMEM_BYTES = 64 * 1024 * 1024
VMEM_LIMIT = 56 * 1024 * 1024

C_RKV = 0
C_MQK = 3072
C_MV = 4096
C_MO = 5120
C_GR = 6144
C_GM = 8192
C_LORA = 10240
C_MG = 10752
C_TOTAL = 10880
PROJ_TN = 2176
PROJ_TM = 512


def _cparams(sem, vmem=VMEM_LIMIT):
    return pltpu.CompilerParams(dimension_semantics=sem, vmem_limit_bytes=vmem)


def _sigmoid(x):
    return 1.0 / (1.0 + jnp.exp(-x))


def _softplus(x):
    return jnp.maximum(x, 0.0) + jnp.log1p(jnp.exp(-jnp.abs(x)))


def _split3(x):
    hi = x.astype(BF16)
    r1 = x - hi.astype(F32)
    mid = r1.astype(BF16)
    lo = (r1 - mid.astype(F32)).astype(BF16)
    return hi, mid, lo


def _dot(a, b):
    return jnp.dot(a, b, preferred_element_type=F32)


def _dot_nt(a, b):
    return lax.dot_general(a, b, (((1,), (1,)), ((), ())), preferred_element_type=F32)


def _dot_tn(a, b):
    return lax.dot_general(a, b, (((0,), (0,)), ((), ())), preferred_element_type=F32)


def _dot_exact_rhs(x, ones_bf16):
    hi, mid, lo = _split3(x)
    return _dot(hi, ones_bf16) + _dot(mid, ones_bf16) + _dot(lo, ones_bf16)


def _block_ones(n, group):
    r = lax.broadcasted_iota(I32, (n, n), 0) // group
    c = lax.broadcasted_iota(I32, (n, n), 1) // group
    return jnp.where(r == c, 1.0, 0.0).astype(BF16)


def _seg_sum(x, group):
    ones = _block_ones(LANES, group)
    slabs = [_dot_exact_rhs(x[:, p * LANES:(p + 1) * LANES], ones) for p in range(x.shape[1] // LANES)]
    return jnp.concatenate(slabs, axis=1)


def _layer_norm(x, w, b):
    mu = jnp.mean(x, axis=-1, keepdims=True)
    xc = x - mu
    var = jnp.mean(xc * xc, axis=-1, keepdims=True)
    return xc * lax.rsqrt(var + LN_EPS) * w + b


def _proj_kernel(x_ref, w_ref, o_ref):
    o_ref[...] = _dot(x_ref[...], w_ref[...])


def _proj_in(x_bf, w_bf):
    m, k = x_bf.shape
    n = w_bf.shape[1]
    return pl.pallas_call(
        _proj_kernel,
        grid=(n // PROJ_TN, m // PROJ_TM),
        in_specs=[pl.BlockSpec((PROJ_TM, k), lambda j, i: (i, 0)),
                  pl.BlockSpec((k, PROJ_TN), lambda j, i: (0, j))],
        out_specs=pl.BlockSpec((PROJ_TM, PROJ_TN), lambda j, i: (i, j)),
        out_shape=jax.ShapeDtypeStruct((m, n), F32),
        compiler_params=_cparams(("parallel", "parallel")),
        name="proj_in",
    )(x_bf, w_bf)


PREP_TM = 256


def _rwkv_prep_kernel(seq, u_ref, l_ref, up_ref, lp_ref, mu_ref, mul_ref, w0_ref, a0_ref, kk_ref, ka_ref,
                      rk_ref, ww_ref, wa_ref, wg_ref,
                      r_o, dec_o, kp_o, v_o, kn_o, beta_o, g_o, bonus_o):
    i = pl.program_id(0)
    tm = u_ref.shape[0]
    first = (i * tm) % seq == 0
    row = lax.broadcasted_iota(I32, (tm, 1), 0)

    def shift(u, prev8):
        prev_row = jnp.where(first, 0.0, prev8[SUBLANES - 1:SUBLANES, :])
        return jnp.where(row == 0, prev_row, pltpu.roll(u, 1, 0))

    u = u_ref[...]
    z = u + mu_ref[...] * (shift(u, up_ref[...]) - u)
    lo = l_ref[...]
    zl = lo + mul_ref[...] * (shift(lo, lp_ref[...]) - lo)

    r = z[:, 0:R_WIDTH]
    k = z[:, R_WIDTH:2 * R_WIDTH]
    v = z[:, 2 * R_WIDTH:3 * R_WIDTH]
    w_pre = w0_ref[...] + _dot(jnp.tanh(zl).astype(BF16), ww_ref[...])
    w = -_softplus(-w_pre) - 0.5
    dec = jnp.exp(-jnp.exp(w))
    a = _sigmoid(a0_ref[...] + _dot(zl.astype(BF16), wa_ref[...]))
    g = _dot(_sigmoid(zl).astype(BF16), wg_ref[...])

    kk = k * kk_ref[...]
    nrm = jnp.sqrt(_seg_sum(kk * kk, R_HEAD))
    kn = kk / jnp.maximum(nrm, 1e-12)
    kp = k * (1.0 + (a - 1.0) * ka_ref[...])
    bonus = _seg_sum(r * kp * rk_ref[...], R_HEAD) * v

    r_o[...] = r
    dec_o[...] = dec
    kp_o[...] = kp
    v_o[...] = v
    kn_o[...] = kn
    beta_o[...] = kn * a
    g_o[...] = g
    bonus_o[...] = bonus


def _rwkv_prep(u, seq, mu, w0, w_w2, a0, w_a2, w_g2, k_k, k_a, r_k):
    n = u.shape[0]
    tm = PREP_TM
    mu_rkv = mu[None, :3 * R_WIDTH]
    mu_lora = jnp.zeros((1, LORA_PAD), F32).at[0, :LORA_COLS].set(mu[3 * R_WIDTH:])
    ww = jnp.zeros((LORA_PAD, R_WIDTH), BF16).at[0:DECAY_LORA].set(w_w2.astype(BF16))
    wa = jnp.zeros((LORA_PAD, R_WIDTH), BF16).at[DECAY_LORA:DECAY_LORA + AAA_LORA].set(w_a2.astype(BF16))
    wg = jnp.zeros((LORA_PAD, R_WIDTH), BF16).at[DECAY_LORA + AAA_LORA:LORA_COLS].set(w_g2.astype(BF16))
    row = lambda t: t.reshape(1, R_WIDTH)
    prev_blk = lambda i: jnp.maximum(i * (tm // SUBLANES) - 1, 0)
    const = lambda shape: pl.BlockSpec(shape, lambda i: (0, 0))
    nat = jax.ShapeDtypeStruct((n, R_WIDTH), F32)
    return pl.pallas_call(
        functools.partial(_rwkv_prep_kernel, seq),
        grid=(n // tm,),
        in_specs=[pl.BlockSpec((tm, 3 * R_WIDTH), lambda i: (i, C_RKV // (3 * R_WIDTH))),
                  pl.BlockSpec((tm, LORA_PAD), lambda i: (i, C_LORA // LORA_PAD)),
                  pl.BlockSpec((SUBLANES, 3 * R_WIDTH), lambda i: (prev_blk(i), C_RKV // (3 * R_WIDTH))),
                  pl.BlockSpec((SUBLANES, LORA_PAD), lambda i: (prev_blk(i), C_LORA // LORA_PAD)),
                  const((1, 3 * R_WIDTH)), const((1, LORA_PAD)),
                  const((1, R_WIDTH)), const((1, R_WIDTH)), const((1, R_WIDTH)), const((1, R_WIDTH)),
                  const((1, R_WIDTH)),
                  const((LORA_PAD, R_WIDTH)), const((LORA_PAD, R_WIDTH)), const((LORA_PAD, R_WIDTH))],
        out_specs=[pl.BlockSpec((tm, R_WIDTH), lambda i: (i, 0))] * 8,
        out_shape=[nat] * 8,
        compiler_params=_cparams(("parallel",)),
        name="rwkv_prep",
    )(u, u, u, u, mu_rkv, mu_lora, row(w0), row(a0), row(k_k), row(k_a), row(r_k), ww, wa, wg)


SCAN_TB = LANES
SCAN_V = R_HEAD // 2
SCAN_ACC = 4
SCAN_KOPS = 5
Z_PITCH = R_HEAD + SUBLANES
V_PITCH = SCAN_V + SUBLANES
RELAYOUT_UNROLL = 8


def _rwkv_scan_kernel(kn_hbm, dec_hbm, kp_hbm, beta_hbm, r_hbm, v_hbm, y_ref,
                      s_ref, stage, z_ref, xs_ref, vs_ref, ys_ref, sem):
    i = pl.program_id(0)
    nb = stage.shape[1]
    tb = SCAN_TB
    slab = R_HEADS * Z_PITCH

    @pl.when(i == 0)
    def _():
        s_ref[...] = jnp.zeros(s_ref.shape, F32)

    srcs = (kn_hbm, dec_hbm, kp_hbm, beta_hbm, r_hbm, v_hbm)

    def fetch(j):
        return pltpu.make_async_copy(srcs[j].at[:, pl.ds(i * tb, tb), :], stage.at[j % 2], sem.at[j % 2])

    def to_time_on_lanes(slot):
        for b in range(nb):
            for hp in range(R_HEADS // 2):
                tt = stage[slot, b, :, hp * LANES:(hp + 1) * LANES].T
                for hh in range(2):
                    z_ref[pl.ds((b * R_HEADS + 2 * hp + hh) * Z_PITCH, R_HEAD), :] = tt[hh * R_HEAD:(hh + 1) * R_HEAD]

    def head_rows(b, c):
        return z_ref[pl.ds(b * slab + c, R_HEADS, stride=Z_PITCH), :]

    def key_operand(op):
        def body(k, carry):
            rows = [head_rows(b, k) for b in range(nb)]
            xs_ref[op, k] = jnp.concatenate(rows + rows, axis=0).T
            return carry
        lax.fori_loop(0, R_HEAD, body, 0, unroll=RELAYOUT_UNROLL)

    def value_operand():
        def body(vp, carry):
            rows = [head_rows(b, vh * SCAN_V + vp) for vh in range(2) for b in range(nb)]
            vs_ref[pl.ds(vp, tb, stride=V_PITCH), :] = jnp.concatenate(rows, axis=0).T
            return carry
        lax.fori_loop(0, SCAN_V, body, 0, unroll=RELAYOUT_UNROLL)

    fetch(0).start()
    fetch(1).start()
    for j in range(SCAN_KOPS + 1):
        fetch(j).wait()
        to_time_on_lanes(j % 2)
        if j + 2 <= SCAN_KOPS:
            fetch(j + 2).start()
        if j < SCAN_KOPS:
            key_operand(j)
        else:
            value_operand()

    kn_ref, dec_ref, kp_ref, beta_ref, r_ref = (xs_ref.at[op] for op in range(SCAN_KOPS))

    def total(acc):
        return (acc[0] + acc[1]) + (acc[2] + acc[3])

    def add_term(acc, k, term):
        acc[k % SCAN_ACC] = term if acc[k % SCAN_ACC] is None else acc[k % SCAN_ACC] + term

    acc = [None] * SCAN_ACC
    for k in range(R_HEAD):
        add_term(acc, k, s_ref[k] * kn_ref[k, 0:1, :])

    def step(t, s_kk):
        row0 = pl.multiple_of(t * V_PITCH, SUBLANES)
        v_t = vs_ref[pl.ds(row0, SCAN_V), :]
        t_next = jnp.minimum(t + 1, tb - 1)
        acc_y = [None] * SCAN_ACC
        acc_s = [None] * SCAN_ACC
        for k in range(R_HEAD):
            s_new = (s_ref[k] * dec_ref[k, pl.ds(t, 1), :]
                     + (v_t * kp_ref[k, pl.ds(t, 1), :] - s_kk * beta_ref[k, pl.ds(t, 1), :]))
            s_ref[k] = s_new
            add_term(acc_y, k, s_new * r_ref[k, pl.ds(t, 1), :])
            add_term(acc_s, k, s_new * kn_ref[k, pl.ds(t_next, 1), :])
        ys_ref[pl.ds(row0, SCAN_V), :] = total(acc_y)
        return total(acc_s)

    lax.fori_loop(0, tb, step, total(acc))

    def out_rows(vp, carry):
        tt = ys_ref[pl.ds(vp, tb, stride=V_PITCH), :].T
        for vh in range(2):
            for b in range(nb):
                g = vh * nb + b
                z_ref[pl.ds(b * slab + vh * SCAN_V + vp, R_HEADS, stride=Z_PITCH), :] = \
                    tt[g * R_HEADS:(g + 1) * R_HEADS]
        return carry
    lax.fori_loop(0, SCAN_V, out_rows, 0, unroll=RELAYOUT_UNROLL)
    for b in range(nb):
        for hp in range(R_HEADS // 2):
            pair = [z_ref[pl.ds((b * R_HEADS + 2 * hp + hh) * Z_PITCH, R_HEAD), :] for hh in range(2)]
            y_ref[b, :, hp * LANES:(hp + 1) * LANES] = jnp.concatenate(pair, axis=0).T


def _rwkv_scan(kn, dec, kp, beta, r, v, bsz, seq):
    shape3 = (bsz, seq, R_WIDTH)
    ops = [a.reshape(shape3) for a in (kn, dec, kp, beta, r, v)]
    y = pl.pallas_call(
        _rwkv_scan_kernel,
        grid=(seq // SCAN_TB,),
        in_specs=[pl.BlockSpec(memory_space=pl.ANY)] * len(ops),
        out_specs=pl.BlockSpec((bsz, SCAN_TB, R_WIDTH), lambda i: (0, i, 0)),
        out_shape=jax.ShapeDtypeStruct(shape3, F32),
        scratch_shapes=[pltpu.VMEM((R_HEAD, SCAN_V, LANES), F32),
                        pltpu.VMEM((2, bsz, SCAN_TB, R_WIDTH), F32),
                        pltpu.VMEM((bsz * R_HEADS * Z_PITCH, SCAN_TB), F32),
                        pltpu.VMEM((SCAN_KOPS, R_HEAD, SCAN_TB, LANES), F32),
                        pltpu.VMEM((SCAN_TB * V_PITCH, LANES), F32),
                        pltpu.VMEM((SCAN_TB * V_PITCH, LANES), F32),
                        pltpu.SemaphoreType.DMA((2,))],
        compiler_params=_cparams(("arbitrary",)),
        name="rwkv_scan",
    )(*ops)
    return y.reshape(bsz * seq, R_WIDTH)


M_AUG = 2 * M_V


def _mlstm_kernel(qk_ref, v_ref, o_ref, g_ref, cw_ref, cb_ref, gb_ref, mh_ref, y_ref,
                  ext_ref, c_ref, m_ref):
    c_idx = pl.program_id(1)
    L = CHUNK

    @pl.when(c_idx == 0)
    def _():
        ext_ref[0:SUBLANES, :] = jnp.zeros((SUBLANES, ext_ref.shape[1]), F32)
        c_ref[...] = jnp.zeros(c_ref.shape, F32)
        m_ref[...] = jnp.full(m_ref.shape, -jnp.inf, F32)

    ext_ref[SUBLANES:SUBLANES + L, :] = qk_ref[...]
    conv = cb_ref[...]
    for j in range(CONV_K):
        off = SUBLANES - (CONV_K - 1) + j
        conv = conv + cw_ref[j:j + 1, :] * ext_ref[off:off + L, :]
    ext_ref[0:SUBLANES, :] = ext_ref[L:L + SUBLANES, :]
    qk = conv * _sigmoid(conv)
    q_all = qk[:, :M_HEADS * M_QK].astype(BF16)
    k_all = qk[:, M_HEADS * M_QK:] * (M_QK ** -0.5)

    gpre = g_ref[...] + gb_ref[...]
    lane = lax.broadcasted_iota(I32, (L, LANES), 1)
    gcols = jnp.where(lane < M_HEADS, gpre, -_softplus(-gpre))
    grows = gcols.T
    ti = lax.broadcasted_iota(I32, (L, L), 0)
    si = lax.broadcasted_iota(I32, (L, L), 1)
    causal = si <= ti
    ltri = jnp.where(causal, 1.0, 0.0).astype(BF16)
    utri = jnp.where(ti <= si, 1.0, 0.0).astype(BF16)
    acols = _dot_exact_rhs_left(ltri, gcols)
    arows = _dot_exact_rhs(grows, utri)
    ones_col = jnp.where(lane == 0, 1.0, 0.0).astype(BF16)

    for h in range(M_HEADS):
        a_col = acols[:, M_HEADS + h:M_HEADS + h + 1]
        a_row = arows[M_HEADS + h:M_HEADS + h + 1, :]
        i_col = gcols[:, h:h + 1]
        i_row = grows[h:h + 1, :]
        m_st = m_ref[h:h + 1, 0:1]
        d = jnp.where(causal, a_col - a_row + i_row, -jnp.inf)
        inter = a_col + m_st
        m_t = jnp.maximum(inter, jnp.max(d, axis=-1, keepdims=True))
        q = q_all[:, h * M_QK:(h + 1) * M_QK]
        k = k_all[:, h * M_QK:(h + 1) * M_QK]
        v_aug = jnp.concatenate([v_ref[:, h * M_V:(h + 1) * M_V].astype(BF16), ones_col], axis=1)
        s = _dot_nt(q, k.astype(BF16)) * jnp.exp(d - m_t)
        ie = jnp.exp(inter - m_t)
        tot = ie * _dot(q, c_ref[h].astype(BF16)) + _dot(s.astype(BF16), v_aug)
        num = tot[:, :M_V]
        den = tot[:, M_V:M_V + 1]
        h_c = num / jnp.maximum(jnp.abs(den), jnp.exp(-m_t))

        a_tot = a_col[L - 1:L, :]
        gl = a_tot - a_col + i_col
        m_new = jnp.maximum(a_tot + m_st, jnp.max(gl, axis=0, keepdims=True))
        sc = jnp.exp(a_tot + m_st - m_new)
        ge = jnp.exp(gl - m_new)
        c_ref[h] = sc * c_ref[h] + _dot_tn((k * ge).astype(BF16), v_aug)
        m_ref[h:h + 1, :] = jnp.broadcast_to(m_new, (1, LANES))

        mu = jnp.mean(h_c, axis=-1, keepdims=True)
        hc = h_c - mu
        var = jnp.mean(hc * hc, axis=-1, keepdims=True)
        hn = hc * lax.rsqrt(var + M_NORM_EPS) * mh_ref[:, h * M_V:(h + 1) * M_V]
        y_ref[:, h * M_V:(h + 1) * M_V] = _sigmoid(o_ref[:, h * M_V:(h + 1) * M_V]) * hn


def _dot_exact_rhs_left(ones_bf16, x):
    hi, mid, lo = _split3(x)
    return _dot(ones_bf16, hi) + _dot(ones_bf16, mid) + _dot(ones_bf16, lo)


def _mlstm(u, bsz, seq, conv_w, conv_b, i_bias, f_bias, mh_w):
    nc = seq // CHUNK
    w = M_WIDTH
    gbias = jnp.zeros((1, LANES), F32).at[0, :M_HEADS].set(i_bias).at[0, M_HEADS:2 * M_HEADS].set(f_bias)
    blk = lambda col: pl.BlockSpec((CHUNK, w), lambda b, c: (b * nc + c, col // w))
    const = lambda shape: pl.BlockSpec(shape, lambda b, c: (0, 0))
    return pl.pallas_call(
        _mlstm_kernel,
        grid=(bsz, nc),
        in_specs=[blk(C_MQK), blk(C_MV), blk(C_MO),
                  pl.BlockSpec((CHUNK, LANES), lambda b, c: (b * nc + c, C_MG // LANES)),
                  const((CONV_K, w)), const((1, w)), const((1, LANES)), const((1, w))],
        out_specs=pl.BlockSpec((CHUNK, w), lambda b, c: (b * nc + c, 0)),
        out_shape=jax.ShapeDtypeStruct((bsz * seq, w), F32),
        scratch_shapes=[pltpu.VMEM((CHUNK + SUBLANES, w), F32),
                        pltpu.VMEM((M_HEADS, M_QK, M_AUG), F32),
                        pltpu.VMEM((M_HEADS, LANES), F32)],
        compiler_params=_cparams(("parallel", "arbitrary")),
        name="mlstm",
    )(u, u, u, u, conv_w, conv_b[None, :], gbias, mh_w[None, :])


MERGE_TM = 256
C_ROUTE_G = N_EXPERTS


def _merge_kernel(alpha, ys_ref, bonus_ref, g_ref, ym_ref, gr_ref, gm_ref, x_ref,
                  lnxw_ref, lnxb_ref, bgr_ref, bgm_ref, wbr_ref, wbm_ref, wout_ref, l1w_ref, l1b_ref,
                  wrh_ref, wrl_ref, br_ref, x1_o, ids_o, wts_o, cnt_o):
    ys = ys_ref[...]
    mu = _seg_sum(ys, R_HEAD) * (1.0 / R_HEAD)
    yc = ys - mu
    var = _seg_sum(yc * yc, R_HEAD) * (1.0 / R_HEAD)
    y = yc * lax.rsqrt(var + R_GN_EPS) * lnxw_ref[...] + lnxb_ref[...]
    y_r = (y + bonus_ref[...]) * g_ref[...]
    br = _dot(y_r.astype(BF16), wbr_ref[...])
    bm = _dot(ym_ref[...].astype(BF16), wbm_ref[...])
    mix_in = _sigmoid(gr_ref[...] + bgr_ref[...]) * br + _sigmoid(gm_ref[...] + bgm_ref[...]) * bm
    mix = _dot(mix_in.astype(BF16), wout_ref[...])
    x1 = _layer_norm(alpha * x_ref[...] + mix, l1w_ref[...], l1b_ref[...])
    x1_o[...] = x1

    xh = x1.astype(BF16)
    xl = (x1 - xh.astype(F32)).astype(BF16)
    logits = (_dot(xh, wrh_ref[...]) + (_dot(xh, wrl_ref[...]) + _dot(xl, wrh_ref[...]))) + br_ref[...]
    tm = logits.shape[0]
    lane_i = lax.broadcasted_iota(I32, (tm, LANES), 1)
    lane = lane_i.astype(F32)
    group_of_lane = (lane_i // EXPERTS_PER_GROUP).astype(F32)
    big = float(LANES)
    neg = -jnp.inf
    lg = jnp.where((lane_i >= C_ROUTE_G) & (lane_i < C_ROUTE_G + N_GROUPS), logits, neg)
    gmax = jnp.max(lg, axis=-1, keepdims=True)
    gsel = jnp.min(jnp.where(lg == gmax, lane - C_ROUTE_G, big), axis=-1, keepdims=True)
    g_w = 1.0 / jnp.sum(jnp.exp(lg - gmax), axis=-1, keepdims=True)
    le = jnp.where((lane_i < N_EXPERTS) & (group_of_lane == gsel), logits, neg)
    m1 = jnp.max(le, axis=-1, keepdims=True)
    i1 = jnp.min(jnp.where(le == m1, lane, big), axis=-1, keepdims=True)
    le2 = jnp.where(lane == i1, neg, le)
    m2 = jnp.max(le2, axis=-1, keepdims=True)
    i2 = jnp.min(jnp.where(le2 == m2, lane, big), axis=-1, keepdims=True)
    e2 = jnp.exp(m2 - m1)
    w1 = g_w / (1.0 + e2)
    w2 = g_w * e2 / (1.0 + e2)
    wts_o[...] = jnp.where(lane_i == 0, w1, jnp.where(lane_i == 1, w2, 0.0))

    oh1 = jnp.where(lane == i1, 1.0, 0.0)
    oh2 = jnp.where(lane == i2, 1.0, 0.0)
    ri = lax.broadcasted_iota(I32, (tm, tm), 0)
    ci = lax.broadcasted_iota(I32, (tm, tm), 1)
    lstrict = jnp.where(ci < ri, 1.0, 0.0).astype(BF16)
    tot1 = jnp.sum(oh1, axis=0, keepdims=True)
    tot2 = jnp.sum(oh2, axis=0, keepdims=True)
    rank1 = jnp.sum(_dot(lstrict, oh1.astype(BF16)) * oh1, axis=-1, keepdims=True)
    rank2 = jnp.sum((_dot(lstrict, oh2.astype(BF16)) + tot1) * oh2, axis=-1, keepdims=True)
    ids = jnp.where(lane_i == 0, i1, jnp.where(lane_i == 1, i2, jnp.where(lane_i == 2, rank1,
                                                                          jnp.where(lane_i == 3, rank2, 0.0))))
    ids_o[...] = ids.astype(I32)
    cnt_o[...] = jnp.broadcast_to(tot1 + tot2, cnt_o.shape).astype(I32)


def _merge(alpha, ys, bonus, g, ym, u, x, lnx_w, lnx_b, b_gate, w_br, w_bm, w_out, ln1_w, ln1_b,
           w_rg, b_rg, w_re, b_re):
    n, d = x.shape
    tm = MERGE_TM
    wr = jnp.zeros((d, LANES), F32).at[:, :N_EXPERTS].set(w_re).at[:, C_ROUTE_G:C_ROUTE_G + N_GROUPS].set(w_rg)
    wr_hi = wr.astype(BF16)
    wr_lo = (wr - wr_hi.astype(F32)).astype(BF16)
    b_r = jnp.zeros((1, LANES), F32).at[0, :N_EXPERTS].set(b_re).at[0, C_ROUTE_G:C_ROUTE_G + N_GROUPS].set(b_rg)
    tile = lambda w_: pl.BlockSpec((tm, w_), lambda i: (i, 0))
    const = lambda shape: pl.BlockSpec(shape, lambda i: (0, 0))
    return pl.pallas_call(
        functools.partial(_merge_kernel, alpha),
        grid=(n // tm,),
        in_specs=[tile(R_WIDTH), tile(R_WIDTH), tile(R_WIDTH), tile(M_WIDTH),
                  pl.BlockSpec((tm, d), lambda i: (i, C_GR // d)),
                  pl.BlockSpec((tm, d), lambda i: (i, C_GM // d)),
                  tile(d),
                  const((1, R_WIDTH)), const((1, R_WIDTH)), const((1, d)), const((1, d)),
                  const((R_WIDTH, d)), const((M_WIDTH, d)), const((d, d)), const((1, d)), const((1, d)),
                  const((d, LANES)), const((d, LANES)), const((1, LANES))],
        out_specs=[tile(d), tile(LANES), tile(LANES), pl.BlockSpec((SUBLANES, LANES), lambda i: (i, 0))],
        out_shape=[jax.ShapeDtypeStruct((n, d), F32), jax.ShapeDtypeStruct((n, LANES), I32),
                   jax.ShapeDtypeStruct((n, LANES), F32),
                   jax.ShapeDtypeStruct((n // tm * SUBLANES, LANES), I32)],
        compiler_params=_cparams(("parallel",)),
        name="merge_ln1_router",
    )(ys, bonus, g, ym, u, u, x, lnx_w[None, :], lnx_b[None, :], b_gate[None, :d], b_gate[None, d:],
      w_br.astype(BF16), w_bm.astype(BF16), w_out.astype(BF16), ln1_w[None, :], ln1_b[None, :],
      wr_hi, wr_lo, b_r)


MOE_CH = 256
DISPATCH_TM = 256


ROW_DMA_UNROLL = 8


def _each(count, fn, unroll=1):
    def body(r, carry):
        fn(r)
        return carry
    lax.fori_loop(0, count, body, 0, unroll=unroll)


def _each_choice(fn):
    def both(r):
        for j in range(TOP_K):
            fn(r, j)
    return both


def _zero_unused_tail(pstart, nchunks, zeros_ref, out_hbm, sem):
    ch = zeros_ref.shape[0]
    used = (pstart[N_EXPERTS - 1] + nchunks[N_EXPERTS - 1] * ch) // ch
    total = out_hbm.shape[0] // ch

    def chunk(c):
        return pltpu.make_async_copy(zeros_ref, out_hbm.at[pl.ds(pl.multiple_of(c * ch, ch), ch)], sem)

    def over_tail(fn):
        def body(c, carry):
            fn(c)
            return carry
        lax.fori_loop(used, total, body, 0)

    over_tail(lambda c: chunk(c).start())
    over_tail(lambda c: chunk(c).wait())


def _moe_dispatch_kernel(dest, pstart, nchunks, x_ref, xs_hbm, zbuf, sem_z, sem):
    i = pl.program_id(0)
    tm = x_ref.shape[0]
    ch = zbuf.shape[0]

    @pl.when(i == 0)
    def _():
        zbuf[...] = jnp.zeros(zbuf.shape, F32)

        def tail(e):
            row0 = pl.multiple_of(pstart[e] + (nchunks[e] - 1) * ch, ch)
            return pltpu.make_async_copy(zbuf, xs_hbm.at[pl.ds(row0, ch)], sem_z)

        def start(e):
            @pl.when(nchunks[e] > 0)
            def _():
                tail(e).start()

        def wait(e):
            @pl.when(nchunks[e] > 0)
            def _():
                tail(e).wait()

        _each(N_EXPERTS, start)
        _each(N_EXPERTS, wait)
        _zero_unused_tail(pstart, nchunks, zbuf, xs_hbm, sem_z)

    def row(r, j):
        return pltpu.make_async_copy(x_ref.at[pl.ds(r, 1)],
                                     xs_hbm.at[pl.ds(dest[(i * tm + r) * TOP_K + j], 1)], sem)

    _each(tm, _each_choice(lambda r, j: row(r, j).start()), ROW_DMA_UNROLL)
    _each(tm, _each_choice(lambda r, j: row(r, j).wait()), ROW_DMA_UNROLL)


def _moe_dispatch(x1, dest, pstart, nchunks, rows_pad):
    n, d = x1.shape
    tm = DISPATCH_TM
    return pl.pallas_call(
        _moe_dispatch_kernel,
        grid_spec=pltpu.PrefetchScalarGridSpec(
            num_scalar_prefetch=3,
            grid=(n // tm,),
            in_specs=[pl.BlockSpec((tm, d), lambda i, *_: (i, 0))],
            out_specs=pl.BlockSpec(memory_space=pl.ANY),
            scratch_shapes=[pltpu.VMEM((MOE_CH, d), F32), pltpu.SemaphoreType.DMA, pltpu.SemaphoreType.DMA]),
        out_shape=jax.ShapeDtypeStruct((rows_pad, d), F32),
        compiler_params=_cparams(("arbitrary",)),
        name="moe_dispatch",
    )(dest, pstart, nchunks, x1)


def _moe_expert_kernel(pstart, nchunks, xs_hbm, wg_ref, wu_ref, wd_ref, ys_hbm,
                       xbuf, ybuf, wgb, wub, wdb, sem_in, sem_out):
    e = pl.program_id(0)
    ch = xbuf.shape[1]
    nc = nchunks[e]
    base = pstart[e]

    def rows(c):
        return pl.ds(pl.multiple_of(base + c * ch, ch), ch)

    def load(c, slot):
        return pltpu.make_async_copy(xs_hbm.at[rows(c)], xbuf.at[slot], sem_in.at[slot])

    def store(c, slot):
        return pltpu.make_async_copy(ybuf.at[slot], ys_hbm.at[rows(c)], sem_out.at[slot])

    @pl.when(nc > 0)
    def _():
        load(0, 0).start()
        wgb[...] = wg_ref[0].astype(BF16)
        wub[...] = wu_ref[0].astype(BF16)
        wdb[...] = wd_ref[0].astype(BF16)

        def chunk(c, carry):
            slot = c % 2

            @pl.when(c + 1 < nc)
            def _():
                load(c + 1, 1 - slot).start()

            load(c, slot).wait()

            @pl.when(c >= 2)
            def _():
                store(c - 2, slot).wait()

            xb = xbuf[slot].astype(BF16)
            gate = _dot(xb, wgb[...])
            hb = gate * _sigmoid(gate) * _dot(xb, wub[...])
            ybuf[slot] = _dot(hb.astype(BF16), wdb[...])
            store(c, slot).start()
            return carry

        lax.fori_loop(0, nc, chunk, 0)

        @pl.when(nc >= 2)
        def _():
            store(nc - 2, nc % 2).wait()

        store(nc - 1, (nc - 1) % 2).wait()

    @pl.when(e == pl.num_programs(0) - 1)
    def _():
        ybuf[0] = jnp.zeros(ybuf.shape[1:], F32)
        _zero_unused_tail(pstart, nchunks, ybuf.at[0], ys_hbm, sem_out.at[0])


def _moe_experts(xs, pstart, nchunks, w_gate, w_up, w_down):
    rows_pad, d = xs.shape
    de = w_gate.shape[-1]
    wspec = lambda shape: pl.BlockSpec(shape, lambda e, *_: (e, 0, 0))
    return pl.pallas_call(
        _moe_expert_kernel,
        grid_spec=pltpu.PrefetchScalarGridSpec(
            num_scalar_prefetch=2,
            grid=(N_EXPERTS,),
            in_specs=[pl.BlockSpec(memory_space=pl.ANY),
                      wspec((1, d, de)), wspec((1, d, de)), wspec((1, de, d))],
            out_specs=pl.BlockSpec(memory_space=pl.ANY),
            scratch_shapes=[pltpu.VMEM((2, MOE_CH, d), F32), pltpu.VMEM((2, MOE_CH, d), F32),
                            pltpu.VMEM((d, de), BF16), pltpu.VMEM((d, de), BF16), pltpu.VMEM((de, d), BF16),
                            pltpu.SemaphoreType.DMA((2,)), pltpu.SemaphoreType.DMA((2,))]),
        out_shape=jax.ShapeDtypeStruct((rows_pad, d), F32),
        compiler_params=_cparams(("arbitrary",)),
        name="moe_experts",
    )(pstart, nchunks, xs, w_gate, w_up, w_down)


def _moe_plan(ids, cnt, tm):
    n = ids.shape[0]
    tile_cnt = cnt[::SUBLANES, :N_EXPERTS]
    counts = jnp.sum(tile_cnt, axis=0)
    padded = ((counts + MOE_CH - 1) // MOE_CH) * MOE_CH
    pstart = jnp.cumsum(padded) - padded
    tile_base = pstart[None, :] + jnp.cumsum(tile_cnt, axis=0) - tile_cnt
    eid = ids[:, 0:TOP_K]
    rank = ids[:, TOP_K:2 * TOP_K]
    base_of_tok = jnp.repeat(tile_base, tm, axis=0)[:, None, :]
    chosen = eid[:, :, None] == jnp.arange(N_EXPERTS, dtype=I32)[None, None, :]
    dest = jnp.sum(jnp.where(chosen, base_of_tok, 0), axis=-1) + rank
    return dest.reshape(-1).astype(I32), pstart.astype(I32), (padded // MOE_CH).astype(I32)


FINAL_TM = 256


def _final_kernel(alpha, dest, x1_ref, wts_ref, p_ref, wpg_ref, wple_ref, l2w_ref, l2b_ref, ys_hbm, o_ref,
                  ybuf, sem):
    i = pl.program_id(0)
    tm = x1_ref.shape[0]

    def row(r, j):
        return pltpu.make_async_copy(ys_hbm.at[pl.ds(dest[(i * tm + r) * TOP_K + j], 1)],
                                     ybuf.at[j, pl.ds(r, 1)], sem)

    _each(tm, _each_choice(lambda r, j: row(r, j).start()), ROW_DMA_UNROLL)
    x1 = x1_ref[...]
    ple = _sigmoid(_dot(x1.astype(BF16), wpg_ref[...])) * _dot(p_ref[...], wple_ref[...])
    _each(tm, _each_choice(lambda r, j: row(r, j).wait()), ROW_DMA_UNROLL)
    moe = ybuf[0] * wts_ref[:, 0:1] + ybuf[1] * wts_ref[:, 1:2]
    o_ref[...] = _layer_norm(alpha * x1 + moe + ple, l2w_ref[...], l2b_ref[...])


def _final(alpha, dest, x1, ys, wts, p_bf, w_pg, w_ple, ln2_w, ln2_b):
    n, d = x1.shape
    tm = FINAL_TM
    tile = lambda w_: pl.BlockSpec((tm, w_), lambda i, *_: (i, 0))
    const = lambda shape: pl.BlockSpec(shape, lambda i, *_: (0, 0))
    return pl.pallas_call(
        functools.partial(_final_kernel, alpha),
        grid_spec=pltpu.PrefetchScalarGridSpec(
            num_scalar_prefetch=1,
            grid=(n // tm,),
            in_specs=[tile(d), tile(LANES), tile(p_bf.shape[1]),
                      const((d, d)), const((p_bf.shape[1], d)), const((1, d)), const((1, d)),
                      pl.BlockSpec(memory_space=pl.ANY)],
            out_specs=tile(d),
            scratch_shapes=[pltpu.VMEM((TOP_K, tm, d), F32), pltpu.SemaphoreType.DMA]),
        out_shape=jax.ShapeDtypeStruct((n, d), F32),
        compiler_params=_cparams(("arbitrary",)),
        name="final_ln2",
    )(dest, x1, wts, p_bf, w_pg.astype(BF16), w_ple.astype(BF16), ln2_w[None, :], ln2_b[None, :], ys)


def _regroup_w_in(w):
    m0 = RWKV_COLS
    g0 = RWKV_COLS + MLSTM_COLS
    mqk = 2 * M_HEADS * M_QK
    pad = lambda c: jnp.zeros((w.shape[0], c), w.dtype)
    parts = [w[:, 0:3 * R_WIDTH],
             w[:, m0:m0 + mqk],
             w[:, m0 + mqk:m0 + mqk + M_WIDTH],
             w[:, m0 + mqk + M_WIDTH + 2 * M_HEADS:m0 + MLSTM_COLS],
             w[:, g0:],
             w[:, 3 * R_WIDTH:RWKV_COLS], pad(LORA_PAD - LORA_COLS),
             w[:, m0 + mqk + M_WIDTH:m0 + mqk + M_WIDTH + 2 * M_HEADS], pad(LANES - 2 * M_HEADS)]
    out = jnp.concatenate(parts, axis=1)
    assert out.shape[1] == C_TOTAL
    return out


def kernel(x, p, w_in, mu_shift, w0, w_w2, a0, w_a2, w_g2, k_k, k_a, r_k, lnx_w, lnx_b, conv_w, conv_b,
           i_bias, f_bias, mh_w, b_gate, w_br, w_bm, w_out, ln1_w, ln1_b, w_rg, b_rg, w_re, b_re,
           w_gate, w_up, w_down, w_pg, w_ple, ln2_w, ln2_b):
    bsz, seq, d = x.shape
    depth = w_in.shape[0]
    assert bsz * R_HEADS * 2 == LANES and seq % PROJ_TM == 0 and seq % CHUNK == 0
    alpha = (2 * depth) ** 0.25
    n = bsz * seq
    xf = x.reshape(n, d)
    for i in range(depth):
        u = _proj_in(xf.astype(BF16), _regroup_w_in(w_in[i]).astype(BF16))
        r, dec, kp, v, kn, beta, g, bonus = _rwkv_prep(
            u, seq, mu_shift[i], w0[i], w_w2[i], a0[i], w_a2[i], w_g2[i], k_k[i], k_a[i], r_k[i])
        ys = _rwkv_scan(kn, dec, kp, beta, r, v, bsz, seq)
        ym = _mlstm(u, bsz, seq, conv_w[i], conv_b[i], i_bias[i], f_bias[i], mh_w[i])
        x1, ids, wts, cnt = _merge(alpha, ys, bonus, g, ym, u, xf, lnx_w[i], lnx_b[i], b_gate[i], w_br[i],
                                   w_bm[i], w_out[i], ln1_w[i], ln1_b[i], w_rg[i], b_rg[i], w_re[i], b_re[i])
        dest, pstart, nchunks = _moe_plan(ids, cnt, MERGE_TM)
        rows_pad = TOP_K * n + N_EXPERTS * MOE_CH
        xs = _moe_dispatch(x1, dest, pstart, nchunks, rows_pad)
        ys_sorted = _moe_experts(xs, pstart, nchunks, w_gate[i], w_up[i], w_down[i])
        xf = _final(alpha, dest, x1, ys_sorted, wts, p[i].reshape(n, -1).astype(BF16), w_pg[i], w_ple[i],
                    ln2_w[i], ln2_b[i])
    return xf.reshape(bsz, seq, d)
```

```python
import functools

import jax
import jax.numpy as jnp
from jax import lax
from jax.experimental import pallas as pl
from jax.experimental.pallas import tpu as pltpu

F32 = jnp.float32
BF16 = jnp.bfloat16
I32 = jnp.int32

R_HEADS, R_HEAD = 16, 64
R_WIDTH = R_HEADS * R_HEAD
DECAY_LORA, AAA_LORA, GATE_LORA = 64, 64, 160
LORA_COLS = DECAY_LORA + AAA_LORA + GATE_LORA
LORA_PAD = 512
R_GN_EPS = 64e-5
RWKV_COLS = 3 * R_WIDTH + LORA_COLS
M_HEADS, M_QK, M_V = 8, 64, 128
M_WIDTH = M_HEADS * M_V
CONV_K = 4
CHUNK = 128
M_NORM_EPS = 1e-6
MLSTM_COLS = 2 * M_HEADS * M_QK + 2 * M_WIDTH + 2 * M_HEADS
N_GROUPS, EXPERTS_PER_GROUP = 4, 8
N_EXPERTS = N_GROUPS * EXPERTS_PER_GROUP
TOP_K = 2
MOE_BLOCK = 128
LN_EPS = 1e-5

LANES = 128
SUBLANES = 8
MXU_DIM = 256
V7X_VMEM_BYTES = 64 * 1024 * 1024
VMEM_LIMIT = 56 * 1024 * 1024

C_RKV = 0
C_MQK = 3072
C_MV = 4096
C_MO = 5120
C_GR = 6144
C_GM = 8192
C_LORA = 10240
C_MG = 10752
C_TOTAL = 10880
PROJ_TN = 2176
PROJ_TM = 512


def _cparams(sem, vmem=VMEM_LIMIT):
    return pltpu.CompilerParams(dimension_semantics=sem, vmem_limit_bytes=vmem)


def _sigmoid(x):
    return 1.0 / (1.0 + jnp.exp(-x))


def _softplus(x):
    return jnp.maximum(x, 0.0) + jnp.log1p(jnp.exp(-jnp.abs(x)))


def _split3(x):
    hi = x.astype(BF16)
    r1 = x - hi.astype(F32)
    mid = r1.astype(BF16)
    lo = (r1 - mid.astype(F32)).astype(BF16)
    return hi, mid, lo


def _dot(a, b):
    return jnp.dot(a, b, preferred_element_type=F32)


def _dot_nt(a, b):
    return lax.dot_general(a, b, (((1,), (1,)), ((), ())), preferred_element_type=F32)


def _dot_tn(a, b):
    return lax.dot_general(a, b, (((0,), (0,)), ((), ())), preferred_element_type=F32)


def _dot_exact_rhs(x, ones_bf16, terms=3):
    parts = _split3(x)[:terms]
    acc = _dot(parts[0], ones_bf16)
    for part in parts[1:]:
        acc = acc + _dot(part, ones_bf16)
    return acc


def _block_ones(n, group):
    r = lax.broadcasted_iota(I32, (n, n), 0) // group
    c = lax.broadcasted_iota(I32, (n, n), 1) // group
    return jnp.where(r == c, 1.0, 0.0).astype(BF16)


def _seg_sum(x, group):
    ones = _block_ones(MXU_DIM, group)
    slabs = [_dot_exact_rhs(x[:, p * MXU_DIM:(p + 1) * MXU_DIM], ones, terms=2)
             for p in range(x.shape[1] // MXU_DIM)]
    return jnp.concatenate(slabs, axis=1)


def _layer_norm(x, w, b):
    mu = jnp.mean(x, axis=-1, keepdims=True)
    xc = x - mu
    var = jnp.mean(xc * xc, axis=-1, keepdims=True)
    return xc * lax.rsqrt(var + LN_EPS) * w + b


def _proj_kernel(x_ref, w_ref, o_ref):
    o_ref[...] = _dot(x_ref[...], w_ref[...])


def _proj_in(x_bf, w_bf):
    m, k = x_bf.shape
    n = w_bf.shape[1]
    return pl.pallas_call(
        _proj_kernel,
        grid=(n // PROJ_TN, m // PROJ_TM),
        in_specs=[pl.BlockSpec((PROJ_TM, k), lambda j, i: (i, 0)),
                  pl.BlockSpec((k, PROJ_TN), lambda j, i: (0, j))],
        out_specs=pl.BlockSpec((PROJ_TM, PROJ_TN), lambda j, i: (i, j)),
        out_shape=jax.ShapeDtypeStruct((m, n), F32),
        compiler_params=_cparams(("parallel", "parallel")),
        name="proj_in",
    )(x_bf, w_bf)


PREP_TM = 256


def _rwkv_prep_kernel(seq, u_ref, l_ref, up_ref, lp_ref, mu_ref, mul_ref, w0_ref, a0_ref, kk_ref, ka_ref,
                      rk_ref, ww_ref, wa_ref, wg_ref,
                      r_o, dec_o, kp_o, v_o, kn_o, beta_o, g_o, bonus_o):
    i = pl.program_id(0)
    tm = u_ref.shape[0]
    first = (i * tm) % seq == 0
    row = lax.broadcasted_iota(I32, (tm, 1), 0)

    def shift(u, prev8):
        prev_row = jnp.where(first, 0.0, prev8[SUBLANES - 1:SUBLANES, :])
        return jnp.where(row == 0, prev_row, pltpu.roll(u, 1, 0))

    u = u_ref[...]
    z = u + mu_ref[...] * (shift(u, up_ref[...]) - u)
    lo = l_ref[...]
    zl = lo + mul_ref[...] * (shift(lo, lp_ref[...]) - lo)

    r = z[:, 0:R_WIDTH]
    k = z[:, R_WIDTH:2 * R_WIDTH]
    v = z[:, 2 * R_WIDTH:3 * R_WIDTH]
    w_pre = w0_ref[...] + _dot(jnp.tanh(zl).astype(BF16), ww_ref[...])
    w = -_softplus(-w_pre) - 0.5
    dec = jnp.exp(-jnp.exp(w))
    a = _sigmoid(a0_ref[...] + _dot(zl.astype(BF16), wa_ref[...]))
    g = _dot(_sigmoid(zl).astype(BF16), wg_ref[...])

    kk = k * kk_ref[...]
    nrm = jnp.sqrt(_seg_sum(kk * kk, R_HEAD))
    kn = kk / jnp.maximum(nrm, 1e-12)
    kp = k * (1.0 + (a - 1.0) * ka_ref[...])
    bonus = _seg_sum(r * kp * rk_ref[...], R_HEAD) * v

    r_o[...] = r
    dec_o[...] = dec
    kp_o[...] = kp
    v_o[...] = v
    kn_o[...] = kn
    beta_o[...] = kn * a
    g_o[...] = g
    bonus_o[...] = bonus


def _rwkv_prep(u, seq, mu, w0, w_w2, a0, w_a2, w_g2, k_k, k_a, r_k):
    n = u.shape[0]
    tm = PREP_TM
    mu_rkv = mu[None, :3 * R_WIDTH]
    mu_lora = jnp.zeros((1, LORA_PAD), F32).at[0, :LORA_COLS].set(mu[3 * R_WIDTH:])
    ww = jnp.zeros((LORA_PAD, R_WIDTH), BF16).at[0:DECAY_LORA].set(w_w2.astype(BF16))
    wa = jnp.zeros((LORA_PAD, R_WIDTH), BF16).at[DECAY_LORA:DECAY_LORA + AAA_LORA].set(w_a2.astype(BF16))
    wg = jnp.zeros((LORA_PAD, R_WIDTH), BF16).at[DECAY_LORA + AAA_LORA:LORA_COLS].set(w_g2.astype(BF16))
    row = lambda t: t.reshape(1, R_WIDTH)
    prev_blk = lambda i: jnp.maximum(i * (tm // SUBLANES) - 1, 0)
    const = lambda shape: pl.BlockSpec(shape, lambda i: (0, 0))
    nat = jax.ShapeDtypeStruct((n, R_WIDTH), F32)
    return pl.pallas_call(
        functools.partial(_rwkv_prep_kernel, seq),
        grid=(n // tm,),
        in_specs=[pl.BlockSpec((tm, 3 * R_WIDTH), lambda i: (i, C_RKV // (3 * R_WIDTH))),
                  pl.BlockSpec((tm, LORA_PAD), lambda i: (i, C_LORA // LORA_PAD)),
                  pl.BlockSpec((SUBLANES, 3 * R_WIDTH), lambda i: (prev_blk(i), C_RKV // (3 * R_WIDTH))),
                  pl.BlockSpec((SUBLANES, LORA_PAD), lambda i: (prev_blk(i), C_LORA // LORA_PAD)),
                  const((1, 3 * R_WIDTH)), const((1, LORA_PAD)),
                  const((1, R_WIDTH)), const((1, R_WIDTH)), const((1, R_WIDTH)), const((1, R_WIDTH)),
                  const((1, R_WIDTH)),
                  const((LORA_PAD, R_WIDTH)), const((LORA_PAD, R_WIDTH)), const((LORA_PAD, R_WIDTH))],
        out_specs=[pl.BlockSpec((tm, R_WIDTH), lambda i: (i, 0))] * 8,
        out_shape=[nat] * 8,
        compiler_params=_cparams(("parallel",)),
        name="rwkv_prep",
    )(u, u, u, u, mu_rkv, mu_lora, row(w0), row(a0), row(k_k), row(k_a), row(r_k), ww, wa, wg)


SCAN_TB = LANES
SCAN_V = R_HEAD // 2
SCAN_ACC = 4
SCAN_KOPS = 5
Z_PITCH = R_HEAD + SUBLANES
V_PITCH = SCAN_V + SUBLANES
RELAYOUT_UNROLL = 8


def _rwkv_scan_kernel(kn_hbm, dec_hbm, kp_hbm, beta_hbm, r_hbm, v_hbm, y_ref,
                      s_ref, stage, z_ref, xs_ref, vs_ref, ys_ref, sem):
    i = pl.program_id(0)
    nb = stage.shape[1]
    tb = SCAN_TB
    slab = R_HEADS * Z_PITCH

    @pl.when(i == 0)
    def _():
        s_ref[...] = jnp.zeros(s_ref.shape, F32)

    srcs = (kn_hbm, dec_hbm, kp_hbm, beta_hbm, r_hbm, v_hbm)

    def fetch(j):
        return pltpu.make_async_copy(srcs[j].at[:, pl.ds(i * tb, tb), :], stage.at[j % 2], sem.at[j % 2])

    def to_time_on_lanes(slot):
        for b in range(nb):
            for hp in range(R_HEADS // 2):
                tt = stage[slot, b, :, hp * LANES:(hp + 1) * LANES].T
                for hh in range(2):
                    z_ref[pl.ds((b * R_HEADS + 2 * hp + hh) * Z_PITCH, R_HEAD), :] = tt[hh * R_HEAD:(hh + 1) * R_HEAD]

    def head_rows(b, c):
        return z_ref[pl.ds(b * slab + c, R_HEADS, stride=Z_PITCH), :]

    def key_operand(op):
        def body(k, carry):
            rows = [head_rows(b, k) for b in range(nb)]
            xs_ref[op, k] = jnp.concatenate(rows + rows, axis=0).T
            return carry
        lax.fori_loop(0, R_HEAD, body, 0, unroll=RELAYOUT_UNROLL)

    def value_operand():
        def body(vp, carry):
            rows = [head_rows(b, vh * SCAN_V + vp) for vh in range(2) for b in range(nb)]
            vs_ref[pl.ds(vp, tb, stride=V_PITCH), :] = jnp.concatenate(rows, axis=0).T
            return carry
        lax.fori_loop(0, SCAN_V, body, 0, unroll=RELAYOUT_UNROLL)

    fetch(0).start()
    fetch(1).start()
    for j in range(SCAN_KOPS + 1):
        fetch(j).wait()
        to_time_on_lanes(j % 2)
        if j + 2 <= SCAN_KOPS:
            fetch(j + 2).start()
        if j < SCAN_KOPS:
            key_operand(j)
        else:
            value_operand()

    kn_ref, dec_ref, kp_ref, beta_ref, r_ref = (xs_ref.at[op] for op in range(SCAN_KOPS))

    def total(acc):
        return (acc[0] + acc[1]) + (acc[2] + acc[3])

    def add_term(acc, k, term):
        acc[k % SCAN_ACC] = term if acc[k % SCAN_ACC] is None else acc[k % SCAN_ACC] + term

    acc = [None] * SCAN_ACC
    for k in range(R_HEAD):
        add_term(acc, k, s_ref[k] * kn_ref[k, 0:1, :])

    def step(t, s_kk):
        row0 = pl.multiple_of(t * V_PITCH, SUBLANES)
        v_t = vs_ref[pl.ds(row0, SCAN_V), :]
        t_next = jnp.minimum(t + 1, tb - 1)
        acc_y = [None] * SCAN_ACC
        acc_s = [None] * SCAN_ACC
        for k in range(R_HEAD):
            s_new = (s_ref[k] * dec_ref[k, pl.ds(t, 1), :]
                     + (v_t * kp_ref[k, pl.ds(t, 1), :] - s_kk * beta_ref[k, pl.ds(t, 1), :]))
            s_ref[k] = s_new
            add_term(acc_y, k, s_new * r_ref[k, pl.ds(t, 1), :])
            add_term(acc_s, k, s_new * kn_ref[k, pl.ds(t_next, 1), :])
        ys_ref[pl.ds(row0, SCAN_V), :] = total(acc_y)
        return total(acc_s)

    lax.fori_loop(0, tb, step, total(acc))

    def out_rows(vp, carry):
        tt = ys_ref[pl.ds(vp, tb, stride=V_PITCH), :].T
        for vh in range(2):
            for b in range(nb):
                g = vh * nb + b
                z_ref[pl.ds(b * slab + vh * SCAN_V + vp, R_HEADS, stride=Z_PITCH), :] = \
                    tt[g * R_HEADS:(g + 1) * R_HEADS]
        return carry
    lax.fori_loop(0, SCAN_V, out_rows, 0, unroll=RELAYOUT_UNROLL)
    for b in range(nb):
        for hp in range(R_HEADS // 2):
            pair = [z_ref[pl.ds((b * R_HEADS + 2 * hp + hh) * Z_PITCH, R_HEAD), :] for hh in range(2)]
            y_ref[b, :, hp * LANES:(hp + 1) * LANES] = jnp.concatenate(pair, axis=0).T


def _rwkv_scan(kn, dec, kp, beta, r, v, bsz, seq):
    shape3 = (bsz, seq, R_WIDTH)
    ops = [a.reshape(shape3) for a in (kn, dec, kp, beta, r, v)]
    y = pl.pallas_call(
        _rwkv_scan_kernel,
        grid=(seq // SCAN_TB,),
        in_specs=[pl.BlockSpec(memory_space=pl.ANY)] * len(ops),
        out_specs=pl.BlockSpec((bsz, SCAN_TB, R_WIDTH), lambda i: (0, i, 0)),
        out_shape=jax.ShapeDtypeStruct(shape3, F32),
        scratch_shapes=[pltpu.VMEM((R_HEAD, SCAN_V, LANES), F32),
                        pltpu.VMEM((2, bsz, SCAN_TB, R_WIDTH), F32),
                        pltpu.VMEM((bsz * R_HEADS * Z_PITCH, SCAN_TB), F32),
                        pltpu.VMEM((SCAN_KOPS, R_HEAD, SCAN_TB, LANES), F32),
                        pltpu.VMEM((SCAN_TB * V_PITCH, LANES), F32),
                        pltpu.VMEM((SCAN_TB * V_PITCH, LANES), F32),
                        pltpu.SemaphoreType.DMA((2,))],
        compiler_params=_cparams(("arbitrary",)),
        name="rwkv_scan",
    )(*ops)
    return y.reshape(bsz * seq, R_WIDTH)


M_AUG = 2 * M_V


def _mlstm_kernel(qk_ref, v_ref, o_ref, g_ref, cw_ref, cb_ref, gb_ref, mh_ref, y_ref,
                  ext_ref, c_ref, m_ref):
    c_idx = pl.program_id(0)
    L = CHUNK
    nb = qk_ref.shape[0]

    @pl.when(c_idx == 0)
    def _():
        ext_ref[:, 0:SUBLANES, :] = jnp.zeros((nb, SUBLANES, ext_ref.shape[2]), F32)
        c_ref[...] = jnp.zeros(c_ref.shape, F32)
        m_ref[...] = jnp.full(m_ref.shape, -jnp.inf, F32)

    lane = lax.broadcasted_iota(I32, (L, LANES), 1)
    ti = lax.broadcasted_iota(I32, (L, L), 0)
    si = lax.broadcasted_iota(I32, (L, L), 1)
    causal = si <= ti
    ltri = jnp.where(causal, 1.0, 0.0).astype(BF16)
    utri = jnp.where(ti <= si, 1.0, 0.0).astype(BF16)
    ones_col = jnp.where(lane == 0, 1.0, 0.0).astype(BF16)

    for b in range(nb):
        ext_ref[b, SUBLANES:SUBLANES + L, :] = qk_ref[b]
        conv = cb_ref[...]
        for j in range(CONV_K):
            off = SUBLANES - (CONV_K - 1) + j
            conv = conv + cw_ref[j:j + 1, :] * ext_ref[b, off:off + L, :]
        ext_ref[b, 0:SUBLANES, :] = ext_ref[b, L:L + SUBLANES, :]
        qk = conv * _sigmoid(conv)
        q_all = qk[:, :M_HEADS * M_QK].astype(BF16)
        k_all = qk[:, M_HEADS * M_QK:] * (M_QK ** -0.5)

        gpre = g_ref[b] + gb_ref[...]
        gcols = jnp.where(lane < M_HEADS, gpre, -_softplus(-gpre))
        grows = gcols.T
        acols = _dot_exact_rhs_left(ltri, gcols)
        arows = _dot_exact_rhs(grows, utri)

        heads = range(M_HEADS)
        a_col = [acols[:, M_HEADS + h:M_HEADS + h + 1] for h in heads]
        i_col = [gcols[:, h:h + 1] for h in heads]
        m_st = [m_ref[b * M_HEADS + h:b * M_HEADS + h + 1, 0:1] for h in heads]
        d = [jnp.where(causal, a_col[h] - arows[M_HEADS + h:M_HEADS + h + 1, :] + grows[h:h + 1, :], -jnp.inf)
             for h in heads]
        dmax = [jnp.max(d[h], axis=-1, keepdims=True) for h in heads]
        inter = [a_col[h] + m_st[h] for h in heads]
        m_t = [jnp.maximum(inter[h], dmax[h]) for h in heads]
        q = [q_all[:, h * M_QK:(h + 1) * M_QK] for h in heads]
        k = [k_all[:, h * M_QK:(h + 1) * M_QK] for h in heads]
        v_aug = [jnp.concatenate([v_ref[b, :, h * M_V:(h + 1) * M_V].astype(BF16), ones_col], axis=1)
                 for h in heads]
        s = [_dot_nt(q[h], k[h].astype(BF16)) * jnp.exp(d[h] - m_t[h]) for h in heads]
        ie = [jnp.exp(inter[h] - m_t[h]) for h in heads]
        tot = [ie[h] * _dot(q[h], c_ref[b * M_HEADS + h].astype(BF16)) + _dot(s[h].astype(BF16), v_aug[h])
               for h in heads]
        h_c = [tot[h][:, :M_V] / jnp.maximum(jnp.abs(tot[h][:, M_V:M_V + 1]), jnp.exp(-m_t[h])) for h in heads]

        a_tot = [a_col[h][L - 1:L, :] for h in heads]
        gl = [a_tot[h] - a_col[h] + i_col[h] for h in heads]
        m_new = [jnp.maximum(a_tot[h] + m_st[h], jnp.max(gl[h], axis=0, keepdims=True)) for h in heads]
        kg = [(k[h] * jnp.exp(gl[h] - m_new[h])).astype(BF16) for h in heads]
        for h in heads:
            bh = b * M_HEADS + h
            c_ref[bh] = jnp.exp(a_tot[h] + m_st[h] - m_new[h]) * c_ref[bh] + _dot_tn(kg[h], v_aug[h])
            m_ref[bh:bh + 1, :] = jnp.broadcast_to(m_new[h], (1, LANES))

        mu = [jnp.mean(h_c[h], axis=-1, keepdims=True) for h in heads]
        hc = [h_c[h] - mu[h] for h in heads]
        var = [jnp.mean(hc[h] * hc[h], axis=-1, keepdims=True) for h in heads]
        for h in heads:
            hn = hc[h] * lax.rsqrt(var[h] + M_NORM_EPS) * mh_ref[:, h * M_V:(h + 1) * M_V]
            y_ref[b, :, h * M_V:(h + 1) * M_V] = _sigmoid(o_ref[b, :, h * M_V:(h + 1) * M_V]) * hn


def _dot_exact_rhs_left(ones_bf16, x):
    hi, mid, lo = _split3(x)
    return _dot(ones_bf16, hi) + _dot(ones_bf16, mid) + _dot(ones_bf16, lo)


def _mlstm(u, bsz, seq, conv_w, conv_b, i_bias, f_bias, mh_w):
    nc = seq // CHUNK
    w = M_WIDTH
    gbias = jnp.zeros((1, LANES), F32).at[0, :M_HEADS].set(i_bias).at[0, M_HEADS:2 * M_HEADS].set(f_bias)
    u3 = u.reshape(bsz, seq, u.shape[1])
    blk = lambda col: pl.BlockSpec((bsz, CHUNK, w), lambda c: (0, c, col // w))
    const = lambda shape: pl.BlockSpec(shape, lambda c: (0, 0))
    y = pl.pallas_call(
        _mlstm_kernel,
        grid=(nc,),
        in_specs=[blk(C_MQK), blk(C_MV), blk(C_MO),
                  pl.BlockSpec((bsz, CHUNK, LANES), lambda c: (0, c, C_MG // LANES)),
                  const((CONV_K, w)), const((1, w)), const((1, LANES)), const((1, w))],
        out_specs=pl.BlockSpec((bsz, CHUNK, w), lambda c: (0, c, 0)),
        out_shape=jax.ShapeDtypeStruct((bsz, seq, w), F32),
        scratch_shapes=[pltpu.VMEM((bsz, CHUNK + SUBLANES, w), F32),
                        pltpu.VMEM((bsz * M_HEADS, M_QK, M_AUG), F32),
                        pltpu.VMEM((bsz * M_HEADS, LANES), F32)],
        compiler_params=_cparams(("arbitrary",)),
        name="mlstm",
    )(u3, u3, u3, u3, conv_w, conv_b[None, :], gbias, mh_w[None, :])
    return y.reshape(bsz * seq, w)


MERGE_TM = 256
C_ROUTE_G = N_EXPERTS


def _merge_kernel(alpha, ys_ref, bonus_ref, g_ref, ym_ref, gr_ref, gm_ref, x_ref,
                  lnxw_ref, lnxb_ref, bgr_ref, bgm_ref, wbr_ref, wbm_ref, wout_ref, l1w_ref, l1b_ref,
                  wrh_ref, wrl_ref, br_ref, x1_o, ids_o, wts_o, cnt_o):
    ys = ys_ref[...]
    mu = _seg_sum(ys, R_HEAD) * (1.0 / R_HEAD)
    yc = ys - mu
    var = _seg_sum(yc * yc, R_HEAD) * (1.0 / R_HEAD)
    y = yc * lax.rsqrt(var + R_GN_EPS) * lnxw_ref[...] + lnxb_ref[...]
    y_r = (y + bonus_ref[...]) * g_ref[...]
    br = _dot(y_r.astype(BF16), wbr_ref[...])
    bm = _dot(ym_ref[...].astype(BF16), wbm_ref[...])
    mix_in = _sigmoid(gr_ref[...] + bgr_ref[...]) * br + _sigmoid(gm_ref[...] + bgm_ref[...]) * bm
    mix = _dot(mix_in.astype(BF16), wout_ref[...])
    x1 = _layer_norm(alpha * x_ref[...] + mix, l1w_ref[...], l1b_ref[...])
    x1_o[...] = x1

    xh = x1.astype(BF16)
    xl = (x1 - xh.astype(F32)).astype(BF16)
    logits = (_dot(xh, wrh_ref[...]) + (_dot(xh, wrl_ref[...]) + _dot(xl, wrh_ref[...]))) + br_ref[...]
    tm = logits.shape[0]
    lane_i = lax.broadcasted_iota(I32, (tm, LANES), 1)
    lane = lane_i.astype(F32)
    group_of_lane = (lane_i // EXPERTS_PER_GROUP).astype(F32)
    big = float(LANES)
    neg = -jnp.inf
    lg = jnp.where((lane_i >= C_ROUTE_G) & (lane_i < C_ROUTE_G + N_GROUPS), logits, neg)
    gmax = jnp.max(lg, axis=-1, keepdims=True)
    gsel = jnp.min(jnp.where(lg == gmax, lane - C_ROUTE_G, big), axis=-1, keepdims=True)
    g_w = 1.0 / jnp.sum(jnp.exp(lg - gmax), axis=-1, keepdims=True)
    le = jnp.where((lane_i < N_EXPERTS) & (group_of_lane == gsel), logits, neg)
    m1 = jnp.max(le, axis=-1, keepdims=True)
    i1 = jnp.min(jnp.where(le == m1, lane, big), axis=-1, keepdims=True)
    le2 = jnp.where(lane == i1, neg, le)
    m2 = jnp.max(le2, axis=-1, keepdims=True)
    i2 = jnp.min(jnp.where(le2 == m2, lane, big), axis=-1, keepdims=True)
    e2 = jnp.exp(m2 - m1)
    w1 = g_w / (1.0 + e2)
    w2 = g_w * e2 / (1.0 + e2)
    wts_o[...] = jnp.where(lane_i == 0, w1, jnp.where(lane_i == 1, w2, 0.0))

    oh1 = jnp.where(lane == i1, 1.0, 0.0)
    oh2 = jnp.where(lane == i2, 1.0, 0.0)
    ri = lax.broadcasted_iota(I32, (tm, tm), 0)
    ci = lax.broadcasted_iota(I32, (tm, tm), 1)
    lstrict = jnp.where(ci < ri, 1.0, 0.0).astype(BF16)
    tot1 = jnp.sum(oh1, axis=0, keepdims=True)
    tot2 = jnp.sum(oh2, axis=0, keepdims=True)
    rank1 = jnp.sum(_dot(lstrict, oh1.astype(BF16)) * oh1, axis=-1, keepdims=True)
    rank2 = jnp.sum((_dot(lstrict, oh2.astype(BF16)) + tot1) * oh2, axis=-1, keepdims=True)
    ids = jnp.where(lane_i == 0, i1, jnp.where(lane_i == 1, i2, jnp.where(lane_i == 2, rank1,
                                                                          jnp.where(lane_i == 3, rank2, 0.0))))
    ids_o[...] = ids.astype(I32)
    cnt_o[...] = jnp.broadcast_to(tot1 + tot2, cnt_o.shape).astype(I32)


def _merge(alpha, ys, bonus, g, ym, u, x, lnx_w, lnx_b, b_gate, w_br, w_bm, w_out, ln1_w, ln1_b,
           w_rg, b_rg, w_re, b_re):
    n, d = x.shape
    tm = MERGE_TM
    wr = jnp.zeros((d, LANES), F32).at[:, :N_EXPERTS].set(w_re).at[:, C_ROUTE_G:C_ROUTE_G + N_GROUPS].set(w_rg)
    wr_hi = wr.astype(BF16)
    wr_lo = (wr - wr_hi.astype(F32)).astype(BF16)
    b_r = jnp.zeros((1, LANES), F32).at[0, :N_EXPERTS].set(b_re).at[0, C_ROUTE_G:C_ROUTE_G + N_GROUPS].set(b_rg)
    tile = lambda w_: pl.BlockSpec((tm, w_), lambda i: (i, 0))
    const = lambda shape: pl.BlockSpec(shape, lambda i: (0, 0))
    return pl.pallas_call(
        functools.partial(_merge_kernel, alpha),
        grid=(n // tm,),
        in_specs=[tile(R_WIDTH), tile(R_WIDTH), tile(R_WIDTH), tile(M_WIDTH),
                  pl.BlockSpec((tm, d), lambda i: (i, C_GR // d)),
                  pl.BlockSpec((tm, d), lambda i: (i, C_GM // d)),
                  tile(d),
                  const((1, R_WIDTH)), const((1, R_WIDTH)), const((1, d)), const((1, d)),
                  const((R_WIDTH, d)), const((M_WIDTH, d)), const((d, d)), const((1, d)), const((1, d)),
                  const((d, LANES)), const((d, LANES)), const((1, LANES))],
        out_specs=[tile(d), tile(LANES), tile(LANES), pl.BlockSpec((SUBLANES, LANES), lambda i: (i, 0))],
        out_shape=[jax.ShapeDtypeStruct((n, d), F32), jax.ShapeDtypeStruct((n, LANES), I32),
                   jax.ShapeDtypeStruct((n, LANES), F32),
                   jax.ShapeDtypeStruct((n // tm * SUBLANES, LANES), I32)],
        compiler_params=_cparams(("parallel",)),
        name="merge_ln1_router",
    )(ys, bonus, g, ym, u, u, x, lnx_w[None, :], lnx_b[None, :], b_gate[None, :d], b_gate[None, d:],
      w_br.astype(BF16), w_bm.astype(BF16), w_out.astype(BF16), ln1_w[None, :], ln1_b[None, :],
      wr_hi, wr_lo, b_r)


MOE_CH = 256
DISPATCH_TM = 256


ROW_DMA_UNROLL = 8


def _each(count, fn, unroll=1):
    def body(r, carry):
        fn(r)
        return carry
    lax.fori_loop(0, count, body, 0, unroll=unroll)


def _each_choice(fn):
    def both(r):
        for j in range(TOP_K):
            fn(r, j)
    return both


def _zero_unused_tail(pstart, nchunks, zeros_ref, out_hbm, sem):
    ch = zeros_ref.shape[0]
    used = (pstart[N_EXPERTS - 1] + nchunks[N_EXPERTS - 1] * ch) // ch
    total = out_hbm.shape[0] // ch

    def chunk(c):
        return pltpu.make_async_copy(zeros_ref, out_hbm.at[pl.ds(pl.multiple_of(c * ch, ch), ch)], sem)

    def over_tail(fn):
        def body(c, carry):
            fn(c)
            return carry
        lax.fori_loop(used, total, body, 0)

    over_tail(lambda c: chunk(c).start())
    over_tail(lambda c: chunk(c).wait())


def _moe_dispatch_kernel(dest, pstart, nchunks, x_ref, xs_hbm, zbuf, sem_z, sem):
    i = pl.program_id(0)
    tm = x_ref.shape[0]
    ch = zbuf.shape[0]

    @pl.when(i == 0)
    def _():
        zbuf[...] = jnp.zeros(zbuf.shape, F32)

        def tail(e):
            row0 = pl.multiple_of(pstart[e] + (nchunks[e] - 1) * ch, ch)
            return pltpu.make_async_copy(zbuf, xs_hbm.at[pl.ds(row0, ch)], sem_z)

        def start(e):
            @pl.when(nchunks[e] > 0)
            def _():
                tail(e).start()

        def wait(e):
            @pl.when(nchunks[e] > 0)
            def _():
                tail(e).wait()

        _each(N_EXPERTS, start)
        _each(N_EXPERTS, wait)
        _zero_unused_tail(pstart, nchunks, zbuf, xs_hbm, sem_z)

    def row(r, j):
        return pltpu.make_async_copy(x_ref.at[pl.ds(r, 1)],
                                     xs_hbm.at[pl.ds(dest[(i * tm + r) * TOP_K + j], 1)], sem)

    _each(tm, _each_choice(lambda r, j: row(r, j).start(priority=j)), ROW_DMA_UNROLL)
    _each(tm, _each_choice(lambda r, j: row(r, j).wait()), ROW_DMA_UNROLL)


def _moe_dispatch(x1, dest, pstart, nchunks, rows_pad):
    n, d = x1.shape
    tm = DISPATCH_TM
    return pl.pallas_call(
        _moe_dispatch_kernel,
        grid_spec=pltpu.PrefetchScalarGridSpec(
            num_scalar_prefetch=3,
            grid=(n // tm,),
            in_specs=[pl.BlockSpec((tm, d), lambda i, *_: (i, 0))],
            out_specs=pl.BlockSpec(memory_space=pl.ANY),
            scratch_shapes=[pltpu.VMEM((MOE_CH, d), F32), pltpu.SemaphoreType.DMA, pltpu.SemaphoreType.DMA]),
        out_shape=jax.ShapeDtypeStruct((rows_pad, d), F32),
        compiler_params=_cparams(("arbitrary",)),
        name="moe_dispatch",
    )(dest, pstart, nchunks, x1)


def _moe_expert_kernel(pstart, nchunks, xs_hbm, wg_ref, wu_ref, wd_ref, ys_hbm,
                       xbuf, ybuf, wgb, wub, wdb, sem_in, sem_out):
    e = pl.program_id(0)
    ch = xbuf.shape[1]
    nc = nchunks[e]
    g0 = pstart[e] // ch
    total = (pstart[N_EXPERTS - 1] + nchunks[N_EXPERTS - 1] * ch) // ch

    def rows(g):
        return pl.ds(pl.multiple_of(g * ch, ch), ch)

    def load(g):
        return pltpu.make_async_copy(xs_hbm.at[rows(g)], xbuf.at[g % 2], sem_in.at[g % 2])

    def store(g):
        return pltpu.make_async_copy(ybuf.at[g % 2], ys_hbm.at[rows(g)], sem_out.at[g % 2])

    @pl.when(nc > 0)
    def _():
        @pl.when(g0 == 0)
        def _():
            load(0).start()

        wgb[...] = wg_ref[0].astype(BF16)
        wub[...] = wu_ref[0].astype(BF16)
        wdb[...] = wd_ref[0].astype(BF16)

        def chunk(c, carry):
            g = g0 + c
            load(g).wait()

            @pl.when(g + 1 < total)
            def _():
                load(g + 1).start()

            @pl.when(g >= 2)
            def _():
                store(g - 2).wait()

            xb = xbuf[g % 2].astype(BF16)
            gate = _dot(xb, wgb[...])
            hb = gate * _sigmoid(gate) * _dot(xb, wub[...])
            ybuf[g % 2] = _dot(hb.astype(BF16), wdb[...])
            store(g).start()
            return carry

        lax.fori_loop(0, nc, chunk, 0)

    @pl.when(e == pl.num_programs(0) - 1)
    def _():
        @pl.when(total >= 2)
        def _():
            store(total - 2).wait()

        @pl.when(total >= 1)
        def _():
            store(total - 1).wait()

        ybuf[0] = jnp.zeros(ybuf.shape[1:], F32)
        _zero_unused_tail(pstart, nchunks, ybuf.at[0], ys_hbm, sem_out.at[0])


def _moe_experts(xs, pstart, nchunks, w_gate, w_up, w_down):
    rows_pad, d = xs.shape
    de = w_gate.shape[-1]
    wspec = lambda shape: pl.BlockSpec(shape, lambda e, *_: (e, 0, 0))
    return pl.pallas_call(
        _moe_expert_kernel,
        grid_spec=pltpu.PrefetchScalarGridSpec(
            num_scalar_prefetch=2,
            grid=(N_EXPERTS,),
            in_specs=[pl.BlockSpec(memory_space=pl.ANY),
                      wspec((1, d, de)), wspec((1, d, de)), wspec((1, de, d))],
            out_specs=pl.BlockSpec(memory_space=pl.ANY),
            scratch_shapes=[pltpu.VMEM((2, MOE_CH, d), F32), pltpu.VMEM((2, MOE_CH, d), F32),
                            pltpu.VMEM((d, de), BF16), pltpu.VMEM((d, de), BF16), pltpu.VMEM((de, d), BF16),
                            pltpu.SemaphoreType.DMA((2,)), pltpu.SemaphoreType.DMA((2,))]),
        out_shape=jax.ShapeDtypeStruct((rows_pad, d), F32),
        compiler_params=_cparams(("arbitrary",)),
        name="moe_experts",
    )(pstart, nchunks, xs, w_gate, w_up, w_down)


def _moe_plan(ids, cnt, tm):
    n = ids.shape[0]
    tile_cnt = cnt[::SUBLANES, :N_EXPERTS]
    counts = jnp.sum(tile_cnt, axis=0)
    padded = ((counts + MOE_CH - 1) // MOE_CH) * MOE_CH
    pstart = jnp.cumsum(padded) - padded
    tile_base = pstart[None, :] + jnp.cumsum(tile_cnt, axis=0) - tile_cnt
    eid = ids[:, 0:TOP_K]
    rank = ids[:, TOP_K:2 * TOP_K]
    base_of_tok = jnp.repeat(tile_base, tm, axis=0)[:, None, :]
    chosen = eid[:, :, None] == jnp.arange(N_EXPERTS, dtype=I32)[None, None, :]
    dest = jnp.sum(jnp.where(chosen, base_of_tok, 0), axis=-1) + rank
    return dest.reshape(-1).astype(I32), pstart.astype(I32), (padded // MOE_CH).astype(I32)


FINAL_TM = 256


def _final_kernel(alpha, dest, x1_ref, wts_ref, p_ref, wpg_ref, wple_ref, l2w_ref, l2b_ref, ys_hbm, o_ref,
                  ybuf, sem):
    i = pl.program_id(0)
    tm = x1_ref.shape[0]
    slot = i % 2

    def row(tile, buf, r, j):
        return pltpu.make_async_copy(ys_hbm.at[pl.ds(dest[(tile * tm + r) * TOP_K + j], 1)],
                                     ybuf.at[buf, j, pl.ds(r, 1)], sem.at[buf])

    def gather(tile, buf):
        _each(tm, _each_choice(lambda r, j: row(tile, buf, r, j).start(priority=j)), ROW_DMA_UNROLL)

    @pl.when(i == 0)
    def _():
        gather(0, 0)

    @pl.when(i + 1 < pl.num_programs(0))
    def _():
        gather(i + 1, 1 - slot)

    x1 = x1_ref[...]
    ple = _sigmoid(_dot(x1.astype(BF16), wpg_ref[...])) * _dot(p_ref[...], wple_ref[...])
    _each(tm, _each_choice(lambda r, j: row(i, slot, r, j).wait()), ROW_DMA_UNROLL)
    moe = ybuf[slot, 0] * wts_ref[:, 0:1] + ybuf[slot, 1] * wts_ref[:, 1:2]
    o_ref[...] = _layer_norm(alpha * x1 + moe + ple, l2w_ref[...], l2b_ref[...])


def _final(alpha, dest, x1, ys, wts, p_bf, w_pg, w_ple, ln2_w, ln2_b):
    n, d = x1.shape
    tm = FINAL_TM
    tile = lambda w_: pl.BlockSpec((tm, w_), lambda i, *_: (i, 0))
    const = lambda shape: pl.BlockSpec(shape, lambda i, *_: (0, 0))
    return pl.pallas_call(
        functools.partial(_final_kernel, alpha),
        grid_spec=pltpu.PrefetchScalarGridSpec(
            num_scalar_prefetch=1,
            grid=(n // tm,),
            in_specs=[tile(d), tile(LANES), tile(p_bf.shape[1]),
                      const((d, d)), const((p_bf.shape[1], d)), const((1, d)), const((1, d)),
                      pl.BlockSpec(memory_space=pl.ANY)],
            out_specs=tile(d),
            scratch_shapes=[pltpu.VMEM((2, TOP_K, tm, d), F32), pltpu.SemaphoreType.DMA((2,))]),
        out_shape=jax.ShapeDtypeStruct((n, d), F32),
        compiler_params=_cparams(("arbitrary",)),
        name="final_ln2",
    )(dest, x1, wts, p_bf, w_pg.astype(BF16), w_ple.astype(BF16), ln2_w[None, :], ln2_b[None, :], ys)


def _regroup_w_in(w):
    m0 = RWKV_COLS
    g0 = RWKV_COLS + MLSTM_COLS
    mqk = 2 * M_HEADS * M_QK
    pad = lambda c: jnp.zeros((w.shape[0], c), w.dtype)
    parts = [w[:, 0:3 * R_WIDTH],
             w[:, m0:m0 + mqk],
             w[:, m0 + mqk:m0 + mqk + M_WIDTH],
             w[:, m0 + mqk + M_WIDTH + 2 * M_HEADS:m0 + MLSTM_COLS],
             w[:, g0:],
             w[:, 3 * R_WIDTH:RWKV_COLS], pad(LORA_PAD - LORA_COLS),
             w[:, m0 + mqk + M_WIDTH:m0 + mqk + M_WIDTH + 2 * M_HEADS], pad(LANES - 2 * M_HEADS)]
    out = jnp.concatenate(parts, axis=1)
    assert out.shape[1] == C_TOTAL
    return out


def kernel(x, p, w_in, mu_shift, w0, w_w2, a0, w_a2, w_g2, k_k, k_a, r_k, lnx_w, lnx_b, conv_w, conv_b,
           i_bias, f_bias, mh_w, b_gate, w_br, w_bm, w_out, ln1_w, ln1_b, w_rg, b_rg, w_re, b_re,
           w_gate, w_up, w_down, w_pg, w_ple, ln2_w, ln2_b):
    bsz, seq, d = x.shape
    depth = w_in.shape[0]
    assert bsz * R_HEADS * 2 == LANES and seq % PROJ_TM == 0 and seq % CHUNK == 0
    alpha = (2 * depth) ** 0.25
    n = bsz * seq
    xf = x.reshape(n, d)
    for i in range(depth):
        u = _proj_in(xf.astype(BF16), _regroup_w_in(w_in[i]).astype(BF16))
        r, dec, kp, v, kn, beta, g, bonus = _rwkv_prep(
            u, seq, mu_shift[i], w0[i], w_w2[i], a0[i], w_a2[i], w_g2[i], k_k[i], k_a[i], r_k[i])
        ys = _rwkv_scan(kn, dec, kp, beta, r, v, bsz, seq)
        ym = _mlstm(u, bsz, seq, conv_w[i], conv_b[i], i_bias[i], f_bias[i], mh_w[i])
        x1, ids, wts, cnt = _merge(alpha, ys, bonus, g, ym, u, xf, lnx_w[i], lnx_b[i], b_gate[i], w_br[i],
                                   w_bm[i], w_out[i], ln1_w[i], ln1_b[i], w_rg[i], b_rg[i], w_re[i], b_re[i])
        dest, pstart, nchunks = _moe_plan(ids, cnt, MERGE_TM)
        rows_pad = TOP_K * n + N_EXPERTS * MOE_CH
        xs = _moe_dispatch(x1, dest, pstart, nchunks, rows_pad)
        ys_sorted = _moe_experts(xs, pstart, nchunks, w_gate[i], w_up[i], w_down[i])
        xf = _final(alpha, dest, x1, ys_sorted, wts, p[i].reshape(n, -1).astype(BF16), w_pg[i], w_ple[i],
                    ln2_w[i], ln2_b[i])
    return xf.reshape(bsz, seq, d)
```

```python
import functools

import jax
import jax.numpy as jnp
from jax import lax
from jax.experimental import pallas as pl
from jax.experimental.pallas import tpu as pltpu

F32 = jnp.float32
BF16 = jnp.bfloat16
I32 = jnp.int32

R_HEADS, R_HEAD = 16, 64
R_WIDTH = R_HEADS * R_HEAD
DECAY_LORA, AAA_LORA, GATE_LORA = 64, 64, 160
LORA_COLS = DECAY_LORA + AAA_LORA + GATE_LORA
LORA_PAD = 512
R_GN_EPS = 64e-5
RWKV_COLS = 3 * R_WIDTH + LORA_COLS
M_HEADS, M_QK, M_V = 8, 64, 128
M_WIDTH = M_HEADS * M_V
CONV_K = 4
CHUNK = 128
M_NORM_EPS = 1e-6
MLSTM_COLS = 2 * M_HEADS * M_QK + 2 * M_WIDTH + 2 * M_HEADS
N_GROUPS, EXPERTS_PER_GROUP = 4, 8
N_EXPERTS = N_GROUPS * EXPERTS_PER_GROUP
TOP_K = 2
MOE_BLOCK = 128
LN_EPS = 1e-5

LANES = 128
SUBLANES = 8
MXU_DIM = 256
V7X_VMEM_BYTES = 64 * 1024 * 1024
VMEM_LIMIT = 56 * 1024 * 1024

C_RKV = 0
C_MQK = 3072
C_MV = 4096
C_MO = 5120
C_GR = 6144
C_GM = 8192
C_LORA = 10240
C_MG = 10752
C_TOTAL = 10880
PROJ_TN = 2176
PROJ_TM = 512


def _cparams(sem, vmem=VMEM_LIMIT):
    return pltpu.CompilerParams(dimension_semantics=sem, vmem_limit_bytes=vmem)


def _sigmoid(x):
    return 0.5 * jnp.tanh(0.5 * x) + 0.5


def _softplus(x):
    return jnp.maximum(x, 0.0) + jnp.log1p(jnp.exp(-jnp.abs(x)))


def _split3(x):
    hi = x.astype(BF16)
    r1 = x - hi.astype(F32)
    mid = r1.astype(BF16)
    lo = (r1 - mid.astype(F32)).astype(BF16)
    return hi, mid, lo


def _dot(a, b):
    return jnp.dot(a, b, preferred_element_type=F32)


def _dot_nt(a, b):
    return lax.dot_general(a, b, (((1,), (1,)), ((), ())), preferred_element_type=F32)


def _dot_tn(a, b):
    return lax.dot_general(a, b, (((0,), (0,)), ((), ())), preferred_element_type=F32)


def _dot_exact_rhs(x, ones_bf16, terms=3):
    parts = _split3(x)[:terms]
    acc = _dot(parts[0], ones_bf16)
    for part in parts[1:]:
        acc = acc + _dot(part, ones_bf16)
    return acc


def _block_ones(n, group):
    r = lax.broadcasted_iota(I32, (n, n), 0) // group
    c = lax.broadcasted_iota(I32, (n, n), 1) // group
    return jnp.where(r == c, 1.0, 0.0).astype(BF16)


def _seg_sum(x, group):
    ones = _block_ones(MXU_DIM, group)
    slabs = [_dot_exact_rhs(x[:, p * MXU_DIM:(p + 1) * MXU_DIM], ones, terms=2)
             for p in range(x.shape[1] // MXU_DIM)]
    return jnp.concatenate(slabs, axis=1)


def _layer_norm(x, w, b):
    mu = jnp.mean(x, axis=-1, keepdims=True)
    xc = x - mu
    var = jnp.mean(xc * xc, axis=-1, keepdims=True)
    return xc * lax.rsqrt(var + LN_EPS) * w + b


def _proj_kernel(x_ref, w_ref, o_ref):
    o_ref[...] = _dot(x_ref[...], w_ref[...])


def _proj_in(x_bf, w_bf):
    m, k = x_bf.shape
    n = w_bf.shape[1]
    return pl.pallas_call(
        _proj_kernel,
        grid=(n // PROJ_TN, m // PROJ_TM),
        in_specs=[pl.BlockSpec((PROJ_TM, k), lambda j, i: (i, 0)),
                  pl.BlockSpec((k, PROJ_TN), lambda j, i: (0, j))],
        out_specs=pl.BlockSpec((PROJ_TM, PROJ_TN), lambda j, i: (i, j)),
        out_shape=jax.ShapeDtypeStruct((m, n), F32),
        compiler_params=_cparams(("parallel", "parallel")),
        name="proj_in",
    )(x_bf, w_bf)


PREP_TM = 256


def _rwkv_prep_kernel(seq, u_ref, l_ref, up_ref, lp_ref, mu_ref, mul_ref, w0_ref, a0_ref, kk_ref, ka_ref,
                      rk_ref, ww_ref, wa_ref, wg_ref,
                      r_o, dec_o, kp_o, v_o, kn_o, beta_o, g_o, bonus_o):
    i = pl.program_id(0)
    tm = u_ref.shape[0]
    first = (i * tm) % seq == 0
    row = lax.broadcasted_iota(I32, (tm, 1), 0)

    def shift(u, prev8):
        prev_row = jnp.where(first, 0.0, prev8[SUBLANES - 1:SUBLANES, :])
        return jnp.where(row == 0, prev_row, pltpu.roll(u, 1, 0))

    u = u_ref[...]
    z = u + mu_ref[...] * (shift(u, up_ref[...]) - u)
    lo = l_ref[...]
    zl = lo + mul_ref[...] * (shift(lo, lp_ref[...]) - lo)

    r = z[:, 0:R_WIDTH]
    k = z[:, R_WIDTH:2 * R_WIDTH]
    v = z[:, 2 * R_WIDTH:3 * R_WIDTH]
    w_pre = w0_ref[...] + _dot(jnp.tanh(zl).astype(BF16), ww_ref[...])
    w = -_softplus(-w_pre) - 0.5
    dec = jnp.exp(-jnp.exp(w))
    a = _sigmoid(a0_ref[...] + _dot(zl.astype(BF16), wa_ref[...]))
    g = _dot(_sigmoid(zl).astype(BF16), wg_ref[...])

    kk = k * kk_ref[...]
    nrm = jnp.sqrt(_seg_sum(kk * kk, R_HEAD))
    kn = kk / jnp.maximum(nrm, 1e-12)
    kp = k * (1.0 + (a - 1.0) * ka_ref[...])
    bonus = _seg_sum(r * kp * rk_ref[...], R_HEAD) * v

    r_o[...] = r
    dec_o[...] = dec
    kp_o[...] = kp
    v_o[...] = v
    kn_o[...] = kn
    beta_o[...] = kn * a
    g_o[...] = g
    bonus_o[...] = bonus


def _rwkv_prep(u, seq, mu, w0, w_w2, a0, w_a2, w_g2, k_k, k_a, r_k):
    n = u.shape[0]
    tm = PREP_TM
    mu_rkv = mu[None, :3 * R_WIDTH]
    mu_lora = jnp.zeros((1, LORA_PAD), F32).at[0, :LORA_COLS].set(mu[3 * R_WIDTH:])
    ww = jnp.zeros((LORA_PAD, R_WIDTH), BF16).at[0:DECAY_LORA].set(w_w2.astype(BF16))
    wa = jnp.zeros((LORA_PAD, R_WIDTH), BF16).at[DECAY_LORA:DECAY_LORA + AAA_LORA].set(w_a2.astype(BF16))
    wg = jnp.zeros((LORA_PAD, R_WIDTH), BF16).at[DECAY_LORA + AAA_LORA:LORA_COLS].set(w_g2.astype(BF16))
    row = lambda t: t.reshape(1, R_WIDTH)
    prev_blk = lambda i: jnp.maximum(i * (tm // SUBLANES) - 1, 0)
    const = lambda shape: pl.BlockSpec(shape, lambda i: (0, 0))
    nat = jax.ShapeDtypeStruct((n, R_WIDTH), F32)
    return pl.pallas_call(
        functools.partial(_rwkv_prep_kernel, seq),
        grid=(n // tm,),
        in_specs=[pl.BlockSpec((tm, 3 * R_WIDTH), lambda i: (i, C_RKV // (3 * R_WIDTH))),
                  pl.BlockSpec((tm, LORA_PAD), lambda i: (i, C_LORA // LORA_PAD)),
                  pl.BlockSpec((SUBLANES, 3 * R_WIDTH), lambda i: (prev_blk(i), C_RKV // (3 * R_WIDTH))),
                  pl.BlockSpec((SUBLANES, LORA_PAD), lambda i: (prev_blk(i), C_LORA // LORA_PAD)),
                  const((1, 3 * R_WIDTH)), const((1, LORA_PAD)),
                  const((1, R_WIDTH)), const((1, R_WIDTH)), const((1, R_WIDTH)), const((1, R_WIDTH)),
                  const((1, R_WIDTH)),
                  const((LORA_PAD, R_WIDTH)), const((LORA_PAD, R_WIDTH)), const((LORA_PAD, R_WIDTH))],
        out_specs=[pl.BlockSpec((tm, R_WIDTH), lambda i: (i, 0))] * 8,
        out_shape=[nat] * 8,
        compiler_params=_cparams(("parallel",)),
        name="rwkv_prep",
    )(u, u, u, u, mu_rkv, mu_lora, row(w0), row(a0), row(k_k), row(k_a), row(r_k), ww, wa, wg)


SCAN_TB = LANES
SCAN_V = R_HEAD // 2
SCAN_ACC = 4
SCAN_KOPS = 5
Z_PITCH = R_HEAD + SUBLANES
V_PITCH = SCAN_V + SUBLANES
RELAYOUT_UNROLL = 8


def _rwkv_scan_kernel(kn_hbm, dec_hbm, kp_hbm, beta_hbm, r_hbm, v_hbm, y_ref,
                      s_ref, stage, z_ref, xs_ref, vs_ref, ys_ref, sem):
    i = pl.program_id(0)
    nb = stage.shape[1]
    tb = SCAN_TB
    slab = R_HEADS * Z_PITCH

    @pl.when(i == 0)
    def _():
        s_ref[...] = jnp.zeros(s_ref.shape, F32)

    srcs = (kn_hbm, dec_hbm, kp_hbm, beta_hbm, r_hbm, v_hbm)

    def fetch(j):
        return pltpu.make_async_copy(srcs[j].at[:, pl.ds(i * tb, tb), :], stage.at[j % 2], sem.at[j % 2])

    def to_time_on_lanes(slot):
        for b in range(nb):
            for hp in range(R_HEADS // 2):
                tt = stage[slot, b, :, hp * LANES:(hp + 1) * LANES].T
                for hh in range(2):
                    z_ref[pl.ds((b * R_HEADS + 2 * hp + hh) * Z_PITCH, R_HEAD), :] = tt[hh * R_HEAD:(hh + 1) * R_HEAD]

    def head_rows(b, c):
        return z_ref[pl.ds(b * slab + c, R_HEADS, stride=Z_PITCH), :]

    def key_operand(op):
        def body(k, carry):
            rows = [head_rows(b, k) for b in range(nb)]
            xs_ref[op, k] = jnp.concatenate(rows + rows, axis=0).T
            return carry
        lax.fori_loop(0, R_HEAD, body, 0, unroll=RELAYOUT_UNROLL)

    def value_operand():
        def body(vp, carry):
            rows = [head_rows(b, vh * SCAN_V + vp) for vh in range(2) for b in range(nb)]
            vs_ref[pl.ds(vp, tb, stride=V_PITCH), :] = jnp.concatenate(rows, axis=0).T
            return carry
        lax.fori_loop(0, SCAN_V, body, 0, unroll=RELAYOUT_UNROLL)

    fetch(0).start()
    fetch(1).start()
    for j in range(SCAN_KOPS + 1):
        fetch(j).wait()
        to_time_on_lanes(j % 2)
        if j + 2 <= SCAN_KOPS:
            fetch(j + 2).start()
        if j < SCAN_KOPS:
            key_operand(j)
        else:
            value_operand()

    kn_ref, dec_ref, kp_ref, beta_ref, r_ref = (xs_ref.at[op] for op in range(SCAN_KOPS))

    def total(acc):
        return (acc[0] + acc[1]) + (acc[2] + acc[3])

    def add_term(acc, k, term):
        acc[k % SCAN_ACC] = term if acc[k % SCAN_ACC] is None else acc[k % SCAN_ACC] + term

    acc = [None] * SCAN_ACC
    for k in range(R_HEAD):
        add_term(acc, k, s_ref[k] * kn_ref[k, 0:1, :])

    def step(t, s_kk):
        row0 = pl.multiple_of(t * V_PITCH, SUBLANES)
        v_t = vs_ref[pl.ds(row0, SCAN_V), :]
        t_next = jnp.minimum(t + 1, tb - 1)
        acc_y = [None] * SCAN_ACC
        acc_s = [None] * SCAN_ACC
        for k in range(R_HEAD):
            s_new = (s_ref[k] * dec_ref[k, pl.ds(t, 1), :]
                     + (v_t * kp_ref[k, pl.ds(t, 1), :] - s_kk * beta_ref[k, pl.ds(t, 1), :]))
            s_ref[k] = s_new
            add_term(acc_y, k, s_new * r_ref[k, pl.ds(t, 1), :])
            add_term(acc_s, k, s_new * kn_ref[k, pl.ds(t_next, 1), :])
        ys_ref[pl.ds(row0, SCAN_V), :] = total(acc_y)
        return total(acc_s)

    lax.fori_loop(0, tb, step, total(acc))

    def out_rows(vp, carry):
        tt = ys_ref[pl.ds(vp, tb, stride=V_PITCH), :].T
        for vh in range(2):
            for b in range(nb):
                g = vh * nb + b
                z_ref[pl.ds(b * slab + vh * SCAN_V + vp, R_HEADS, stride=Z_PITCH), :] = \
                    tt[g * R_HEADS:(g + 1) * R_HEADS]
        return carry
    lax.fori_loop(0, SCAN_V, out_rows, 0, unroll=RELAYOUT_UNROLL)
    for b in range(nb):
        for hp in range(R_HEADS // 2):
            pair = [z_ref[pl.ds((b * R_HEADS + 2 * hp + hh) * Z_PITCH, R_HEAD), :] for hh in range(2)]
            y_ref[b, :, hp * LANES:(hp + 1) * LANES] = jnp.concatenate(pair, axis=0).T


def _rwkv_scan(kn, dec, kp, beta, r, v, bsz, seq):
    shape3 = (bsz, seq, R_WIDTH)
    ops = [a.reshape(shape3) for a in (kn, dec, kp, beta, r, v)]
    y = pl.pallas_call(
        _rwkv_scan_kernel,
        grid=(seq // SCAN_TB,),
        in_specs=[pl.BlockSpec(memory_space=pl.ANY)] * len(ops),
        out_specs=pl.BlockSpec((bsz, SCAN_TB, R_WIDTH), lambda i: (0, i, 0)),
        out_shape=jax.ShapeDtypeStruct(shape3, F32),
        scratch_shapes=[pltpu.VMEM((R_HEAD, SCAN_V, LANES), F32),
                        pltpu.VMEM((2, bsz, SCAN_TB, R_WIDTH), F32),
                        pltpu.VMEM((bsz * R_HEADS * Z_PITCH, SCAN_TB), F32),
                        pltpu.VMEM((SCAN_KOPS, R_HEAD, SCAN_TB, LANES), F32),
                        pltpu.VMEM((SCAN_TB * V_PITCH, LANES), F32),
                        pltpu.VMEM((SCAN_TB * V_PITCH, LANES), F32),
                        pltpu.SemaphoreType.DMA((2,))],
        compiler_params=_cparams(("arbitrary",)),
        name="rwkv_scan",
    )(*ops)
    return y.reshape(bsz * seq, R_WIDTH)


M_AUG = 2 * M_V


def _mlstm_kernel(qk_ref, v_ref, o_ref, g_ref, cw_ref, cb_ref, gb_ref, mh_ref, y_ref,
                  ext_ref, c_ref, m_ref):
    c_idx = pl.program_id(0)
    L = CHUNK
    nb = qk_ref.shape[0]

    @pl.when(c_idx == 0)
    def _():
        ext_ref[:, 0:SUBLANES, :] = jnp.zeros((nb, SUBLANES, ext_ref.shape[2]), F32)
        c_ref[...] = jnp.zeros(c_ref.shape, F32)
        m_ref[...] = jnp.full(m_ref.shape, -jnp.inf, F32)

    lane = lax.broadcasted_iota(I32, (L, LANES), 1)
    ti = lax.broadcasted_iota(I32, (L, L), 0)
    si = lax.broadcasted_iota(I32, (L, L), 1)
    causal = si <= ti
    ltri = jnp.where(causal, 1.0, 0.0).astype(BF16)
    utri = jnp.where(ti <= si, 1.0, 0.0).astype(BF16)
    ones_col = jnp.where(lane == 0, 1.0, 0.0).astype(BF16)

    for b in range(nb):
        ext_ref[b, SUBLANES:SUBLANES + L, :] = qk_ref[b]
        conv = cb_ref[...]
        for j in range(CONV_K):
            off = SUBLANES - (CONV_K - 1) + j
            conv = conv + cw_ref[j:j + 1, :] * ext_ref[b, off:off + L, :]
        ext_ref[b, 0:SUBLANES, :] = ext_ref[b, L:L + SUBLANES, :]
        qk = conv * _sigmoid(conv)
        q_all = qk[:, :M_HEADS * M_QK].astype(BF16)
        k_all = qk[:, M_HEADS * M_QK:] * (M_QK ** -0.5)

        gpre = g_ref[b] + gb_ref[...]
        gcols = jnp.where(lane < M_HEADS, gpre, -_softplus(-gpre))
        grows = gcols.T
        acols = _dot_exact_rhs_left(ltri, gcols)
        arows = _dot_exact_rhs(grows, utri)

        heads = range(M_HEADS)
        a_col = [acols[:, M_HEADS + h:M_HEADS + h + 1] for h in heads]
        i_col = [gcols[:, h:h + 1] for h in heads]
        m_st = [m_ref[b * M_HEADS + h:b * M_HEADS + h + 1, 0:1] for h in heads]
        d = [jnp.where(causal, a_col[h] - arows[M_HEADS + h:M_HEADS + h + 1, :] + grows[h:h + 1, :], -jnp.inf)
             for h in heads]
        dmax = [jnp.max(d[h], axis=-1, keepdims=True) for h in heads]
        inter = [a_col[h] + m_st[h] for h in heads]
        m_t = [jnp.maximum(inter[h], dmax[h]) for h in heads]
        q = [q_all[:, h * M_QK:(h + 1) * M_QK] for h in heads]
        k = [k_all[:, h * M_QK:(h + 1) * M_QK] for h in heads]
        v_aug = [jnp.concatenate([v_ref[b, :, h * M_V:(h + 1) * M_V].astype(BF16), ones_col], axis=1)
                 for h in heads]
        s = [_dot_nt(q[h], k[h].astype(BF16)) * jnp.exp(d[h] - m_t[h]) for h in heads]
        ie = [jnp.exp(inter[h] - m_t[h]) for h in heads]
        tot = [ie[h] * _dot(q[h], c_ref[b * M_HEADS + h].astype(BF16)) + _dot(s[h].astype(BF16), v_aug[h])
               for h in heads]
        h_c = [tot[h][:, :M_V] / jnp.maximum(jnp.abs(tot[h][:, M_V:M_V + 1]), jnp.exp(-m_t[h])) for h in heads]

        a_tot = [a_col[h][L - 1:L, :] for h in heads]
        gl = [a_tot[h] - a_col[h] + i_col[h] for h in heads]
        m_new = [jnp.maximum(a_tot[h] + m_st[h], jnp.max(gl[h], axis=0, keepdims=True)) for h in heads]
        kg = [(k[h] * jnp.exp(gl[h] - m_new[h])).astype(BF16) for h in heads]
        for h in heads:
            bh = b * M_HEADS + h
            c_ref[bh] = jnp.exp(a_tot[h] + m_st[h] - m_new[h]) * c_ref[bh] + _dot_tn(kg[h], v_aug[h])
            m_ref[bh:bh + 1, :] = jnp.broadcast_to(m_new[h], (1, LANES))

        mu = [jnp.mean(h_c[h], axis=-1, keepdims=True) for h in heads]
        hc = [h_c[h] - mu[h] for h in heads]
        var = [jnp.mean(hc[h] * hc[h], axis=-1, keepdims=True) for h in heads]
        for h in heads:
            hn = hc[h] * lax.rsqrt(var[h] + M_NORM_EPS) * mh_ref[:, h * M_V:(h + 1) * M_V]
            y_ref[b, :, h * M_V:(h + 1) * M_V] = _sigmoid(o_ref[b, :, h * M_V:(h + 1) * M_V]) * hn


def _dot_exact_rhs_left(ones_bf16, x):
    hi, mid, lo = _split3(x)
    return _dot(ones_bf16, hi) + _dot(ones_bf16, mid) + _dot(ones_bf16, lo)


def _mlstm(u, bsz, seq, conv_w, conv_b, i_bias, f_bias, mh_w):
    nc = seq // CHUNK
    w = M_WIDTH
    gbias = jnp.zeros((1, LANES), F32).at[0, :M_HEADS].set(i_bias).at[0, M_HEADS:2 * M_HEADS].set(f_bias)
    u3 = u.reshape(bsz, seq, u.shape[1])
    blk = lambda col: pl.BlockSpec((bsz, CHUNK, w), lambda c: (0, c, col // w))
    const = lambda shape: pl.BlockSpec(shape, lambda c: (0, 0))
    y = pl.pallas_call(
        _mlstm_kernel,
        grid=(nc,),
        in_specs=[blk(C_MQK), blk(C_MV), blk(C_MO),
                  pl.BlockSpec((bsz, CHUNK, LANES), lambda c: (0, c, C_MG // LANES)),
                  const((CONV_K, w)), const((1, w)), const((1, LANES)), const((1, w))],
        out_specs=pl.BlockSpec((bsz, CHUNK, w), lambda c: (0, c, 0)),
        out_shape=jax.ShapeDtypeStruct((bsz, seq, w), F32),
        scratch_shapes=[pltpu.VMEM((bsz, CHUNK + SUBLANES, w), F32),
                        pltpu.VMEM((bsz * M_HEADS, M_QK, M_AUG), F32),
                        pltpu.VMEM((bsz * M_HEADS, LANES), F32)],
        compiler_params=_cparams(("arbitrary",)),
        name="mlstm",
    )(u3, u3, u3, u3, conv_w, conv_b[None, :], gbias, mh_w[None, :])
    return y.reshape(bsz * seq, w)


MERGE_TM = 256
C_ROUTE_G = N_EXPERTS


def _merge_kernel(alpha, ys_ref, bonus_ref, g_ref, ym_ref, gr_ref, gm_ref, x_ref,
                  lnxw_ref, lnxb_ref, bgr_ref, bgm_ref, wbr_ref, wbm_ref, wout_ref, l1w_ref, l1b_ref,
                  wrh_ref, wrl_ref, br_ref, x1_o, ids_o, wts_o, cnt_o):
    ys = ys_ref[...]
    mu = _seg_sum(ys, R_HEAD) * (1.0 / R_HEAD)
    yc = ys - mu
    var = _seg_sum(yc * yc, R_HEAD) * (1.0 / R_HEAD)
    y = yc * lax.rsqrt(var + R_GN_EPS) * lnxw_ref[...] + lnxb_ref[...]
    y_r = (y + bonus_ref[...]) * g_ref[...]
    br = _dot(y_r.astype(BF16), wbr_ref[...])
    bm = _dot(ym_ref[...].astype(BF16), wbm_ref[...])
    mix_in = _sigmoid(gr_ref[...] + bgr_ref[...]) * br + _sigmoid(gm_ref[...] + bgm_ref[...]) * bm
    mix = _dot(mix_in.astype(BF16), wout_ref[...])
    x1 = _layer_norm(alpha * x_ref[...] + mix, l1w_ref[...], l1b_ref[...])
    x1_o[...] = x1

    xh = x1.astype(BF16)
    xl = (x1 - xh.astype(F32)).astype(BF16)
    logits = (_dot(xh, wrh_ref[...]) + (_dot(xh, wrl_ref[...]) + _dot(xl, wrh_ref[...]))) + br_ref[...]
    tm = logits.shape[0]
    lane_i = lax.broadcasted_iota(I32, (tm, LANES), 1)
    lane = lane_i.astype(F32)
    group_of_lane = (lane_i // EXPERTS_PER_GROUP).astype(F32)
    big = float(LANES)
    neg = -jnp.inf
    lg = jnp.where((lane_i >= C_ROUTE_G) & (lane_i < C_ROUTE_G + N_GROUPS), logits, neg)
    gmax = jnp.max(lg, axis=-1, keepdims=True)
    gsel = jnp.min(jnp.where(lg == gmax, lane - C_ROUTE_G, big), axis=-1, keepdims=True)
    g_w = 1.0 / jnp.sum(jnp.exp(lg - gmax), axis=-1, keepdims=True)
    le = jnp.where((lane_i < N_EXPERTS) & (group_of_lane == gsel), logits, neg)
    m1 = jnp.max(le, axis=-1, keepdims=True)
    i1 = jnp.min(jnp.where(le == m1, lane, big), axis=-1, keepdims=True)
    le2 = jnp.where(lane == i1, neg, le)
    m2 = jnp.max(le2, axis=-1, keepdims=True)
    i2 = jnp.min(jnp.where(le2 == m2, lane, big), axis=-1, keepdims=True)
    e2 = jnp.exp(m2 - m1)
    w1 = g_w / (1.0 + e2)
    w2 = g_w * e2 / (1.0 + e2)
    wts_o[...] = jnp.where(lane_i == 0, w1, jnp.where(lane_i == 1, w2, 0.0))

    oh1 = jnp.where(lane == i1, 1.0, 0.0)
    oh2 = jnp.where(lane == i2, 1.0, 0.0)
    ri = lax.broadcasted_iota(I32, (tm, tm), 0)
    ci = lax.broadcasted_iota(I32, (tm, tm), 1)
    lstrict = jnp.where(ci < ri, 1.0, 0.0).astype(BF16)
    tot1 = jnp.sum(oh1, axis=0, keepdims=True)
    tot2 = jnp.sum(oh2, axis=0, keepdims=True)
    rank1 = jnp.sum(_dot(lstrict, oh1.astype(BF16)) * oh1, axis=-1, keepdims=True)
    rank2 = jnp.sum((_dot(lstrict, oh2.astype(BF16)) + tot1) * oh2, axis=-1, keepdims=True)
    ids = jnp.where(lane_i == 0, i1, jnp.where(lane_i == 1, i2, jnp.where(lane_i == 2, rank1,
                                                                          jnp.where(lane_i == 3, rank2, 0.0))))
    ids_o[...] = ids.astype(I32)
    cnt_o[...] = jnp.broadcast_to(tot1 + tot2, cnt_o.shape).astype(I32)


def _merge(alpha, ys, bonus, g, ym, u, x, lnx_w, lnx_b, b_gate, w_br, w_bm, w_out, ln1_w, ln1_b,
           w_rg, b_rg, w_re, b_re):
    n, d = x.shape
    tm = MERGE_TM
    wr = jnp.zeros((d, LANES), F32).at[:, :N_EXPERTS].set(w_re).at[:, C_ROUTE_G:C_ROUTE_G + N_GROUPS].set(w_rg)
    wr_hi = wr.astype(BF16)
    wr_lo = (wr - wr_hi.astype(F32)).astype(BF16)
    b_r = jnp.zeros((1, LANES), F32).at[0, :N_EXPERTS].set(b_re).at[0, C_ROUTE_G:C_ROUTE_G + N_GROUPS].set(b_rg)
    tile = lambda w_: pl.BlockSpec((tm, w_), lambda i: (i, 0))
    const = lambda shape: pl.BlockSpec(shape, lambda i: (0, 0))
    return pl.pallas_call(
        functools.partial(_merge_kernel, alpha),
        grid=(n // tm,),
        in_specs=[tile(R_WIDTH), tile(R_WIDTH), tile(R_WIDTH), tile(M_WIDTH),
                  pl.BlockSpec((tm, d), lambda i: (i, C_GR // d)),
                  pl.BlockSpec((tm, d), lambda i: (i, C_GM // d)),
                  tile(d),
                  const((1, R_WIDTH)), const((1, R_WIDTH)), const((1, d)), const((1, d)),
                  const((R_WIDTH, d)), const((M_WIDTH, d)), const((d, d)), const((1, d)), const((1, d)),
                  const((d, LANES)), const((d, LANES)), const((1, LANES))],
        out_specs=[tile(d), tile(LANES), tile(LANES), pl.BlockSpec((SUBLANES, LANES), lambda i: (i, 0))],
        out_shape=[jax.ShapeDtypeStruct((n, d), F32), jax.ShapeDtypeStruct((n, LANES), I32),
                   jax.ShapeDtypeStruct((n, LANES), F32),
                   jax.ShapeDtypeStruct((n // tm * SUBLANES, LANES), I32)],
        compiler_params=_cparams(("parallel",)),
        name="merge_ln1_router",
    )(ys, bonus, g, ym, u, u, x, lnx_w[None, :], lnx_b[None, :], b_gate[None, :d], b_gate[None, d:],
      w_br.astype(BF16), w_bm.astype(BF16), w_out.astype(BF16), ln1_w[None, :], ln1_b[None, :],
      wr_hi, wr_lo, b_r)


MOE_CH = 256
TOK_ROWS = 2048 // LANES
DISPATCH_TM = 256


ROW_DMA_UNROLL = 8


def _each(count, fn, unroll=1):
    def body(r, carry):
        fn(r)
        return carry
    lax.fori_loop(0, count, body, 0, unroll=unroll)


def _each_dyn(lo, hi, fn):
    def body(r, carry):
        fn(r)
        return carry
    lax.fori_loop(lo, hi, body, 0)


def _each_choice(fn):
    def both(r):
        for j in range(TOP_K):
            fn(r, j)
    return both


def _zero_unused_tail(pstart, nchunks, zeros_ref, out_hbm, sem):
    ch = zeros_ref.shape[0]
    used = (pstart[N_EXPERTS - 1] + nchunks[N_EXPERTS - 1] * ch) // ch
    total = out_hbm.shape[0] // ch

    def chunk(c):
        return pltpu.make_async_copy(zeros_ref, out_hbm.at[pl.ds(pl.multiple_of(c * ch, ch), ch)], sem)

    def over_tail(fn):
        def body(c, carry):
            fn(c)
            return carry
        lax.fori_loop(used, total, body, 0)

    over_tail(lambda c: chunk(c).start())
    over_tail(lambda c: chunk(c).wait())


def _moe_dispatch_kernel(dest, pstart, nchunks, x_ref, xs_hbm, zbuf, sem_z, sem):
    i = pl.program_id(0)
    tm = x_ref.shape[0]
    ch = zbuf.shape[0]

    @pl.when(i == 0)
    def _():
        zbuf[...] = jnp.zeros(zbuf.shape, F32)

        def tail(e):
            row0 = pl.multiple_of(pstart[e] + (nchunks[e] - 1) * ch, ch)
            return pltpu.make_async_copy(zbuf, xs_hbm.at[pl.ds(row0, ch)], sem_z)

        def start(e):
            @pl.when(nchunks[e] > 0)
            def _():
                tail(e).start()

        def wait(e):
            @pl.when(nchunks[e] > 0)
            def _():
                tail(e).wait()

        _each(N_EXPERTS, start)
        _each(N_EXPERTS, wait)
        _zero_unused_tail(pstart, nchunks, zbuf, xs_hbm, sem_z)

    def row(r, j):
        return pltpu.make_async_copy(x_ref.at[pl.ds(r, 1)],
                                     xs_hbm.at[pl.ds(dest[(i * tm + r) * TOP_K + j], 1)], sem)

    _each(tm, _each_choice(lambda r, j: row(r, j).start(priority=j)), ROW_DMA_UNROLL)
    _each(tm, _each_choice(lambda r, j: row(r, j).wait()), ROW_DMA_UNROLL)


def _moe_dispatch(x1, dest, pstart, nchunks, rows_pad):
    n, d = x1.shape
    tm = DISPATCH_TM
    return pl.pallas_call(
        _moe_dispatch_kernel,
        grid_spec=pltpu.PrefetchScalarGridSpec(
            num_scalar_prefetch=3,
            grid=(n // tm,),
            in_specs=[pl.BlockSpec((tm, d), lambda i, *_: (i, 0))],
            out_specs=pl.BlockSpec(memory_space=pl.ANY),
            scratch_shapes=[pltpu.VMEM((MOE_CH, d), F32), pltpu.SemaphoreType.DMA, pltpu.SemaphoreType.DMA]),
        out_shape=jax.ShapeDtypeStruct((rows_pad, d), F32),
        compiler_params=_cparams(("arbitrary",)),
        name="moe_dispatch",
    )(dest, pstart, nchunks, x1)


def _moe_expert_kernel(pstart, nchunks, xs_hbm, wg_ref, wu_ref, wd_ref, ys_hbm,
                       xbuf, ybuf, wgb, wub, wdb, sem_in, sem_out):
    e = pl.program_id(0)
    ch = xbuf.shape[1]
    nc = nchunks[e]
    g0 = pstart[e] // ch
    total = (pstart[N_EXPERTS - 1] + nchunks[N_EXPERTS - 1] * ch) // ch

    def rows(g):
        return pl.ds(pl.multiple_of(g * ch, ch), ch)

    def load(g):
        return pltpu.make_async_copy(xs_hbm.at[rows(g)], xbuf.at[g % 2], sem_in.at[g % 2])

    def store(g, sub):
        blocks = pl.ds(pl.multiple_of(g * (ch // SUBLANES), ch // SUBLANES), ch // SUBLANES)
        return pltpu.make_async_copy(ybuf.at[g % 2, :, :, sub, :], ys_hbm.at[blocks, sub], sem_out.at[g % 2])

    def store_start(g):
        for sub in range(SUBLANES):
            store(g, sub).start()

    def store_wait(g):
        for sub in range(SUBLANES):
            store(g, sub).wait()

    @pl.when(nc > 0)
    def _():
        @pl.when(g0 == 0)
        def _():
            load(0).start()

        wgb[...] = wg_ref[0].astype(BF16)
        wub[...] = wu_ref[0].astype(BF16)
        wdb[...] = wd_ref[0].astype(BF16)

        def chunk(c, carry):
            g = g0 + c
            load(g).wait()

            @pl.when(g + 1 < total)
            def _():
                load(g + 1).start()

            @pl.when(g >= 2)
            def _():
                store_wait(g - 2)

            xb = xbuf[g % 2].astype(BF16)
            gate = _dot(xb, wgb[...])
            hb = gate * _sigmoid(gate) * _dot(xb, wub[...])
            y = _dot(hb.astype(BF16), wdb[...])
            for c in range(TOK_ROWS):
                ybuf[g % 2, :, c, :, :] = y[:, c * LANES:(c + 1) * LANES].reshape(ch // SUBLANES, SUBLANES, LANES)
            store_start(g)
            return carry

        lax.fori_loop(0, nc, chunk, 0)

    @pl.when(e == pl.num_programs(0) - 1)
    def _():
        @pl.when(total >= 2)
        def _():
            store_wait(total - 2)

        @pl.when(total >= 1)
        def _():
            store_wait(total - 1)

        ybuf[...] = jnp.zeros(ybuf.shape, F32)
        all_chunks = ys_hbm.shape[0] * SUBLANES // ch
        _each_dyn(total, all_chunks, store_start)
        _each_dyn(total, all_chunks, store_wait)


def _moe_experts(xs, pstart, nchunks, w_gate, w_up, w_down):
    rows_pad, d = xs.shape
    de = w_gate.shape[-1]
    wspec = lambda shape: pl.BlockSpec(shape, lambda e, *_: (e, 0, 0))
    return pl.pallas_call(
        _moe_expert_kernel,
        grid_spec=pltpu.PrefetchScalarGridSpec(
            num_scalar_prefetch=2,
            grid=(N_EXPERTS,),
            in_specs=[pl.BlockSpec(memory_space=pl.ANY),
                      wspec((1, d, de)), wspec((1, d, de)), wspec((1, de, d))],
            out_specs=pl.BlockSpec(memory_space=pl.ANY),
            scratch_shapes=[pltpu.VMEM((2, MOE_CH, d), F32),
                            pltpu.VMEM((2, MOE_CH // SUBLANES, TOK_ROWS, SUBLANES, LANES), F32),
                            pltpu.VMEM((d, de), BF16), pltpu.VMEM((d, de), BF16), pltpu.VMEM((de, d), BF16),
                            pltpu.SemaphoreType.DMA((2,)), pltpu.SemaphoreType.DMA((2,))]),
        out_shape=jax.ShapeDtypeStruct((rows_pad // SUBLANES, SUBLANES, TOK_ROWS, LANES), F32),
        compiler_params=_cparams(("arbitrary",)),
        name="moe_experts",
    )(pstart, nchunks, xs, w_gate, w_up, w_down)


def _moe_plan(ids, cnt, tm):
    n = ids.shape[0]
    tile_cnt = cnt[::SUBLANES, :N_EXPERTS]
    counts = jnp.sum(tile_cnt, axis=0)
    padded = ((counts + MOE_CH - 1) // MOE_CH) * MOE_CH
    pstart = jnp.cumsum(padded) - padded
    tile_base = pstart[None, :] + jnp.cumsum(tile_cnt, axis=0) - tile_cnt
    eid = ids[:, 0:TOP_K]
    rank = ids[:, TOP_K:2 * TOP_K]
    base_of_tok = jnp.repeat(tile_base, tm, axis=0)[:, None, :]
    chosen = eid[:, :, None] == jnp.arange(N_EXPERTS, dtype=I32)[None, None, :]
    dest = jnp.sum(jnp.where(chosen, base_of_tok, 0), axis=-1) + rank
    return dest.reshape(-1).astype(I32), pstart.astype(I32), (padded // MOE_CH).astype(I32)


FINAL_TM = 256


def _final_kernel(alpha, dest, x1_ref, wts_ref, p_ref, wpg_ref, wple_ref, l2w_ref, l2b_ref, ys_hbm, o_ref,
                  ybuf, sem):
    i = pl.program_id(0)
    tm = x1_ref.shape[0]
    slot = i % 2

    def row(tile, buf, r, j):
        src = dest[(tile * tm + r) * TOP_K + j]
        return pltpu.make_async_copy(ys_hbm.at[src // SUBLANES, src % SUBLANES],
                                     ybuf.at[buf, j, r // SUBLANES, :, r % SUBLANES, :], sem.at[buf])

    def gather(tile, buf):
        _each(tm, _each_choice(lambda r, j: row(tile, buf, r, j).start(priority=j)), ROW_DMA_UNROLL)

    @pl.when(i == 0)
    def _():
        gather(0, 0)

    @pl.when(i + 1 < pl.num_programs(0))
    def _():
        gather(i + 1, 1 - slot)

    x1 = x1_ref[...]
    ple = _sigmoid(_dot(x1.astype(BF16), wpg_ref[...])) * _dot(p_ref[...], wple_ref[...])
    _each(tm, _each_choice(lambda r, j: row(i, slot, r, j).wait()), ROW_DMA_UNROLL)
    y_tok = [jnp.concatenate([ybuf[slot, j, :, c, :, :].reshape(tm, LANES) for c in range(TOK_ROWS)], axis=1)
             for j in range(TOP_K)]
    moe = y_tok[0] * wts_ref[:, 0:1] + y_tok[1] * wts_ref[:, 1:2]
    o_ref[...] = _layer_norm(alpha * x1 + moe + ple, l2w_ref[...], l2b_ref[...])


def _final(alpha, dest, x1, ys, wts, p_bf, w_pg, w_ple, ln2_w, ln2_b):
    n, d = x1.shape
    tm = FINAL_TM
    tile = lambda w_: pl.BlockSpec((tm, w_), lambda i, *_: (i, 0))
    const = lambda shape: pl.BlockSpec(shape, lambda i, *_: (0, 0))
    return pl.pallas_call(
        functools.partial(_final_kernel, alpha),
        grid_spec=pltpu.PrefetchScalarGridSpec(
            num_scalar_prefetch=1,
            grid=(n // tm,),
            in_specs=[tile(d), tile(LANES), tile(p_bf.shape[1]),
                      const((d, d)), const((p_bf.shape[1], d)), const((1, d)), const((1, d)),
                      pl.BlockSpec(memory_space=pl.ANY)],
            out_specs=tile(d),
            scratch_shapes=[pltpu.VMEM((2, TOP_K, tm // SUBLANES, TOK_ROWS, SUBLANES, LANES), F32),
                            pltpu.SemaphoreType.DMA((2,))]),
        out_shape=jax.ShapeDtypeStruct((n, d), F32),
        compiler_params=_cparams(("arbitrary",)),
        name="final_ln2",
    )(dest, x1, wts, p_bf, w_pg.astype(BF16), w_ple.astype(BF16), ln2_w[None, :], ln2_b[None, :], ys)


def _regroup_w_in(w):
    m0 = RWKV_COLS
    g0 = RWKV_COLS + MLSTM_COLS
    mqk = 2 * M_HEADS * M_QK
    pad = lambda c: jnp.zeros((w.shape[0], c), w.dtype)
    parts = [w[:, 0:3 * R_WIDTH],
             w[:, m0:m0 + mqk],
             w[:, m0 + mqk:m0 + mqk + M_WIDTH],
             w[:, m0 + mqk + M_WIDTH + 2 * M_HEADS:m0 + MLSTM_COLS],
             w[:, g0:],
             w[:, 3 * R_WIDTH:RWKV_COLS], pad(LORA_PAD - LORA_COLS),
             w[:, m0 + mqk + M_WIDTH:m0 + mqk + M_WIDTH + 2 * M_HEADS], pad(LANES - 2 * M_HEADS)]
    out = jnp.concatenate(parts, axis=1)
    assert out.shape[1] == C_TOTAL
    return out


def kernel(x, p, w_in, mu_shift, w0, w_w2, a0, w_a2, w_g2, k_k, k_a, r_k, lnx_w, lnx_b, conv_w, conv_b,
           i_bias, f_bias, mh_w, b_gate, w_br, w_bm, w_out, ln1_w, ln1_b, w_rg, b_rg, w_re, b_re,
           w_gate, w_up, w_down, w_pg, w_ple, ln2_w, ln2_b):
    bsz, seq, d = x.shape
    depth = w_in.shape[0]
    assert bsz * R_HEADS * 2 == LANES and seq % PROJ_TM == 0 and seq % CHUNK == 0
    alpha = (2 * depth) ** 0.25
    n = bsz * seq
    xf = x.reshape(n, d)
    for i in range(depth):
        u = _proj_in(xf.astype(BF16), _regroup_w_in(w_in[i]).astype(BF16))
        r, dec, kp, v, kn, beta, g, bonus = _rwkv_prep(
            u, seq, mu_shift[i], w0[i], w_w2[i], a0[i], w_a2[i], w_g2[i], k_k[i], k_a[i], r_k[i])
        ys = _rwkv_scan(kn, dec, kp, beta, r, v, bsz, seq)
        ym = _mlstm(u, bsz, seq, conv_w[i], conv_b[i], i_bias[i], f_bias[i], mh_w[i])
        x1, ids, wts, cnt = _merge(alpha, ys, bonus, g, ym, u, xf, lnx_w[i], lnx_b[i], b_gate[i], w_br[i],
                                   w_bm[i], w_out[i], ln1_w[i], ln1_b[i], w_rg[i], b_rg[i], w_re[i], b_re[i])
        dest, pstart, nchunks = _moe_plan(ids, cnt, MERGE_TM)
        rows_pad = TOP_K * n + N_EXPERTS * MOE_CH
        xs = _moe_dispatch(x1, dest, pstart, nchunks, rows_pad)
        ys_sorted = _moe_experts(xs, pstart, nchunks, w_gate[i], w_up[i], w_down[i])
        xf = _final(alpha, dest, x1, ys_sorted, wts, p[i].reshape(n, -1).astype(BF16), w_pg[i], w_ple[i],
                    ln2_w[i], ln2_b[i])
    return xf.reshape(bsz, seq, d)
```

```python
import functools

import jax
import jax.numpy as jnp
from jax import lax
from jax.experimental import pallas as pl
from jax.experimental.pallas import tpu as pltpu

F32 = jnp.float32
BF16 = jnp.bfloat16
I32 = jnp.int32

R_HEADS, R_HEAD = 16, 64
R_WIDTH = R_HEADS * R_HEAD
DECAY_LORA, AAA_LORA, GATE_LORA = 64, 64, 160
LORA_COLS = DECAY_LORA + AAA_LORA + GATE_LORA
LORA_PAD = 512
R_GN_EPS = 64e-5
RWKV_COLS = 3 * R_WIDTH + LORA_COLS
M_HEADS, M_QK, M_V = 8, 64, 128
M_WIDTH = M_HEADS * M_V
CONV_K = 4
CHUNK = 128
M_NORM_EPS = 1e-6
MLSTM_COLS = 2 * M_HEADS * M_QK + 2 * M_WIDTH + 2 * M_HEADS
N_GROUPS, EXPERTS_PER_GROUP = 4, 8
N_EXPERTS = N_GROUPS * EXPERTS_PER_GROUP
TOP_K = 2
MOE_BLOCK = 128
LN_EPS = 1e-5

LANES = 128
SUBLANES = 8
MXU_DIM = 256
V7X_VMEM_BYTES = 64 * 1024 * 1024
VMEM_LIMIT = 56 * 1024 * 1024
SCAN_VMEM_LIMIT = 60 * 1024 * 1024

C_RKV = 0
C_MQK = 3072
C_MV = 4096
C_MO = 5120
C_GR = 6144
C_GM = 8192
C_LORA = 10240
C_MG = 10752
C_TOTAL = 10880
PROJ_TN = 2176
PROJ_TM = 512


def _cparams(sem, vmem=VMEM_LIMIT):
    return pltpu.CompilerParams(dimension_semantics=sem, vmem_limit_bytes=vmem)


def _sigmoid(x):
    return 1.0 / (1.0 + jnp.exp(-x))


def _softplus(x):
    return jnp.maximum(x, 0.0) + jnp.log1p(jnp.exp(-jnp.abs(x)))


def _split3(x):
    hi = x.astype(BF16)
    r1 = x - hi.astype(F32)
    mid = r1.astype(BF16)
    lo = (r1 - mid.astype(F32)).astype(BF16)
    return hi, mid, lo


def _dot(a, b):
    return jnp.dot(a, b, preferred_element_type=F32)


def _dot_nt(a, b):
    return lax.dot_general(a, b, (((1,), (1,)), ((), ())), preferred_element_type=F32)


def _dot_tn(a, b):
    return lax.dot_general(a, b, (((0,), (0,)), ((), ())), preferred_element_type=F32)


def _dot_exact_rhs(x, ones_bf16, terms=3):
    parts = _split3(x)[:terms]
    acc = _dot(parts[0], ones_bf16)
    for part in parts[1:]:
        acc = acc + _dot(part, ones_bf16)
    return acc


def _block_ones(n, group):
    r = lax.broadcasted_iota(I32, (n, n), 0) // group
    c = lax.broadcasted_iota(I32, (n, n), 1) // group
    return jnp.where(r == c, 1.0, 0.0).astype(BF16)


def _seg_sum(x, group):
    ones = _block_ones(MXU_DIM, group)
    slabs = [_dot_exact_rhs(x[:, p * MXU_DIM:(p + 1) * MXU_DIM], ones, terms=2)
             for p in range(x.shape[1] // MXU_DIM)]
    return jnp.concatenate(slabs, axis=1)


def _layer_norm(x, w, b):
    mu = jnp.mean(x, axis=-1, keepdims=True)
    xc = x - mu
    var = jnp.mean(xc * xc, axis=-1, keepdims=True)
    return xc * lax.rsqrt(var + LN_EPS) * w + b


def _proj_kernel(x_ref, w_ref, o_ref):
    o_ref[...] = _dot(x_ref[...], w_ref[...])


def _proj_in(x_bf, w_bf):
    m, k = x_bf.shape
    n = w_bf.shape[1]
    return pl.pallas_call(
        _proj_kernel,
        grid=(n // PROJ_TN, m // PROJ_TM),
        in_specs=[pl.BlockSpec((PROJ_TM, k), lambda j, i: (i, 0)),
                  pl.BlockSpec((k, PROJ_TN), lambda j, i: (0, j))],
        out_specs=pl.BlockSpec((PROJ_TM, PROJ_TN), lambda j, i: (i, j)),
        out_shape=jax.ShapeDtypeStruct((m, n), F32),
        compiler_params=_cparams(("parallel", "parallel")),
        name="proj_in",
    )(x_bf, w_bf)


PREP_TM = 256


def _rwkv_prep_kernel(seq, u_ref, l_ref, up_ref, lp_ref, mu_ref, mul_ref, w0_ref, a0_ref, kk_ref, ka_ref,
                      rk_ref, ww_ref, wa_ref, wg_ref,
                      r_o, dec_o, kp_o, v_o, kn_o, beta_o, g_o, bonus_o):
    i = pl.program_id(0)
    tm = u_ref.shape[0]
    first = (i * tm) % seq == 0
    row = lax.broadcasted_iota(I32, (tm, 1), 0)

    def shift(u, prev8):
        prev_row = jnp.where(first, 0.0, prev8[SUBLANES - 1:SUBLANES, :])
        return jnp.where(row == 0, prev_row, pltpu.roll(u, 1, 0))

    u = u_ref[...]
    z = u + mu_ref[...] * (shift(u, up_ref[...]) - u)
    lo = l_ref[...]
    zl = lo + mul_ref[...] * (shift(lo, lp_ref[...]) - lo)

    r = z[:, 0:R_WIDTH]
    k = z[:, R_WIDTH:2 * R_WIDTH]
    v = z[:, 2 * R_WIDTH:3 * R_WIDTH]
    w_pre = w0_ref[...] + _dot(jnp.tanh(zl).astype(BF16), ww_ref[...])
    w = -_softplus(-w_pre) - 0.5
    dec = jnp.exp(-jnp.exp(w))
    a = _sigmoid(a0_ref[...] + _dot(zl.astype(BF16), wa_ref[...]))
    g = _dot(_sigmoid(zl).astype(BF16), wg_ref[...])

    kk = k * kk_ref[...]
    nrm = jnp.sqrt(_seg_sum(kk * kk, R_HEAD))
    kn = kk / jnp.maximum(nrm, 1e-12)
    kp = k * (1.0 + (a - 1.0) * ka_ref[...])
    bonus = _seg_sum(r * kp * rk_ref[...], R_HEAD) * v

    r_o[...] = r
    dec_o[...] = dec
    kp_o[...] = kp
    v_o[...] = v
    kn_o[...] = kn
    beta_o[...] = kn * a
    g_o[...] = g
    bonus_o[...] = bonus


def _rwkv_prep(u, seq, mu, w0, w_w2, a0, w_a2, w_g2, k_k, k_a, r_k):
    n = u.shape[0]
    tm = PREP_TM
    mu_rkv = mu[None, :3 * R_WIDTH]
    mu_lora = jnp.zeros((1, LORA_PAD), F32).at[0, :LORA_COLS].set(mu[3 * R_WIDTH:])
    ww = jnp.zeros((LORA_PAD, R_WIDTH), BF16).at[0:DECAY_LORA].set(w_w2.astype(BF16))
    wa = jnp.zeros((LORA_PAD, R_WIDTH), BF16).at[DECAY_LORA:DECAY_LORA + AAA_LORA].set(w_a2.astype(BF16))
    wg = jnp.zeros((LORA_PAD, R_WIDTH), BF16).at[DECAY_LORA + AAA_LORA:LORA_COLS].set(w_g2.astype(BF16))
    row = lambda t: t.reshape(1, R_WIDTH)
    prev_blk = lambda i: jnp.maximum(i * (tm // SUBLANES) - 1, 0)
    const = lambda shape: pl.BlockSpec(shape, lambda i: (0, 0))
    nat = jax.ShapeDtypeStruct((n, R_WIDTH), F32)
    return pl.pallas_call(
        functools.partial(_rwkv_prep_kernel, seq),
        grid=(n // tm,),
        in_specs=[pl.BlockSpec((tm, 3 * R_WIDTH), lambda i: (i, C_RKV // (3 * R_WIDTH))),
                  pl.BlockSpec((tm, LORA_PAD), lambda i: (i, C_LORA // LORA_PAD)),
                  pl.BlockSpec((SUBLANES, 3 * R_WIDTH), lambda i: (prev_blk(i), C_RKV // (3 * R_WIDTH))),
                  pl.BlockSpec((SUBLANES, LORA_PAD), lambda i: (prev_blk(i), C_LORA // LORA_PAD)),
                  const((1, 3 * R_WIDTH)), const((1, LORA_PAD)),
                  const((1, R_WIDTH)), const((1, R_WIDTH)), const((1, R_WIDTH)), const((1, R_WIDTH)),
                  const((1, R_WIDTH)),
                  const((LORA_PAD, R_WIDTH)), const((LORA_PAD, R_WIDTH)), const((LORA_PAD, R_WIDTH))],
        out_specs=[pl.BlockSpec((tm, R_WIDTH), lambda i: (i, 0))] * 8,
        out_shape=[nat] * 8,
        compiler_params=_cparams(("parallel",)),
        name="rwkv_prep",
    )(u, u, u, u, mu_rkv, mu_lora, row(w0), row(a0), row(k_k), row(k_a), row(r_k), ww, wa, wg)


SCAN_TB = LANES // 2
SCAN_OUT = 2
SCAN_V = R_HEAD // 2
SCAN_ACC = 4
SCAN_KOPS = 5
SCAN_PAIRS = ((0, 1), (2, 3), (4, 5))
SCAN_HP = R_WIDTH // LANES
SCAN_STEPS1 = 4
SCAN_STEPS2 = 3
SCAN_KEYS_PER_TRIP = 4
SCAN_TRIPS2 = R_HEAD // SCAN_KEYS_PER_TRIP
Z_PITCH = R_HEAD + SUBLANES
V_PITCH = SCAN_V + SUBLANES
RELAYOUT_UNROLL = 8


def _rwkv_scan_kernel(kn_hbm, dec_hbm, kp_hbm, beta_hbm, r_hbm, v_hbm, y_ref,
                      s_ref, stage, z_ref, xs_ref, vs_ref, ys_ref, sem):
    i = pl.program_id(0)
    nblk = pl.num_programs(0)
    nb = stage.shape[2]
    tb = SCAN_TB
    slab = R_HEADS * Z_PITCH
    zrows = nb * slab
    cur = i % 2
    srcs = (kn_hbm, dec_hbm, kp_hbm, beta_hbm, r_hbm, v_hbm)

    def fetch(op, blk, par):
        return pltpu.make_async_copy(srcs[op].at[:, pl.ds(blk * tb, tb), :], stage.at[par, op], sem.at[par])

    def fetch_start(blk, par):
        for op in range(len(srcs)):
            fetch(op, blk, par).start()

    def fetch_wait(blk, par):
        for op in range(len(srcs)):
            fetch(op, blk, par).wait()

    def to_time_on_lanes(par, b):
        for p, (oa, ob) in enumerate(SCAN_PAIRS):
            for hp in range(SCAN_HP):
                cols = slice(hp * LANES, (hp + 1) * LANES)
                tt = jnp.concatenate([stage[par, oa, b, :, cols], stage[par, ob, b, :, cols]], axis=0).T
                for hh in range(2):
                    row0 = pl.multiple_of(p * zrows + (b * R_HEADS + 2 * hp + hh) * Z_PITCH, SUBLANES)
                    z_ref[pl.ds(row0, R_HEAD), :] = tt[hh * R_HEAD:(hh + 1) * R_HEAD]

    def head_rows(p, b, c):
        return z_ref[pl.ds(p * zrows + b * slab + c, R_HEADS, stride=Z_PITCH), :]

    def key_unit(buf, p, k):
        oa, ob = SCAN_PAIRS[p]
        rows = [head_rows(p, b, k) for b in range(nb)]
        tt = jnp.concatenate(rows + rows, axis=0).T
        xs_ref[buf, oa, k] = tt[0:tb]
        if ob < SCAN_KOPS:
            xs_ref[buf, ob, k] = tt[tb:2 * tb]

    def value_unit(buf, vp):
        rows = [head_rows(len(SCAN_PAIRS) - 1, b, vh * SCAN_V + vp) for vh in range(2) for b in range(nb)]
        tt = jnp.concatenate(rows, axis=0).T
        vs_ref[pl.ds(buf * tb * V_PITCH + vp, tb, stride=V_PITCH), :] = tt[tb:2 * tb]

    def chain_units(buf, q):
        units = [functools.partial(key_unit, buf, p, SCAN_KEYS_PER_TRIP * q + kk)
                 for kk in range(SCAN_KEYS_PER_TRIP) for p in range(len(SCAN_PAIRS))]
        units += [functools.partial(value_unit, buf, (SCAN_KEYS_PER_TRIP // 2) * q + vv)
                  for vv in range(SCAN_KEYS_PER_TRIP // 2)]
        return units

    @pl.when(i == 0)
    def _():
        s_ref[...] = jnp.zeros(s_ref.shape, F32)
        fetch_start(0, 0)
        fetch_wait(0, 0)

        @pl.when(nblk > 1)
        def _():
            fetch_start(1, 1)

        _each(nb, lambda b: to_time_on_lanes(0, b))
        _each(SCAN_TRIPS2, lambda q: [unit() for unit in chain_units(0, q)])

    @pl.when(i + 1 < nblk)
    def _():
        fetch_wait(i + 1, 1 - cur)

    @pl.when(i + 2 < nblk)
    def _():
        fetch_start(i + 2, cur)

    def total(acc):
        return (acc[0] + acc[1]) + (acc[2] + acc[3])

    def add_term(acc, k, term):
        acc[k % SCAN_ACC] = term if acc[k % SCAN_ACC] is None else acc[k % SCAN_ACC] + term

    def key_row(op, k, t):
        return xs_ref[cur, op, k, pl.ds(t, 1), :]

    acc = [None] * SCAN_ACC
    for k in range(R_HEAD):
        add_term(acc, k, s_ref[k] * xs_ref[cur, 0, k, 0:1, :])

    def step(t, s_kk):
        v_t = vs_ref[pl.ds(pl.multiple_of((cur * tb + t) * V_PITCH, SUBLANES), SCAN_V), :]
        t_next = jnp.minimum(t + 1, tb - 1)
        acc_y = [None] * SCAN_ACC
        acc_s = [None] * SCAN_ACC
        for k in range(R_HEAD):
            s_new = (s_ref[k] * key_row(1, k, t)
                     + (v_t * key_row(2, k, t) - s_kk * key_row(3, k, t)))
            s_ref[k] = s_new
            add_term(acc_y, k, s_new * key_row(4, k, t))
            add_term(acc_s, k, s_new * key_row(0, k, t_next))
        out_row = pl.multiple_of(((i % SCAN_OUT) * tb + t) * V_PITCH, SUBLANES)
        ys_ref[pl.ds(out_row, SCAN_V), :] = total(acc_y)
        return total(acc_s)

    def first_part(b, s_kk):
        for j in range(SCAN_STEPS1):
            s_kk = step(b * SCAN_STEPS1 + j, s_kk)
        to_time_on_lanes(1 - cur, b)
        return s_kk

    def second_part(q, s_kk):
        for j in range(SCAN_STEPS2):
            s_kk = step(nb * SCAN_STEPS1 + q * SCAN_STEPS2 + j, s_kk)
        for unit in chain_units(1 - cur, q):
            unit()
        return s_kk

    s_kk = lax.fori_loop(0, nb, first_part, total(acc))
    lax.fori_loop(0, SCAN_TRIPS2, second_part, s_kk)

    @pl.when(i % SCAN_OUT == SCAN_OUT - 1)
    def _():
        steps = SCAN_OUT * tb

        def out_rows(vp, carry):
            tt = ys_ref[pl.ds(vp, steps, stride=V_PITCH), :].T
            for vh in range(2):
                for b in range(nb):
                    g = vh * nb + b
                    z_ref[pl.ds(b * slab + vh * SCAN_V + vp, R_HEADS, stride=Z_PITCH), :] = \
                        tt[g * R_HEADS:(g + 1) * R_HEADS]
            return carry
        lax.fori_loop(0, SCAN_V, out_rows, 0, unroll=RELAYOUT_UNROLL)
        for b in range(nb):
            for hp in range(SCAN_HP):
                pair = [z_ref[pl.ds((b * R_HEADS + 2 * hp + hh) * Z_PITCH, R_HEAD), :] for hh in range(2)]
                y_ref[b, :, hp * LANES:(hp + 1) * LANES] = jnp.concatenate(pair, axis=0).T


def _rwkv_scan(kn, dec, kp, beta, r, v, bsz, seq):
    shape3 = (bsz, seq, R_WIDTH)
    ops = [a.reshape(shape3) for a in (kn, dec, kp, beta, r, v)]
    assert SCAN_TB == bsz * SCAN_STEPS1 + SCAN_TRIPS2 * SCAN_STEPS2 and seq % (SCAN_OUT * SCAN_TB) == 0
    y = pl.pallas_call(
        _rwkv_scan_kernel,
        grid=(seq // SCAN_TB,),
        in_specs=[pl.BlockSpec(memory_space=pl.ANY)] * len(ops),
        out_specs=pl.BlockSpec((bsz, SCAN_OUT * SCAN_TB, R_WIDTH), lambda i: (0, i // SCAN_OUT, 0)),
        out_shape=jax.ShapeDtypeStruct(shape3, F32),
        scratch_shapes=[pltpu.VMEM((R_HEAD, SCAN_V, LANES), F32),
                        pltpu.VMEM((2, len(ops), bsz, SCAN_TB, R_WIDTH), F32),
                        pltpu.VMEM((len(SCAN_PAIRS) * bsz * R_HEADS * Z_PITCH, LANES), F32),
                        pltpu.VMEM((2, SCAN_KOPS, R_HEAD, SCAN_TB, LANES), F32),
                        pltpu.VMEM((2 * SCAN_TB * V_PITCH, LANES), F32),
                        pltpu.VMEM((SCAN_OUT * SCAN_TB * V_PITCH, LANES), F32),
                        pltpu.SemaphoreType.DMA((2,))],
        compiler_params=_cparams(("arbitrary",), vmem=SCAN_VMEM_LIMIT),
        name="rwkv_scan",
    )(*ops)
    return y.reshape(bsz * seq, R_WIDTH)


M_AUG = 2 * M_V


def _mlstm_kernel(qk_ref, v_ref, o_ref, g_ref, cw_ref, cb_ref, gb_ref, mh_ref, y_ref,
                  ext_ref, c_ref, m_ref):
    c_idx = pl.program_id(0)
    L = CHUNK
    nb = qk_ref.shape[0]

    @pl.when(c_idx == 0)
    def _():
        ext_ref[:, 0:SUBLANES, :] = jnp.zeros((nb, SUBLANES, ext_ref.shape[2]), F32)
        c_ref[...] = jnp.zeros(c_ref.shape, F32)
        m_ref[...] = jnp.full(m_ref.shape, -jnp.inf, F32)

    lane = lax.broadcasted_iota(I32, (L, LANES), 1)
    ti = lax.broadcasted_iota(I32, (L, L), 0)
    si = lax.broadcasted_iota(I32, (L, L), 1)
    causal = si <= ti
    ltri = jnp.where(causal, 1.0, 0.0).astype(BF16)
    utri = jnp.where(ti <= si, 1.0, 0.0).astype(BF16)
    ones_col = jnp.where(lane == 0, 1.0, 0.0).astype(BF16)

    for b in range(nb):
        ext_ref[b, SUBLANES:SUBLANES + L, :] = qk_ref[b]
        conv = cb_ref[...]
        for j in range(CONV_K):
            off = SUBLANES - (CONV_K - 1) + j
            conv = conv + cw_ref[j:j + 1, :] * ext_ref[b, off:off + L, :]
        ext_ref[b, 0:SUBLANES, :] = ext_ref[b, L:L + SUBLANES, :]
        qk = conv * _sigmoid(conv)
        q_all = qk[:, :M_HEADS * M_QK].astype(BF16)
        k_all = qk[:, M_HEADS * M_QK:] * (M_QK ** -0.5)

        gpre = g_ref[b] + gb_ref[...]
        gcols = jnp.where(lane < M_HEADS, gpre, -_softplus(-gpre))
        grows = gcols.T
        acols = _dot_exact_rhs_left(ltri, gcols)
        arows = _dot_exact_rhs(grows, utri)

        heads = range(M_HEADS)
        a_col = [acols[:, M_HEADS + h:M_HEADS + h + 1] for h in heads]
        i_col = [gcols[:, h:h + 1] for h in heads]
        m_st = [m_ref[b * M_HEADS + h:b * M_HEADS + h + 1, 0:1] for h in heads]
        d = [jnp.where(causal, a_col[h] - arows[M_HEADS + h:M_HEADS + h + 1, :] + grows[h:h + 1, :], -jnp.inf)
             for h in heads]
        dmax = [jnp.max(d[h], axis=-1, keepdims=True) for h in heads]
        inter = [a_col[h] + m_st[h] for h in heads]
        m_t = [jnp.maximum(inter[h], dmax[h]) for h in heads]
        q = [q_all[:, h * M_QK:(h + 1) * M_QK] for h in heads]
        k = [k_all[:, h * M_QK:(h + 1) * M_QK] for h in heads]
        v_aug = [jnp.concatenate([v_ref[b, :, h * M_V:(h + 1) * M_V].astype(BF16), ones_col], axis=1)
                 for h in heads]
        s = [_dot_nt(q[h], k[h].astype(BF16)) * jnp.exp(d[h] - m_t[h]) for h in heads]
        ie = [jnp.exp(inter[h] - m_t[h]) for h in heads]
        tot = [ie[h] * _dot(q[h], c_ref[b * M_HEADS + h].astype(BF16)) + _dot(s[h].astype(BF16), v_aug[h])
               for h in heads]
        h_c = [tot[h][:, :M_V] / jnp.maximum(jnp.abs(tot[h][:, M_V:M_V + 1]), jnp.exp(-m_t[h])) for h in heads]

        a_tot = [a_col[h][L - 1:L, :] for h in heads]
        gl = [a_tot[h] - a_col[h] + i_col[h] for h in heads]
        m_new = [jnp.maximum(a_tot[h] + m_st[h], jnp.max(gl[h], axis=0, keepdims=True)) for h in heads]
        kg = [(k[h] * jnp.exp(gl[h] - m_new[h])).astype(BF16) for h in heads]
        for h in heads:
            bh = b * M_HEADS + h
            c_ref[bh] = jnp.exp(a_tot[h] + m_st[h] - m_new[h]) * c_ref[bh] + _dot_tn(kg[h], v_aug[h])
            m_ref[bh:bh + 1, :] = jnp.broadcast_to(m_new[h], (1, LANES))

        mu = [jnp.mean(h_c[h], axis=-1, keepdims=True) for h in heads]
        hc = [h_c[h] - mu[h] for h in heads]
        var = [jnp.mean(hc[h] * hc[h], axis=-1, keepdims=True) for h in heads]
        for h in heads:
            hn = hc[h] * lax.rsqrt(var[h] + M_NORM_EPS) * mh_ref[:, h * M_V:(h + 1) * M_V]
            y_ref[b, :, h * M_V:(h + 1) * M_V] = _sigmoid(o_ref[b, :, h * M_V:(h + 1) * M_V]) * hn


def _dot_exact_rhs_left(ones_bf16, x):
    hi, mid, lo = _split3(x)
    return _dot(ones_bf16, hi) + _dot(ones_bf16, mid) + _dot(ones_bf16, lo)


def _mlstm(u, bsz, seq, conv_w, conv_b, i_bias, f_bias, mh_w):
    nc = seq // CHUNK
    w = M_WIDTH
    gbias = jnp.zeros((1, LANES), F32).at[0, :M_HEADS].set(i_bias).at[0, M_HEADS:2 * M_HEADS].set(f_bias)
    u3 = u.reshape(bsz, seq, u.shape[1])
    blk = lambda col: pl.BlockSpec((bsz, CHUNK, w), lambda c: (0, c, col // w))
    const = lambda shape: pl.BlockSpec(shape, lambda c: (0, 0))
    y = pl.pallas_call(
        _mlstm_kernel,
        grid=(nc,),
        in_specs=[blk(C_MQK), blk(C_MV), blk(C_MO),
                  pl.BlockSpec((bsz, CHUNK, LANES), lambda c: (0, c, C_MG // LANES)),
                  const((CONV_K, w)), const((1, w)), const((1, LANES)), const((1, w))],
        out_specs=pl.BlockSpec((bsz, CHUNK, w), lambda c: (0, c, 0)),
        out_shape=jax.ShapeDtypeStruct((bsz, seq, w), F32),
        scratch_shapes=[pltpu.VMEM((bsz, CHUNK + SUBLANES, w), F32),
                        pltpu.VMEM((bsz * M_HEADS, M_QK, M_AUG), F32),
                        pltpu.VMEM((bsz * M_HEADS, LANES), F32)],
        compiler_params=_cparams(("arbitrary",)),
        name="mlstm",
    )(u3, u3, u3, u3, conv_w, conv_b[None, :], gbias, mh_w[None, :])
    return y.reshape(bsz * seq, w)


MERGE_TM = 256
C_ROUTE_G = N_EXPERTS


def _merge_kernel(alpha, ys_ref, bonus_ref, g_ref, ym_ref, gr_ref, gm_ref, x_ref,
                  lnxw_ref, lnxb_ref, bgr_ref, bgm_ref, wbr_ref, wbm_ref, wout_ref, l1w_ref, l1b_ref,
                  wrh_ref, wrl_ref, br_ref, x1_o, ids_o, wts_o, cnt_o):
    ys = ys_ref[...]
    mu = _seg_sum(ys, R_HEAD) * (1.0 / R_HEAD)
    yc = ys - mu
    var = _seg_sum(yc * yc, R_HEAD) * (1.0 / R_HEAD)
    y = yc * lax.rsqrt(var + R_GN_EPS) * lnxw_ref[...] + lnxb_ref[...]
    y_r = (y + bonus_ref[...]) * g_ref[...]
    br = _dot(y_r.astype(BF16), wbr_ref[...])
    bm = _dot(ym_ref[...].astype(BF16), wbm_ref[...])
    mix_in = _sigmoid(gr_ref[...] + bgr_ref[...]) * br + _sigmoid(gm_ref[...] + bgm_ref[...]) * bm
    mix = _dot(mix_in.astype(BF16), wout_ref[...])
    x1 = _layer_norm(alpha * x_ref[...] + mix, l1w_ref[...], l1b_ref[...])
    x1_o[...] = x1

    xh = x1.astype(BF16)
    xl = (x1 - xh.astype(F32)).astype(BF16)
    logits = (_dot(xh, wrh_ref[...]) + (_dot(xh, wrl_ref[...]) + _dot(xl, wrh_ref[...]))) + br_ref[...]
    tm = logits.shape[0]
    lane_i = lax.broadcasted_iota(I32, (tm, LANES), 1)
    lane = lane_i.astype(F32)
    group_of_lane = (lane_i // EXPERTS_PER_GROUP).astype(F32)
    big = float(LANES)
    neg = -jnp.inf
    lg = jnp.where((lane_i >= C_ROUTE_G) & (lane_i < C_ROUTE_G + N_GROUPS), logits, neg)
    gmax = jnp.max(lg, axis=-1, keepdims=True)
    gsel = jnp.min(jnp.where(lg == gmax, lane - C_ROUTE_G, big), axis=-1, keepdims=True)
    g_w = 1.0 / jnp.sum(jnp.exp(lg - gmax), axis=-1, keepdims=True)
    le = jnp.where((lane_i < N_EXPERTS) & (group_of_lane == gsel), logits, neg)
    m1 = jnp.max(le, axis=-1, keepdims=True)
    i1 = jnp.min(jnp.where(le == m1, lane, big), axis=-1, keepdims=True)
    le2 = jnp.where(lane == i1, neg, le)
    m2 = jnp.max(le2, axis=-1, keepdims=True)
    i2 = jnp.min(jnp.where(le2 == m2, lane, big), axis=-1, keepdims=True)
    e2 = jnp.exp(m2 - m1)
    w1 = g_w / (1.0 + e2)
    w2 = g_w * e2 / (1.0 + e2)
    wts_o[...] = jnp.where(lane_i == 0, w1, jnp.where(lane_i == 1, w2, 0.0))

    oh1 = jnp.where(lane == i1, 1.0, 0.0)
    oh2 = jnp.where(lane == i2, 1.0, 0.0)
    ri = lax.broadcasted_iota(I32, (tm, tm), 0)
    ci = lax.broadcasted_iota(I32, (tm, tm), 1)
    lstrict = jnp.where(ci < ri, 1.0, 0.0).astype(BF16)
    tot1 = jnp.sum(oh1, axis=0, keepdims=True)
    tot2 = jnp.sum(oh2, axis=0, keepdims=True)
    rank1 = jnp.sum(_dot(lstrict, oh1.astype(BF16)) * oh1, axis=-1, keepdims=True)
    rank2 = jnp.sum((_dot(lstrict, oh2.astype(BF16)) + tot1) * oh2, axis=-1, keepdims=True)
    ids = jnp.where(lane_i == 0, i1, jnp.where(lane_i == 1, i2, jnp.where(lane_i == 2, rank1,
                                                                          jnp.where(lane_i == 3, rank2, 0.0))))
    ids_o[...] = ids.astype(I32)
    cnt_o[...] = jnp.broadcast_to(tot1 + tot2, cnt_o.shape).astype(I32)


def _merge(alpha, ys, bonus, g, ym, u, x, lnx_w, lnx_b, b_gate, w_br, w_bm, w_out, ln1_w, ln1_b,
           w_rg, b_rg, w_re, b_re):
    n, d = x.shape
    tm = MERGE_TM
    wr = jnp.zeros((d, LANES), F32).at[:, :N_EXPERTS].set(w_re).at[:, C_ROUTE_G:C_ROUTE_G + N_GROUPS].set(w_rg)
    wr_hi = wr.astype(BF16)
    wr_lo = (wr - wr_hi.astype(F32)).astype(BF16)
    b_r = jnp.zeros((1, LANES), F32).at[0, :N_EXPERTS].set(b_re).at[0, C_ROUTE_G:C_ROUTE_G + N_GROUPS].set(b_rg)
    tile = lambda w_: pl.BlockSpec((tm, w_), lambda i: (i, 0))
    const = lambda shape: pl.BlockSpec(shape, lambda i: (0, 0))
    return pl.pallas_call(
        functools.partial(_merge_kernel, alpha),
        grid=(n // tm,),
        in_specs=[tile(R_WIDTH), tile(R_WIDTH), tile(R_WIDTH), tile(M_WIDTH),
                  pl.BlockSpec((tm, d), lambda i: (i, C_GR // d)),
                  pl.BlockSpec((tm, d), lambda i: (i, C_GM // d)),
                  tile(d),
                  const((1, R_WIDTH)), const((1, R_WIDTH)), const((1, d)), const((1, d)),
                  const((R_WIDTH, d)), const((M_WIDTH, d)), const((d, d)), const((1, d)), const((1, d)),
                  const((d, LANES)), const((d, LANES)), const((1, LANES))],
        out_specs=[tile(d), tile(LANES), tile(LANES), pl.BlockSpec((SUBLANES, LANES), lambda i: (i, 0))],
        out_shape=[jax.ShapeDtypeStruct((n, d), F32), jax.ShapeDtypeStruct((n, LANES), I32),
                   jax.ShapeDtypeStruct((n, LANES), F32),
                   jax.ShapeDtypeStruct((n // tm * SUBLANES, LANES), I32)],
        compiler_params=_cparams(("parallel",)),
        name="merge_ln1_router",
    )(ys, bonus, g, ym, u, u, x, lnx_w[None, :], lnx_b[None, :], b_gate[None, :d], b_gate[None, d:],
      w_br.astype(BF16), w_bm.astype(BF16), w_out.astype(BF16), ln1_w[None, :], ln1_b[None, :],
      wr_hi, wr_lo, b_r)


MOE_CH = 256
DISPATCH_TM = 256


ROW_DMA_UNROLL = 8


def _each(count, fn, unroll=1):
    def body(r, carry):
        fn(r)
        return carry
    lax.fori_loop(0, count, body, 0, unroll=unroll)


def _each_choice(fn):
    def both(r):
        for j in range(TOP_K):
            fn(r, j)
    return both


def _zero_unused_tail(pstart, nchunks, zeros_ref, out_hbm, sem):
    ch = zeros_ref.shape[0]
    used = (pstart[N_EXPERTS - 1] + nchunks[N_EXPERTS - 1] * ch) // ch
    total = out_hbm.shape[0] // ch

    def chunk(c):
        return pltpu.make_async_copy(zeros_ref, out_hbm.at[pl.ds(pl.multiple_of(c * ch, ch), ch)], sem)

    def over_tail(fn):
        def body(c, carry):
            fn(c)
            return carry
        lax.fori_loop(used, total, body, 0)

    over_tail(lambda c: chunk(c).start())
    over_tail(lambda c: chunk(c).wait())


def _moe_dispatch_kernel(dest, pstart, nchunks, x_ref, xs_hbm, zbuf, sem_z, sem):
    i = pl.program_id(0)
    tm = x_ref.shape[0]
    ch = zbuf.shape[0]

    @pl.when(i == 0)
    def _():
        zbuf[...] = jnp.zeros(zbuf.shape, F32)

        def tail(e):
            row0 = pl.multiple_of(pstart[e] + (nchunks[e] - 1) * ch, ch)
            return pltpu.make_async_copy(zbuf, xs_hbm.at[pl.ds(row0, ch)], sem_z)

        def start(e):
            @pl.when(nchunks[e] > 0)
            def _():
                tail(e).start()

        def wait(e):
            @pl.when(nchunks[e] > 0)
            def _():
                tail(e).wait()

        _each(N_EXPERTS, start)
        _each(N_EXPERTS, wait)
        _zero_unused_tail(pstart, nchunks, zbuf, xs_hbm, sem_z)

    def row(r, j):
        return pltpu.make_async_copy(x_ref.at[pl.ds(r, 1)],
                                     xs_hbm.at[pl.ds(dest[(i * tm + r) * TOP_K + j], 1)], sem)

    _each(tm, _each_choice(lambda r, j: row(r, j).start(priority=j)), ROW_DMA_UNROLL)
    _each(tm, _each_choice(lambda r, j: row(r, j).wait()), ROW_DMA_UNROLL)


def _moe_dispatch(x1, dest, pstart, nchunks, rows_pad):
    n, d = x1.shape
    tm = DISPATCH_TM
    return pl.pallas_call(
        _moe_dispatch_kernel,
        grid_spec=pltpu.PrefetchScalarGridSpec(
            num_scalar_prefetch=3,
            grid=(n // tm,),
            in_specs=[pl.BlockSpec((tm, d), lambda i, *_: (i, 0))],
            out_specs=pl.BlockSpec(memory_space=pl.ANY),
            scratch_shapes=[pltpu.VMEM((MOE_CH, d), F32), pltpu.SemaphoreType.DMA, pltpu.SemaphoreType.DMA]),
        out_shape=jax.ShapeDtypeStruct((rows_pad, d), F32),
        compiler_params=_cparams(("arbitrary",)),
        name="moe_dispatch",
    )(dest, pstart, nchunks, x1)


def _moe_expert_kernel(pstart, nchunks, xs_hbm, wg_ref, wu_ref, wd_ref, ys_hbm,
                       xbuf, ybuf, wgb, wub, wdb, sem_in, sem_out):
    e = pl.program_id(0)
    ch = xbuf.shape[1]
    nc = nchunks[e]
    g0 = pstart[e] // ch
    total = (pstart[N_EXPERTS - 1] + nchunks[N_EXPERTS - 1] * ch) // ch

    def rows(g):
        return pl.ds(pl.multiple_of(g * ch, ch), ch)

    def load(g):
        return pltpu.make_async_copy(xs_hbm.at[rows(g)], xbuf.at[g % 2], sem_in.at[g % 2])

    def store(g):
        return pltpu.make_async_copy(ybuf.at[g % 2], ys_hbm.at[rows(g)], sem_out.at[g % 2])

    @pl.when(nc > 0)
    def _():
        @pl.when(g0 == 0)
        def _():
            load(0).start()

        wgb[...] = wg_ref[0].astype(BF16)
        wub[...] = wu_ref[0].astype(BF16)
        wdb[...] = wd_ref[0].astype(BF16)

        def chunk(c, carry):
            g = g0 + c
            load(g).wait()

            @pl.when(g + 1 < total)
            def _():
                load(g + 1).start()

            @pl.when(g >= 2)
            def _():
                store(g - 2).wait()

            xb = xbuf[g % 2].astype(BF16)
            gate = _dot(xb, wgb[...])
            hb = gate * _sigmoid(gate) * _dot(xb, wub[...])
            ybuf[g % 2] = _dot(hb.astype(BF16), wdb[...])
            store(g).start()
            return carry

        lax.fori_loop(0, nc, chunk, 0)

    @pl.when(e == pl.num_programs(0) - 1)
    def _():
        @pl.when(total >= 2)
        def _():
            store(total - 2).wait()

        @pl.when(total >= 1)
        def _():
            store(total - 1).wait()

        ybuf[0] = jnp.zeros(ybuf.shape[1:], F32)
        _zero_unused_tail(pstart, nchunks, ybuf.at[0], ys_hbm, sem_out.at[0])


def _moe_experts(xs, pstart, nchunks, w_gate, w_up, w_down):
    rows_pad, d = xs.shape
    de = w_gate.shape[-1]
    wspec = lambda shape: pl.BlockSpec(shape, lambda e, *_: (e, 0, 0))
    return pl.pallas_call(
        _moe_expert_kernel,
        grid_spec=pltpu.PrefetchScalarGridSpec(
            num_scalar_prefetch=2,
            grid=(N_EXPERTS,),
            in_specs=[pl.BlockSpec(memory_space=pl.ANY),
                      wspec((1, d, de)), wspec((1, d, de)), wspec((1, de, d))],
            out_specs=pl.BlockSpec(memory_space=pl.ANY),
            scratch_shapes=[pltpu.VMEM((2, MOE_CH, d), F32), pltpu.VMEM((2, MOE_CH, d), F32),
                            pltpu.VMEM((d, de), BF16), pltpu.VMEM((d, de), BF16), pltpu.VMEM((de, d), BF16),
                            pltpu.SemaphoreType.DMA((2,)), pltpu.SemaphoreType.DMA((2,))]),
        out_shape=jax.ShapeDtypeStruct((rows_pad, d), F32),
        compiler_params=_cparams(("arbitrary",)),
        name="moe_experts",
    )(pstart, nchunks, xs, w_gate, w_up, w_down)


def _moe_plan(ids, cnt, tm):
    n = ids.shape[0]
    tile_cnt = cnt[::SUBLANES, :N_EXPERTS]
    counts = jnp.sum(tile_cnt, axis=0)
    padded = ((counts + MOE_CH - 1) // MOE_CH) * MOE_CH
    pstart = jnp.cumsum(padded) - padded
    tile_base = pstart[None, :] + jnp.cumsum(tile_cnt, axis=0) - tile_cnt
    eid = ids[:, 0:TOP_K]
    rank = ids[:, TOP_K:2 * TOP_K]
    base_of_tok = jnp.repeat(tile_base, tm, axis=0)[:, None, :]
    chosen = eid[:, :, None] == jnp.arange(N_EXPERTS, dtype=I32)[None, None, :]
    dest = jnp.sum(jnp.where(chosen, base_of_tok, 0), axis=-1) + rank
    return dest.reshape(-1).astype(I32), pstart.astype(I32), (padded // MOE_CH).astype(I32)


FINAL_TM = 256


def _final_kernel(alpha, dest, x1_ref, wts_ref, p_ref, wpg_ref, wple_ref, l2w_ref, l2b_ref, ys_hbm, o_ref,
                  ybuf, sem):
    i = pl.program_id(0)
    tm = x1_ref.shape[0]
    slot = i % 2

    def row(tile, buf, r, j):
        return pltpu.make_async_copy(ys_hbm.at[pl.ds(dest[(tile * tm + r) * TOP_K + j], 1)],
                                     ybuf.at[buf, j, pl.ds(r, 1)], sem.at[buf])

    def gather(tile, buf):
        _each(tm, _each_choice(lambda r, j: row(tile, buf, r, j).start(priority=j)), ROW_DMA_UNROLL)

    @pl.when(i == 0)
    def _():
        gather(0, 0)

    @pl.when(i + 1 < pl.num_programs(0))
    def _():
        gather(i + 1, 1 - slot)

    x1 = x1_ref[...]
    ple = _sigmoid(_dot(x1.astype(BF16), wpg_ref[...])) * _dot(p_ref[...], wple_ref[...])
    _each(tm, _each_choice(lambda r, j: row(i, slot, r, j).wait()), ROW_DMA_UNROLL)
    moe = ybuf[slot, 0] * wts_ref[:, 0:1] + ybuf[slot, 1] * wts_ref[:, 1:2]
    o_ref[...] = _layer_norm(alpha * x1 + moe + ple, l2w_ref[...], l2b_ref[...])


def _final(alpha, dest, x1, ys, wts, p_bf, w_pg, w_ple, ln2_w, ln2_b):
    n, d = x1.shape
    tm = FINAL_TM
    tile = lambda w_: pl.BlockSpec((tm, w_), lambda i, *_: (i, 0))
    const = lambda shape: pl.BlockSpec(shape, lambda i, *_: (0, 0))
    return pl.pallas_call(
        functools.partial(_final_kernel, alpha),
        grid_spec=pltpu.PrefetchScalarGridSpec(
            num_scalar_prefetch=1,
            grid=(n // tm,),
            in_specs=[tile(d), tile(LANES), tile(p_bf.shape[1]),
                      const((d, d)), const((p_bf.shape[1], d)), const((1, d)), const((1, d)),
                      pl.BlockSpec(memory_space=pl.ANY)],
            out_specs=tile(d),
            scratch_shapes=[pltpu.VMEM((2, TOP_K, tm, d), F32), pltpu.SemaphoreType.DMA((2,))]),
        out_shape=jax.ShapeDtypeStruct((n, d), F32),
        compiler_params=_cparams(("arbitrary",)),
        name="final_ln2",
    )(dest, x1, wts, p_bf, w_pg.astype(BF16), w_ple.astype(BF16), ln2_w[None, :], ln2_b[None, :], ys)


def _regroup_w_in(w):
    m0 = RWKV_COLS
    g0 = RWKV_COLS + MLSTM_COLS
    mqk = 2 * M_HEADS * M_QK
    pad = lambda c: jnp.zeros((w.shape[0], c), w.dtype)
    parts = [w[:, 0:3 * R_WIDTH],
             w[:, m0:m0 + mqk],
             w[:, m0 + mqk:m0 + mqk + M_WIDTH],
             w[:, m0 + mqk + M_WIDTH + 2 * M_HEADS:m0 + MLSTM_COLS],
             w[:, g0:],
             w[:, 3 * R_WIDTH:RWKV_COLS], pad(LORA_PAD - LORA_COLS),
             w[:, m0 + mqk + M_WIDTH:m0 + mqk + M_WIDTH + 2 * M_HEADS], pad(LANES - 2 * M_HEADS)]
    out = jnp.concatenate(parts, axis=1)
    assert out.shape[1] == C_TOTAL
    return out


def kernel(x, p, w_in, mu_shift, w0, w_w2, a0, w_a2, w_g2, k_k, k_a, r_k, lnx_w, lnx_b, conv_w, conv_b,
           i_bias, f_bias, mh_w, b_gate, w_br, w_bm, w_out, ln1_w, ln1_b, w_rg, b_rg, w_re, b_re,
           w_gate, w_up, w_down, w_pg, w_ple, ln2_w, ln2_b):
    bsz, seq, d = x.shape
    depth = w_in.shape[0]
    assert bsz * R_HEADS * 2 == LANES and seq % PROJ_TM == 0 and seq % CHUNK == 0
    alpha = (2 * depth) ** 0.25
    n = bsz * seq
    xf = x.reshape(n, d)
    for i in range(depth):
        u = _proj_in(xf.astype(BF16), _regroup_w_in(w_in[i]).astype(BF16))
        r, dec, kp, v, kn, beta, g, bonus = _rwkv_prep(
            u, seq, mu_shift[i], w0[i], w_w2[i], a0[i], w_a2[i], w_g2[i], k_k[i], k_a[i], r_k[i])
        ys = _rwkv_scan(kn, dec, kp, beta, r, v, bsz, seq)
        ym = _mlstm(u, bsz, seq, conv_w[i], conv_b[i], i_bias[i], f_bias[i], mh_w[i])
        x1, ids, wts, cnt = _merge(alpha, ys, bonus, g, ym, u, xf, lnx_w[i], lnx_b[i], b_gate[i], w_br[i],
                                   w_bm[i], w_out[i], ln1_w[i], ln1_b[i], w_rg[i], b_rg[i], w_re[i], b_re[i])
        dest, pstart, nchunks = _moe_plan(ids, cnt, MERGE_TM)
        rows_pad = TOP_K * n + N_EXPERTS * MOE_CH
        xs = _moe_dispatch(x1, dest, pstart, nchunks, rows_pad)
        ys_sorted = _moe_experts(xs, pstart, nchunks, w_gate[i], w_up[i], w_down[i])
        xf = _final(alpha, dest, x1, ys_sorted, wts, p[i].reshape(n, -1).astype(BF16), w_pg[i], w_ple[i],
                    ln2_w[i], ln2_b[i])
    return xf.reshape(bsz, seq, d)
```

```python
import functools

import jax
import jax.numpy as jnp
from jax import lax
from jax.experimental import pallas as pl
from jax.experimental.pallas import tpu as pltpu

F32 = jnp.float32
BF16 = jnp.bfloat16
I32 = jnp.int32

R_HEADS, R_HEAD = 16, 64
R_WIDTH = R_HEADS * R_HEAD
DECAY_LORA, AAA_LORA, GATE_LORA = 64, 64, 160
LORA_COLS = DECAY_LORA + AAA_LORA + GATE_LORA
LORA_PAD = 512
R_GN_EPS = 64e-5
RWKV_COLS = 3 * R_WIDTH + LORA_COLS
M_HEADS, M_QK, M_V = 8, 64, 128
M_WIDTH = M_HEADS * M_V
CONV_K = 4
CHUNK = 128
M_NORM_EPS = 1e-6
MLSTM_COLS = 2 * M_HEADS * M_QK + 2 * M_WIDTH + 2 * M_HEADS
N_GROUPS, EXPERTS_PER_GROUP = 4, 8
N_EXPERTS = N_GROUPS * EXPERTS_PER_GROUP
TOP_K = 2
MOE_BLOCK = 128
LN_EPS = 1e-5

LANES = 128
SUBLANES = 8
MXU_DIM = 256
V7X_VMEM_BYTES = 64 * 1024 * 1024
VMEM_LIMIT = 56 * 1024 * 1024
SCAN_VMEM_LIMIT = 60 * 1024 * 1024

C_RKV = 0
C_MQK = 3072
C_MV = 4096
C_MO = 5120
C_GR = 6144
C_GM = 8192
C_LORA = 10240
C_MG = 10752
C_TOTAL = 10880
PROJ_TN = 2176
PROJ_TM = 512


def _cparams(sem, vmem=VMEM_LIMIT):
    return pltpu.CompilerParams(dimension_semantics=sem, vmem_limit_bytes=vmem)


def _sigmoid(x):
    return 1.0 / (1.0 + jnp.exp(-x))


def _softplus(x):
    return jnp.maximum(x, 0.0) + jnp.log1p(jnp.exp(-jnp.abs(x)))


def _split3(x):
    hi = x.astype(BF16)
    r1 = x - hi.astype(F32)
    mid = r1.astype(BF16)
    lo = (r1 - mid.astype(F32)).astype(BF16)
    return hi, mid, lo


def _dot(a, b):
    return jnp.dot(a, b, preferred_element_type=F32)


def _dot_nt(a, b):
    return lax.dot_general(a, b, (((1,), (1,)), ((), ())), preferred_element_type=F32)


def _dot_tn(a, b):
    return lax.dot_general(a, b, (((0,), (0,)), ((), ())), preferred_element_type=F32)


def _dot_exact_rhs(x, ones_bf16, terms=3):
    parts = _split3(x)[:terms]
    acc = _dot(parts[0], ones_bf16)
    for part in parts[1:]:
        acc = acc + _dot(part, ones_bf16)
    return acc


def _block_ones(n, group):
    r = lax.broadcasted_iota(I32, (n, n), 0) // group
    c = lax.broadcasted_iota(I32, (n, n), 1) // group
    return jnp.where(r == c, 1.0, 0.0).astype(BF16)


def _seg_sum(x, group):
    ones = _block_ones(MXU_DIM, group)
    slabs = [_dot_exact_rhs(x[:, p * MXU_DIM:(p + 1) * MXU_DIM], ones, terms=2)
             for p in range(x.shape[1] // MXU_DIM)]
    return jnp.concatenate(slabs, axis=1)


def _layer_norm(x, w, b):
    mu = jnp.mean(x, axis=-1, keepdims=True)
    xc = x - mu
    var = jnp.mean(xc * xc, axis=-1, keepdims=True)
    return xc * lax.rsqrt(var + LN_EPS) * w + b


def _proj_kernel(x_ref, w_ref, o_ref):
    o_ref[...] = _dot(x_ref[...].astype(BF16), w_ref[...])


def _proj_in(x_bf, w_bf):
    m, k = x_bf.shape
    n = w_bf.shape[1]
    return pl.pallas_call(
        _proj_kernel,
        grid=(n // PROJ_TN, m // PROJ_TM),
        in_specs=[pl.BlockSpec((PROJ_TM, k), lambda j, i: (i, 0)),
                  pl.BlockSpec((k, PROJ_TN), lambda j, i: (0, j))],
        out_specs=pl.BlockSpec((PROJ_TM, PROJ_TN), lambda j, i: (i, j)),
        out_shape=jax.ShapeDtypeStruct((m, n), F32),
        compiler_params=_cparams(("parallel", "parallel")),
        name="proj_in",
    )(x_bf, w_bf)


PREP_TM = 256


def _rwkv_prep_kernel(seq, u_ref, l_ref, up_ref, lp_ref, mu_ref, mul_ref, w0_ref, a0_ref, kk_ref, ka_ref,
                      rk_ref, ww_ref, wa_ref, wg_ref,
                      r_o, dec_o, kp_o, v_o, kn_o, beta_o, g_o, bonus_o):
    i = pl.program_id(0)
    tm = u_ref.shape[0]
    first = (i * tm) % seq == 0
    row = lax.broadcasted_iota(I32, (tm, 1), 0)

    def shift(u, prev8):
        prev_row = jnp.where(first, 0.0, prev8[SUBLANES - 1:SUBLANES, :])
        return jnp.where(row == 0, prev_row, pltpu.roll(u, 1, 0))

    u = u_ref[...]
    z = u + mu_ref[...] * (shift(u, up_ref[...]) - u)
    lo = l_ref[...]
    zl = lo + mul_ref[...] * (shift(lo, lp_ref[...]) - lo)

    r = z[:, 0:R_WIDTH]
    k = z[:, R_WIDTH:2 * R_WIDTH]
    v = z[:, 2 * R_WIDTH:3 * R_WIDTH]
    w_pre = w0_ref[...] + _dot(jnp.tanh(zl).astype(BF16), ww_ref[...])
    w = -_softplus(-w_pre) - 0.5
    dec = jnp.exp(-jnp.exp(w))
    a = _sigmoid(a0_ref[...] + _dot(zl.astype(BF16), wa_ref[...]))
    g = _dot(_sigmoid(zl).astype(BF16), wg_ref[...])

    kk = k * kk_ref[...]
    nrm = jnp.sqrt(_seg_sum(kk * kk, R_HEAD))
    kn = kk / jnp.maximum(nrm, 1e-12)
    kp = k * (1.0 + (a - 1.0) * ka_ref[...])
    bonus = _seg_sum(r * kp * rk_ref[...], R_HEAD) * v

    r_o[...] = r
    dec_o[...] = dec
    kp_o[...] = kp
    v_o[...] = v
    kn_o[...] = kn
    beta_o[...] = kn * a
    g_o[...] = g
    bonus_o[...] = bonus


def _rwkv_prep(u, seq, mu, w0, w_w2, a0, w_a2, w_g2, k_k, k_a, r_k):
    n = u.shape[0]
    tm = PREP_TM
    mu_rkv = mu[None, :3 * R_WIDTH]
    mu_lora = jnp.zeros((1, LORA_PAD), F32).at[0, :LORA_COLS].set(mu[3 * R_WIDTH:])
    ww = jnp.zeros((LORA_PAD, R_WIDTH), BF16).at[0:DECAY_LORA].set(w_w2.astype(BF16))
    wa = jnp.zeros((LORA_PAD, R_WIDTH), BF16).at[DECAY_LORA:DECAY_LORA + AAA_LORA].set(w_a2.astype(BF16))
    wg = jnp.zeros((LORA_PAD, R_WIDTH), BF16).at[DECAY_LORA + AAA_LORA:LORA_COLS].set(w_g2.astype(BF16))
    row = lambda t: t.reshape(1, R_WIDTH)
    prev_blk = lambda i: jnp.maximum(i * (tm // SUBLANES) - 1, 0)
    const = lambda shape: pl.BlockSpec(shape, lambda i: (0, 0))
    nat = jax.ShapeDtypeStruct((n, R_WIDTH), F32)
    return pl.pallas_call(
        functools.partial(_rwkv_prep_kernel, seq),
        grid=(n // tm,),
        in_specs=[pl.BlockSpec((tm, 3 * R_WIDTH), lambda i: (i, C_RKV // (3 * R_WIDTH))),
                  pl.BlockSpec((tm, LORA_PAD), lambda i: (i, C_LORA // LORA_PAD)),
                  pl.BlockSpec((SUBLANES, 3 * R_WIDTH), lambda i: (prev_blk(i), C_RKV // (3 * R_WIDTH))),
                  pl.BlockSpec((SUBLANES, LORA_PAD), lambda i: (prev_blk(i), C_LORA // LORA_PAD)),
                  const((1, 3 * R_WIDTH)), const((1, LORA_PAD)),
                  const((1, R_WIDTH)), const((1, R_WIDTH)), const((1, R_WIDTH)), const((1, R_WIDTH)),
                  const((1, R_WIDTH)),
                  const((LORA_PAD, R_WIDTH)), const((LORA_PAD, R_WIDTH)), const((LORA_PAD, R_WIDTH))],
        out_specs=[pl.BlockSpec((tm, R_WIDTH), lambda i: (i, 0))] * 8,
        out_shape=[nat] * 8,
        compiler_params=_cparams(("parallel",)),
        name="rwkv_prep",
    )(u, u, u, u, mu_rkv, mu_lora, row(w0), row(a0), row(k_k), row(k_a), row(r_k), ww, wa, wg)


SCAN_TB = LANES // 2
SCAN_OUT = 2
SCAN_V = R_HEAD // 2
SCAN_ACC = 4
SCAN_KOPS = 5
SCAN_PAIRS = ((0, 1), (2, 3), (4, 5))
SCAN_HP = R_WIDTH // LANES
SCAN_STEPS1 = 4
SCAN_STEPS2 = 3
SCAN_KEYS_PER_TRIP = 4
SCAN_TRIPS2 = R_HEAD // SCAN_KEYS_PER_TRIP
Z_PITCH = R_HEAD + SUBLANES
V_PITCH = SCAN_V + SUBLANES
RELAYOUT_UNROLL = 8


def _rwkv_scan_kernel(kn_hbm, dec_hbm, kp_hbm, beta_hbm, r_hbm, v_hbm, y_ref,
                      s_ref, stage, z_ref, xs_ref, vs_ref, ys_ref, sem):
    i = pl.program_id(0)
    nblk = pl.num_programs(0)
    nb = stage.shape[2]
    tb = SCAN_TB
    slab = R_HEADS * Z_PITCH
    zrows = nb * slab
    cur = i % 2
    srcs = (kn_hbm, dec_hbm, kp_hbm, beta_hbm, r_hbm, v_hbm)

    def fetch(op, blk, par):
        return pltpu.make_async_copy(srcs[op].at[:, pl.ds(blk * tb, tb), :], stage.at[par, op], sem.at[par])

    def fetch_start(blk, par):
        for op in range(len(srcs)):
            fetch(op, blk, par).start()

    def fetch_wait(blk, par):
        for op in range(len(srcs)):
            fetch(op, blk, par).wait()

    def to_time_on_lanes(par, b):
        for p, (oa, ob) in enumerate(SCAN_PAIRS):
            for hp in range(SCAN_HP):
                cols = slice(hp * LANES, (hp + 1) * LANES)
                tt = jnp.concatenate([stage[par, oa, b, :, cols], stage[par, ob, b, :, cols]], axis=0).T
                for hh in range(2):
                    row0 = pl.multiple_of(p * zrows + (b * R_HEADS + 2 * hp + hh) * Z_PITCH, SUBLANES)
                    z_ref[pl.ds(row0, R_HEAD), :] = tt[hh * R_HEAD:(hh + 1) * R_HEAD]

    def head_rows(p, b, c):
        return z_ref[pl.ds(p * zrows + b * slab + c, R_HEADS, stride=Z_PITCH), :]

    def key_unit(buf, p, k):
        oa, ob = SCAN_PAIRS[p]
        rows = [head_rows(p, b, k) for b in range(nb)]
        tt = jnp.concatenate(rows + rows, axis=0).T
        xs_ref[buf, oa, k] = tt[0:tb]
        if ob < SCAN_KOPS:
            xs_ref[buf, ob, k] = tt[tb:2 * tb]

    def value_unit(buf, vp):
        rows = [head_rows(len(SCAN_PAIRS) - 1, b, vh * SCAN_V + vp) for vh in range(2) for b in range(nb)]
        tt = jnp.concatenate(rows, axis=0).T
        vs_ref[pl.ds(buf * tb * V_PITCH + vp, tb, stride=V_PITCH), :] = tt[tb:2 * tb]

    def chain_units(buf, q):
        units = [functools.partial(key_unit, buf, p, SCAN_KEYS_PER_TRIP * q + kk)
                 for kk in range(SCAN_KEYS_PER_TRIP) for p in range(len(SCAN_PAIRS))]
        units += [functools.partial(value_unit, buf, (SCAN_KEYS_PER_TRIP // 2) * q + vv)
                  for vv in range(SCAN_KEYS_PER_TRIP // 2)]
        return units

    @pl.when(i == 0)
    def _():
        s_ref[...] = jnp.zeros(s_ref.shape, F32)
        fetch_start(0, 0)
        fetch_wait(0, 0)

        @pl.when(nblk > 1)
        def _():
            fetch_start(1, 1)

        _each(nb, lambda b: to_time_on_lanes(0, b))
        _each(SCAN_TRIPS2, lambda q: [unit() for unit in chain_units(0, q)])

    @pl.when(i + 1 < nblk)
    def _():
        fetch_wait(i + 1, 1 - cur)

    @pl.when(i + 2 < nblk)
    def _():
        fetch_start(i + 2, cur)

    def total(acc):
        return (acc[0] + acc[1]) + (acc[2] + acc[3])

    def add_term(acc, k, term):
        acc[k % SCAN_ACC] = term if acc[k % SCAN_ACC] is None else acc[k % SCAN_ACC] + term

    def key_row(op, k, t):
        return xs_ref[cur, op, k, pl.ds(t, 1), :]

    acc = [None] * SCAN_ACC
    for k in range(R_HEAD):
        add_term(acc, k, s_ref[k] * xs_ref[cur, 0, k, 0:1, :])

    def step(t, s_kk):
        v_t = vs_ref[pl.ds(pl.multiple_of((cur * tb + t) * V_PITCH, SUBLANES), SCAN_V), :]
        t_next = jnp.minimum(t + 1, tb - 1)
        acc_y = [None] * SCAN_ACC
        acc_s = [None] * SCAN_ACC
        for k in range(R_HEAD):
            s_new = (s_ref[k] * key_row(1, k, t)
                     + (v_t * key_row(2, k, t) - s_kk * key_row(3, k, t)))
            s_ref[k] = s_new
            add_term(acc_y, k, s_new * key_row(4, k, t))
            add_term(acc_s, k, s_new * key_row(0, k, t_next))
        out_row = pl.multiple_of(((i % SCAN_OUT) * tb + t) * V_PITCH, SUBLANES)
        ys_ref[pl.ds(out_row, SCAN_V), :] = total(acc_y)
        return total(acc_s)

    def first_part(b, s_kk):
        for j in range(SCAN_STEPS1):
            s_kk = step(b * SCAN_STEPS1 + j, s_kk)
        to_time_on_lanes(1 - cur, b)
        return s_kk

    def second_part(q, s_kk):
        for j in range(SCAN_STEPS2):
            s_kk = step(nb * SCAN_STEPS1 + q * SCAN_STEPS2 + j, s_kk)
        for unit in chain_units(1 - cur, q):
            unit()
        return s_kk

    s_kk = lax.fori_loop(0, nb, first_part, total(acc))
    lax.fori_loop(0, SCAN_TRIPS2, second_part, s_kk)

    @pl.when(i % SCAN_OUT == SCAN_OUT - 1)
    def _():
        steps = SCAN_OUT * tb

        def out_rows(vp, carry):
            tt = ys_ref[pl.ds(vp, steps, stride=V_PITCH), :].T
            for vh in range(2):
                for b in range(nb):
                    g = vh * nb + b
                    z_ref[pl.ds(b * slab + vh * SCAN_V + vp, R_HEADS, stride=Z_PITCH), :] = \
                        tt[g * R_HEADS:(g + 1) * R_HEADS]
            return carry
        lax.fori_loop(0, SCAN_V, out_rows, 0, unroll=RELAYOUT_UNROLL)
        for b in range(nb):
            for hp in range(SCAN_HP):
                pair = [z_ref[pl.ds((b * R_HEADS + 2 * hp + hh) * Z_PITCH, R_HEAD), :] for hh in range(2)]
                y_ref[b, :, hp * LANES:(hp + 1) * LANES] = jnp.concatenate(pair, axis=0).T


def _rwkv_scan(kn, dec, kp, beta, r, v, bsz, seq):
    shape3 = (bsz, seq, R_WIDTH)
    ops = [a.reshape(shape3) for a in (kn, dec, kp, beta, r, v)]
    assert SCAN_TB == bsz * SCAN_STEPS1 + SCAN_TRIPS2 * SCAN_STEPS2 and seq % (SCAN_OUT * SCAN_TB) == 0
    y = pl.pallas_call(
        _rwkv_scan_kernel,
        grid=(seq // SCAN_TB,),
        in_specs=[pl.BlockSpec(memory_space=pl.ANY)] * len(ops),
        out_specs=pl.BlockSpec((bsz, SCAN_OUT * SCAN_TB, R_WIDTH), lambda i: (0, i // SCAN_OUT, 0)),
        out_shape=jax.ShapeDtypeStruct(shape3, F32),
        scratch_shapes=[pltpu.VMEM((R_HEAD, SCAN_V, LANES), F32),
                        pltpu.VMEM((2, len(ops), bsz, SCAN_TB, R_WIDTH), F32),
                        pltpu.VMEM((len(SCAN_PAIRS) * bsz * R_HEADS * Z_PITCH, LANES), F32),
                        pltpu.VMEM((2, SCAN_KOPS, R_HEAD, SCAN_TB, LANES), F32),
                        pltpu.VMEM((2 * SCAN_TB * V_PITCH, LANES), F32),
                        pltpu.VMEM((SCAN_OUT * SCAN_TB * V_PITCH, LANES), F32),
                        pltpu.SemaphoreType.DMA((2,))],
        compiler_params=_cparams(("arbitrary",), vmem=SCAN_VMEM_LIMIT),
        name="rwkv_scan",
    )(*ops)
    return y.reshape(bsz * seq, R_WIDTH)


M_AUG = 2 * M_V


def _mlstm_kernel(qk_ref, v_ref, o_ref, g_ref, cw_ref, cb_ref, gb_ref, mh_ref, y_ref,
                  ext_ref, c_ref, m_ref):
    c_idx = pl.program_id(0)
    L = CHUNK
    nb = qk_ref.shape[0]

    @pl.when(c_idx == 0)
    def _():
        ext_ref[:, 0:SUBLANES, :] = jnp.zeros((nb, SUBLANES, ext_ref.shape[2]), F32)
        c_ref[...] = jnp.zeros(c_ref.shape, F32)
        m_ref[...] = jnp.full(m_ref.shape, -jnp.inf, F32)

    lane = lax.broadcasted_iota(I32, (L, LANES), 1)
    ti = lax.broadcasted_iota(I32, (L, L), 0)
    si = lax.broadcasted_iota(I32, (L, L), 1)
    causal = si <= ti
    ltri = jnp.where(causal, 1.0, 0.0).astype(BF16)
    utri = jnp.where(ti <= si, 1.0, 0.0).astype(BF16)
    ones_col = jnp.where(lane == 0, 1.0, 0.0).astype(BF16)

    q_all, k_all, gcols, grows, acols, arows = [], [], [], [], [], []
    for b in range(nb):
        ext_ref[b, SUBLANES:SUBLANES + L, :] = qk_ref[b]
        conv = cb_ref[...]
        for j in range(CONV_K):
            off = SUBLANES - (CONV_K - 1) + j
            conv = conv + cw_ref[j:j + 1, :] * ext_ref[b, off:off + L, :]
        ext_ref[b, 0:SUBLANES, :] = ext_ref[b, L:L + SUBLANES, :]
        qk = conv * _sigmoid(conv)
        q_all.append(qk[:, :M_HEADS * M_QK].astype(BF16))
        k_all.append(qk[:, M_HEADS * M_QK:] * (M_QK ** -0.5))
        gpre = g_ref[b] + gb_ref[...]
        gcols.append(jnp.where(lane < M_HEADS, gpre, -_softplus(-gpre)))
        grows.append(gcols[b].T)
        acols.append(_dot_exact_rhs_left(ltri, gcols[b]))
        arows.append(_dot_exact_rhs(grows[b], utri))

    chains = [(b, h) for b in range(nb) for h in range(M_HEADS)]
    every = range(len(chains))
    a_col = [acols[b][:, M_HEADS + h:M_HEADS + h + 1] for b, h in chains]
    i_col = [gcols[b][:, h:h + 1] for b, h in chains]
    m_st = [m_ref[u:u + 1, 0:1] for u in every]
    d = [jnp.where(causal, a_col[u] - arows[b][M_HEADS + h:M_HEADS + h + 1, :] + grows[b][h:h + 1, :], -jnp.inf)
         for u, (b, h) in enumerate(chains)]
    dmax = [jnp.max(d[u], axis=-1, keepdims=True) for u in every]
    inter = [a_col[u] + m_st[u] for u in every]
    m_t = [jnp.maximum(inter[u], dmax[u]) for u in every]
    q = [q_all[b][:, h * M_QK:(h + 1) * M_QK] for b, h in chains]
    k = [k_all[b][:, h * M_QK:(h + 1) * M_QK] for b, h in chains]
    v_aug = [jnp.concatenate([v_ref[b, :, h * M_V:(h + 1) * M_V].astype(BF16), ones_col], axis=1)
             for b, h in chains]
    s = [_dot_nt(q[u], k[u].astype(BF16)) * jnp.exp(d[u] - m_t[u]) for u in every]
    ie = [jnp.exp(inter[u] - m_t[u]) for u in every]
    tot = [ie[u] * _dot(q[u], c_ref[u].astype(BF16)) + _dot(s[u].astype(BF16), v_aug[u]) for u in every]
    h_c = [tot[u][:, :M_V] / jnp.maximum(jnp.abs(tot[u][:, M_V:M_V + 1]), jnp.exp(-m_t[u])) for u in every]

    a_tot = [a_col[u][L - 1:L, :] for u in every]
    gl = [a_tot[u] - a_col[u] + i_col[u] for u in every]
    m_new = [jnp.maximum(a_tot[u] + m_st[u], jnp.max(gl[u], axis=0, keepdims=True)) for u in every]
    kg = [(k[u] * jnp.exp(gl[u] - m_new[u])).astype(BF16) for u in every]
    for u in every:
        c_ref[u] = jnp.exp(a_tot[u] + m_st[u] - m_new[u]) * c_ref[u] + _dot_tn(kg[u], v_aug[u])
        m_ref[u:u + 1, :] = jnp.broadcast_to(m_new[u], (1, LANES))

    mu = [jnp.mean(h_c[u], axis=-1, keepdims=True) for u in every]
    hc = [h_c[u] - mu[u] for u in every]
    var = [jnp.mean(hc[u] * hc[u], axis=-1, keepdims=True) for u in every]
    for u, (b, h) in enumerate(chains):
        hn = hc[u] * lax.rsqrt(var[u] + M_NORM_EPS) * mh_ref[:, h * M_V:(h + 1) * M_V]
        y_ref[b, :, h * M_V:(h + 1) * M_V] = _sigmoid(o_ref[b, :, h * M_V:(h + 1) * M_V]) * hn


def _dot_exact_rhs_left(ones_bf16, x):
    hi, mid, lo = _split3(x)
    return _dot(ones_bf16, hi) + _dot(ones_bf16, mid) + _dot(ones_bf16, lo)


def _mlstm(u, bsz, seq, conv_w, conv_b, i_bias, f_bias, mh_w):
    nc = seq // CHUNK
    w = M_WIDTH
    gbias = jnp.zeros((1, LANES), F32).at[0, :M_HEADS].set(i_bias).at[0, M_HEADS:2 * M_HEADS].set(f_bias)
    u3 = u.reshape(bsz, seq, u.shape[1])
    blk = lambda col: pl.BlockSpec((bsz, CHUNK, w), lambda c: (0, c, col // w))
    const = lambda shape: pl.BlockSpec(shape, lambda c: (0, 0))
    y = pl.pallas_call(
        _mlstm_kernel,
        grid=(nc,),
        in_specs=[blk(C_MQK), blk(C_MV), blk(C_MO),
                  pl.BlockSpec((bsz, CHUNK, LANES), lambda c: (0, c, C_MG // LANES)),
                  const((CONV_K, w)), const((1, w)), const((1, LANES)), const((1, w))],
        out_specs=pl.BlockSpec((bsz, CHUNK, w), lambda c: (0, c, 0)),
        out_shape=jax.ShapeDtypeStruct((bsz, seq, w), F32),
        scratch_shapes=[pltpu.VMEM((bsz, CHUNK + SUBLANES, w), F32),
                        pltpu.VMEM((bsz * M_HEADS, M_QK, M_AUG), F32),
                        pltpu.VMEM((bsz * M_HEADS, LANES), F32)],
        compiler_params=_cparams(("arbitrary",)),
        name="mlstm",
    )(u3, u3, u3, u3, conv_w, conv_b[None, :], gbias, mh_w[None, :])
    return y.reshape(bsz * seq, w)


MERGE_TM = 256
C_ROUTE_G = N_EXPERTS


def _merge_kernel(alpha, ys_ref, bonus_ref, g_ref, ym_ref, gr_ref, gm_ref, x_ref,
                  lnxw_ref, lnxb_ref, bgr_ref, bgm_ref, wbr_ref, wbm_ref, wout_ref, l1w_ref, l1b_ref,
                  wrh_ref, wrl_ref, br_ref, x1_o, ids_o, wts_o, cnt_o):
    ys = ys_ref[...]
    mu = _seg_sum(ys, R_HEAD) * (1.0 / R_HEAD)
    yc = ys - mu
    var = _seg_sum(yc * yc, R_HEAD) * (1.0 / R_HEAD)
    y = yc * lax.rsqrt(var + R_GN_EPS) * lnxw_ref[...] + lnxb_ref[...]
    y_r = (y + bonus_ref[...]) * g_ref[...]
    br = _dot(y_r.astype(BF16), wbr_ref[...])
    bm = _dot(ym_ref[...].astype(BF16), wbm_ref[...])
    mix_in = _sigmoid(gr_ref[...] + bgr_ref[...]) * br + _sigmoid(gm_ref[...] + bgm_ref[...]) * bm
    mix = _dot(mix_in.astype(BF16), wout_ref[...])
    x1 = _layer_norm(alpha * x_ref[...] + mix, l1w_ref[...], l1b_ref[...])
    x1_o[...] = x1

    xh = x1.astype(BF16)
    xl = (x1 - xh.astype(F32)).astype(BF16)
    logits = (_dot(xh, wrh_ref[...]) + (_dot(xh, wrl_ref[...]) + _dot(xl, wrh_ref[...]))) + br_ref[...]
    tm = logits.shape[0]
    lane_i = lax.broadcasted_iota(I32, (tm, LANES), 1)
    lane = lane_i.astype(F32)
    group_of_lane = (lane_i // EXPERTS_PER_GROUP).astype(F32)
    big = float(LANES)
    neg = -jnp.inf
    lg = jnp.where((lane_i >= C_ROUTE_G) & (lane_i < C_ROUTE_G + N_GROUPS), logits, neg)
    gmax = jnp.max(lg, axis=-1, keepdims=True)
    gsel = jnp.min(jnp.where(lg == gmax, lane - C_ROUTE_G, big), axis=-1, keepdims=True)
    g_w = 1.0 / jnp.sum(jnp.exp(lg - gmax), axis=-1, keepdims=True)
    le = jnp.where((lane_i < N_EXPERTS) & (group_of_lane == gsel), logits, neg)
    m1 = jnp.max(le, axis=-1, keepdims=True)
    i1 = jnp.min(jnp.where(le == m1, lane, big), axis=-1, keepdims=True)
    le2 = jnp.where(lane == i1, neg, le)
    m2 = jnp.max(le2, axis=-1, keepdims=True)
    i2 = jnp.min(jnp.where(le2 == m2, lane, big), axis=-1, keepdims=True)
    e2 = jnp.exp(m2 - m1)
    w1 = g_w / (1.0 + e2)
    w2 = g_w * e2 / (1.0 + e2)
    wts_o[...] = jnp.where(lane_i == 0, w1, jnp.where(lane_i == 1, w2, 0.0))

    oh1 = jnp.where(lane == i1, 1.0, 0.0)
    oh2 = jnp.where(lane == i2, 1.0, 0.0)
    ri = lax.broadcasted_iota(I32, (tm, tm), 0)
    ci = lax.broadcasted_iota(I32, (tm, tm), 1)
    lstrict = jnp.where(ci < ri, 1.0, 0.0).astype(BF16)
    tot1 = jnp.sum(oh1, axis=0, keepdims=True)
    tot2 = jnp.sum(oh2, axis=0, keepdims=True)
    rank1 = jnp.sum(_dot(lstrict, oh1.astype(BF16)) * oh1, axis=-1, keepdims=True)
    rank2 = jnp.sum((_dot(lstrict, oh2.astype(BF16)) + tot1) * oh2, axis=-1, keepdims=True)
    ids = jnp.where(lane_i == 0, i1, jnp.where(lane_i == 1, i2, jnp.where(lane_i == 2, rank1,
                                                                          jnp.where(lane_i == 3, rank2, 0.0))))
    ids_o[...] = ids.astype(I32)
    cnt_o[...] = jnp.broadcast_to(tot1 + tot2, cnt_o.shape).astype(I32)


def _merge(alpha, ys, bonus, g, ym, u, x, lnx_w, lnx_b, b_gate, w_br, w_bm, w_out, ln1_w, ln1_b,
           w_rg, b_rg, w_re, b_re):
    n, d = x.shape
    tm = MERGE_TM
    wr = jnp.zeros((d, LANES), F32).at[:, :N_EXPERTS].set(w_re).at[:, C_ROUTE_G:C_ROUTE_G + N_GROUPS].set(w_rg)
    wr_hi = wr.astype(BF16)
    wr_lo = (wr - wr_hi.astype(F32)).astype(BF16)
    b_r = jnp.zeros((1, LANES), F32).at[0, :N_EXPERTS].set(b_re).at[0, C_ROUTE_G:C_ROUTE_G + N_GROUPS].set(b_rg)
    tile = lambda w_: pl.BlockSpec((tm, w_), lambda i: (i, 0))
    const = lambda shape: pl.BlockSpec(shape, lambda i: (0, 0))
    return pl.pallas_call(
        functools.partial(_merge_kernel, alpha),
        grid=(n // tm,),
        in_specs=[tile(R_WIDTH), tile(R_WIDTH), tile(R_WIDTH), tile(M_WIDTH),
                  pl.BlockSpec((tm, d), lambda i: (i, C_GR // d)),
                  pl.BlockSpec((tm, d), lambda i: (i, C_GM // d)),
                  tile(d),
                  const((1, R_WIDTH)), const((1, R_WIDTH)), const((1, d)), const((1, d)),
                  const((R_WIDTH, d)), const((M_WIDTH, d)), const((d, d)), const((1, d)), const((1, d)),
                  const((d, LANES)), const((d, LANES)), const((1, LANES))],
        out_specs=[tile(d), tile(LANES), tile(LANES), pl.BlockSpec((SUBLANES, LANES), lambda i: (i, 0))],
        out_shape=[jax.ShapeDtypeStruct((n, d), F32), jax.ShapeDtypeStruct((n, LANES), I32),
                   jax.ShapeDtypeStruct((n, LANES), F32),
                   jax.ShapeDtypeStruct((n // tm * SUBLANES, LANES), I32)],
        compiler_params=_cparams(("parallel",)),
        name="merge_ln1_router",
    )(ys, bonus, g, ym, u, u, x, lnx_w[None, :], lnx_b[None, :], b_gate[None, :d], b_gate[None, d:],
      w_br.astype(BF16), w_bm.astype(BF16), w_out.astype(BF16), ln1_w[None, :], ln1_b[None, :],
      wr_hi, wr_lo, b_r)


MOE_CH = 256
DISPATCH_TM = 256


ROW_DMA_UNROLL = 8


def _each(count, fn, unroll=1):
    def body(r, carry):
        fn(r)
        return carry
    lax.fori_loop(0, count, body, 0, unroll=unroll)


def _each_choice(fn):
    def both(r):
        for j in range(TOP_K):
            fn(r, j)
    return both


def _zero_unused_tail(pstart, nchunks, zeros_ref, out_hbm, sem):
    ch = zeros_ref.shape[0]
    used = (pstart[N_EXPERTS - 1] + nchunks[N_EXPERTS - 1] * ch) // ch
    total = out_hbm.shape[0] // ch

    def chunk(c):
        return pltpu.make_async_copy(zeros_ref, out_hbm.at[pl.ds(pl.multiple_of(c * ch, ch), ch)], sem)

    def over_tail(fn):
        def body(c, carry):
            fn(c)
            return carry
        lax.fori_loop(used, total, body, 0)

    over_tail(lambda c: chunk(c).start())
    over_tail(lambda c: chunk(c).wait())


def _moe_dispatch_kernel(dest, pstart, nchunks, x_ref, xs_hbm, zbuf, sem_z, sem):
    i = pl.program_id(0)
    tm = x_ref.shape[0]
    ch = zbuf.shape[0]

    @pl.when(i == 0)
    def _():
        zbuf[...] = jnp.zeros(zbuf.shape, F32)

        def tail(e):
            row0 = pl.multiple_of(pstart[e] + (nchunks[e] - 1) * ch, ch)
            return pltpu.make_async_copy(zbuf, xs_hbm.at[pl.ds(row0, ch)], sem_z)

        def start(e):
            @pl.when(nchunks[e] > 0)
            def _():
                tail(e).start()

        def wait(e):
            @pl.when(nchunks[e] > 0)
            def _():
                tail(e).wait()

        _each(N_EXPERTS, start)
        _each(N_EXPERTS, wait)
        _zero_unused_tail(pstart, nchunks, zbuf, xs_hbm, sem_z)

    def row(r, j):
        return pltpu.make_async_copy(x_ref.at[pl.ds(r, 1)],
                                     xs_hbm.at[pl.ds(dest[(i * tm + r) * TOP_K + j], 1)], sem)

    _each(tm, _each_choice(lambda r, j: row(r, j).start(priority=j)), ROW_DMA_UNROLL)
    _each(tm, _each_choice(lambda r, j: row(r, j).wait()), ROW_DMA_UNROLL)


def _moe_dispatch(x1, dest, pstart, nchunks, rows_pad):
    n, d = x1.shape
    tm = DISPATCH_TM
    return pl.pallas_call(
        _moe_dispatch_kernel,
        grid_spec=pltpu.PrefetchScalarGridSpec(
            num_scalar_prefetch=3,
            grid=(n // tm,),
            in_specs=[pl.BlockSpec((tm, d), lambda i, *_: (i, 0))],
            out_specs=pl.BlockSpec(memory_space=pl.ANY),
            scratch_shapes=[pltpu.VMEM((MOE_CH, d), F32), pltpu.SemaphoreType.DMA, pltpu.SemaphoreType.DMA]),
        out_shape=jax.ShapeDtypeStruct((rows_pad, d), F32),
        compiler_params=_cparams(("arbitrary",)),
        name="moe_dispatch",
    )(dest, pstart, nchunks, x1)


def _moe_expert_kernel(pstart, nchunks, xs_hbm, wg_ref, wu_ref, wd_ref, ys_hbm,
                       xbuf, ybuf, wgb, wub, wdb, sem_in, sem_out):
    e = pl.program_id(0)
    ch = xbuf.shape[1]
    nc = nchunks[e]
    g0 = pstart[e] // ch
    total = (pstart[N_EXPERTS - 1] + nchunks[N_EXPERTS - 1] * ch) // ch

    def rows(g):
        return pl.ds(pl.multiple_of(g * ch, ch), ch)

    def load(g):
        return pltpu.make_async_copy(xs_hbm.at[rows(g)], xbuf.at[g % 2], sem_in.at[g % 2])

    def store(g):
        return pltpu.make_async_copy(ybuf.at[g % 2], ys_hbm.at[rows(g)], sem_out.at[g % 2])

    @pl.when(nc > 0)
    def _():
        @pl.when(g0 == 0)
        def _():
            load(0).start()

        wgb[...] = wg_ref[0].astype(BF16)
        wub[...] = wu_ref[0].astype(BF16)
        wdb[...] = wd_ref[0].astype(BF16)

        def chunk(c, carry):
            g = g0 + c
            load(g).wait()

            @pl.when(g + 1 < total)
            def _():
                load(g + 1).start()

            @pl.when(g >= 2)
            def _():
                store(g - 2).wait()

            xb = xbuf[g % 2].astype(BF16)
            gate = _dot(xb, wgb[...])
            hb = gate * _sigmoid(gate) * _dot(xb, wub[...])
            ybuf[g % 2] = _dot(hb.astype(BF16), wdb[...])
            store(g).start()
            return carry

        lax.fori_loop(0, nc, chunk, 0)

    @pl.when(e == pl.num_programs(0) - 1)
    def _():
        @pl.when(total >= 2)
        def _():
            store(total - 2).wait()

        @pl.when(total >= 1)
        def _():
            store(total - 1).wait()

        ybuf[0] = jnp.zeros(ybuf.shape[1:], F32)
        _zero_unused_tail(pstart, nchunks, ybuf.at[0], ys_hbm, sem_out.at[0])


def _moe_experts(xs, pstart, nchunks, w_gate, w_up, w_down):
    rows_pad, d = xs.shape
    de = w_gate.shape[-1]
    wspec = lambda shape: pl.BlockSpec(shape, lambda e, *_: (e, 0, 0))
    return pl.pallas_call(
        _moe_expert_kernel,
        grid_spec=pltpu.PrefetchScalarGridSpec(
            num_scalar_prefetch=2,
            grid=(N_EXPERTS,),
            in_specs=[pl.BlockSpec(memory_space=pl.ANY),
                      wspec((1, d, de)), wspec((1, d, de)), wspec((1, de, d))],
            out_specs=pl.BlockSpec(memory_space=pl.ANY),
            scratch_shapes=[pltpu.VMEM((2, MOE_CH, d), F32), pltpu.VMEM((2, MOE_CH, d), F32),
                            pltpu.VMEM((d, de), BF16), pltpu.VMEM((d, de), BF16), pltpu.VMEM((de, d), BF16),
                            pltpu.SemaphoreType.DMA((2,)), pltpu.SemaphoreType.DMA((2,))]),
        out_shape=jax.ShapeDtypeStruct((rows_pad, d), F32),
        compiler_params=_cparams(("arbitrary",)),
        name="moe_experts",
    )(pstart, nchunks, xs, w_gate, w_up, w_down)


def _moe_plan(ids, cnt, tm):
    n = ids.shape[0]
    tile_cnt = cnt[::SUBLANES, :N_EXPERTS]
    counts = jnp.sum(tile_cnt, axis=0)
    padded = ((counts + MOE_CH - 1) // MOE_CH) * MOE_CH
    pstart = jnp.cumsum(padded) - padded
    tile_base = pstart[None, :] + jnp.cumsum(tile_cnt, axis=0) - tile_cnt
    eid = ids[:, 0:TOP_K]
    rank = ids[:, TOP_K:2 * TOP_K]
    base_of_tok = jnp.repeat(tile_base, tm, axis=0)[:, None, :]
    chosen = eid[:, :, None] == jnp.arange(N_EXPERTS, dtype=I32)[None, None, :]
    dest = jnp.sum(jnp.where(chosen, base_of_tok, 0), axis=-1) + rank
    return dest.reshape(-1).astype(I32), pstart.astype(I32), (padded // MOE_CH).astype(I32)


FINAL_TM = 256


def _final_kernel(alpha, dest, x1_ref, wts_ref, p_ref, wpg_ref, wple_ref, l2w_ref, l2b_ref, ys_hbm, o_ref,
                  ybuf, sem):
    i = pl.program_id(0)
    tm = x1_ref.shape[0]
    slot = i % 2

    def row(tile, buf, r, j):
        return pltpu.make_async_copy(ys_hbm.at[pl.ds(dest[(tile * tm + r) * TOP_K + j], 1)],
                                     ybuf.at[buf, j, pl.ds(r, 1)], sem.at[buf])

    def gather(tile, buf):
        _each(tm, _each_choice(lambda r, j: row(tile, buf, r, j).start(priority=j)), ROW_DMA_UNROLL)

    @pl.when(i == 0)
    def _():
        gather(0, 0)

    @pl.when(i + 1 < pl.num_programs(0))
    def _():
        gather(i + 1, 1 - slot)

    x1 = x1_ref[...]
    ple = _sigmoid(_dot(x1.astype(BF16), wpg_ref[...])) * _dot(p_ref[...], wple_ref[...])
    _each(tm, _each_choice(lambda r, j: row(i, slot, r, j).wait()), ROW_DMA_UNROLL)
    moe = ybuf[slot, 0] * wts_ref[:, 0:1] + ybuf[slot, 1] * wts_ref[:, 1:2]
    o_ref[...] = _layer_norm(alpha * x1 + moe + ple, l2w_ref[...], l2b_ref[...])


def _final(alpha, dest, x1, ys, wts, p_bf, w_pg, w_ple, ln2_w, ln2_b):
    n, d = x1.shape
    tm = FINAL_TM
    tile = lambda w_: pl.BlockSpec((tm, w_), lambda i, *_: (i, 0))
    const = lambda shape: pl.BlockSpec(shape, lambda i, *_: (0, 0))
    return pl.pallas_call(
        functools.partial(_final_kernel, alpha),
        grid_spec=pltpu.PrefetchScalarGridSpec(
            num_scalar_prefetch=1,
            grid=(n // tm,),
            in_specs=[tile(d), tile(LANES), tile(p_bf.shape[1]),
                      const((d, d)), const((p_bf.shape[1], d)), const((1, d)), const((1, d)),
                      pl.BlockSpec(memory_space=pl.ANY)],
            out_specs=tile(d),
            scratch_shapes=[pltpu.VMEM((2, TOP_K, tm, d), F32), pltpu.SemaphoreType.DMA((2,))]),
        out_shape=jax.ShapeDtypeStruct((n, d), F32),
        compiler_params=_cparams(("arbitrary",)),
        name="final_ln2",
    )(dest, x1, wts, p_bf, w_pg.astype(BF16), w_ple.astype(BF16), ln2_w[None, :], ln2_b[None, :], ys)


def _regroup_w_in(w):
    m0 = RWKV_COLS
    g0 = RWKV_COLS + MLSTM_COLS
    mqk = 2 * M_HEADS * M_QK
    pad = lambda c: jnp.zeros((w.shape[0], c), w.dtype)
    parts = [w[:, 0:3 * R_WIDTH],
             w[:, m0:m0 + mqk],
             w[:, m0 + mqk:m0 + mqk + M_WIDTH],
             w[:, m0 + mqk + M_WIDTH + 2 * M_HEADS:m0 + MLSTM_COLS],
             w[:, g0:],
             w[:, 3 * R_WIDTH:RWKV_COLS], pad(LORA_PAD - LORA_COLS),
             w[:, m0 + mqk + M_WIDTH:m0 + mqk + M_WIDTH + 2 * M_HEADS], pad(LANES - 2 * M_HEADS)]
    out = jnp.concatenate(parts, axis=1)
    assert out.shape[1] == C_TOTAL
    return out


def kernel(x, p, w_in, mu_shift, w0, w_w2, a0, w_a2, w_g2, k_k, k_a, r_k, lnx_w, lnx_b, conv_w, conv_b,
           i_bias, f_bias, mh_w, b_gate, w_br, w_bm, w_out, ln1_w, ln1_b, w_rg, b_rg, w_re, b_re,
           w_gate, w_up, w_down, w_pg, w_ple, ln2_w, ln2_b):
    bsz, seq, d = x.shape
    depth = w_in.shape[0]
    assert bsz * R_HEADS * 2 == LANES and seq % PROJ_TM == 0 and seq % CHUNK == 0
    alpha = (2 * depth) ** 0.25
    n = bsz * seq
    xf = x.reshape(n, d)
    for i in range(depth):
        u = _proj_in(xf, _regroup_w_in(w_in[i].astype(BF16)))
        r, dec, kp, v, kn, beta, g, bonus = _rwkv_prep(
            u, seq, mu_shift[i], w0[i], w_w2[i], a0[i], w_a2[i], w_g2[i], k_k[i], k_a[i], r_k[i])
        ys = _rwkv_scan(kn, dec, kp, beta, r, v, bsz, seq)
        ym = _mlstm(u, bsz, seq, conv_w[i], conv_b[i], i_bias[i], f_bias[i], mh_w[i])
        x1, ids, wts, cnt = _merge(alpha, ys, bonus, g, ym, u, xf, lnx_w[i], lnx_b[i], b_gate[i], w_br[i],
                                   w_bm[i], w_out[i], ln1_w[i], ln1_b[i], w_rg[i], b_rg[i], w_re[i], b_re[i])
        dest, pstart, nchunks = _moe_plan(ids, cnt, MERGE_TM)
        rows_pad = TOP_K * n + N_EXPERTS * MOE_CH
        xs = _moe_dispatch(x1, dest, pstart, nchunks, rows_pad)
        ys_sorted = _moe_experts(xs, pstart, nchunks, w_gate[i], w_up[i], w_down[i])
        xf = _final(alpha, dest, x1, ys_sorted, wts, p[i].reshape(n, -1).astype(BF16), w_pg[i], w_ple[i],
                    ln2_w[i], ln2_b[i])
    return xf.reshape(bsz, seq, d)
```

```python
import functools

import jax
import jax.numpy as jnp
from jax import lax
from jax.experimental import pallas as pl
from jax.experimental.pallas import tpu as pltpu

F32 = jnp.float32
BF16 = jnp.bfloat16
I32 = jnp.int32

R_HEADS, R_HEAD = 16, 64
R_WIDTH = R_HEADS * R_HEAD
DECAY_LORA, AAA_LORA, GATE_LORA = 64, 64, 160
LORA_COLS = DECAY_LORA + AAA_LORA + GATE_LORA
LORA_PAD = 512
R_GN_EPS = 64e-5
RWKV_COLS = 3 * R_WIDTH + LORA_COLS
M_HEADS, M_QK, M_V = 8, 64, 128
M_WIDTH = M_HEADS * M_V
CONV_K = 4
CHUNK = 128
M_NORM_EPS = 1e-6
MLSTM_COLS = 2 * M_HEADS * M_QK + 2 * M_WIDTH + 2 * M_HEADS
N_GROUPS, EXPERTS_PER_GROUP = 4, 8
N_EXPERTS = N_GROUPS * EXPERTS_PER_GROUP
TOP_K = 2
MOE_BLOCK = 128
LN_EPS = 1e-5

LANES = 128
SUBLANES = 8
MXU_DIM = 256
V7X_VMEM_BYTES = 64 * 1024 * 1024
VMEM_LIMIT = 56 * 1024 * 1024
SCAN_VMEM_LIMIT = 60 * 1024 * 1024

C_RKV = 0
C_MQK = 3072
C_MV = 4096
C_MO = 5120
C_GR = 6144
C_GM = 8192
C_LORA = 10240
C_MG = 10752
C_TOTAL = 10880
PROJ_TN = 2176
PROJ_TM = 512


def _cparams(sem, vmem=VMEM_LIMIT):
    return pltpu.CompilerParams(dimension_semantics=sem, vmem_limit_bytes=vmem)


def _sigmoid(x):
    return 1.0 / (1.0 + jnp.exp(-x))


def _softplus(x):
    return jnp.maximum(x, 0.0) + jnp.log1p(jnp.exp(-jnp.abs(x)))


def _split3(x):
    hi = x.astype(BF16)
    r1 = x - hi.astype(F32)
    mid = r1.astype(BF16)
    lo = (r1 - mid.astype(F32)).astype(BF16)
    return hi, mid, lo


def _dot(a, b):
    return jnp.dot(a, b, preferred_element_type=F32)


def _dot_nt(a, b):
    return lax.dot_general(a, b, (((1,), (1,)), ((), ())), preferred_element_type=F32)


def _dot_tn(a, b):
    return lax.dot_general(a, b, (((0,), (0,)), ((), ())), preferred_element_type=F32)


def _dot_exact_rhs(x, ones_bf16, terms=3):
    parts = _split3(x)[:terms]
    acc = _dot(parts[0], ones_bf16)
    for part in parts[1:]:
        acc = acc + _dot(part, ones_bf16)
    return acc


def _block_ones(n, group):
    r = lax.broadcasted_iota(I32, (n, n), 0) // group
    c = lax.broadcasted_iota(I32, (n, n), 1) // group
    return jnp.where(r == c, 1.0, 0.0).astype(BF16)


def _seg_sum(x, group):
    ones = _block_ones(MXU_DIM, group)
    slabs = [_dot_exact_rhs(x[:, p * MXU_DIM:(p + 1) * MXU_DIM], ones, terms=2)
             for p in range(x.shape[1] // MXU_DIM)]
    return jnp.concatenate(slabs, axis=1)


def _layer_norm(x, w, b):
    mu = jnp.mean(x, axis=-1, keepdims=True)
    xc = x - mu
    var = jnp.mean(xc * xc, axis=-1, keepdims=True)
    return xc * lax.rsqrt(var + LN_EPS) * w + b


def _proj_kernel(x_ref, w_ref, o_ref):
    o_ref[...] = _dot(x_ref[...].astype(BF16), w_ref[...])


def _proj_in(x_bf, w_bf):
    m, k = x_bf.shape
    n = w_bf.shape[1]
    return pl.pallas_call(
        _proj_kernel,
        grid=(n // PROJ_TN, m // PROJ_TM),
        in_specs=[pl.BlockSpec((PROJ_TM, k), lambda j, i: (i, 0)),
                  pl.BlockSpec((k, PROJ_TN), lambda j, i: (0, j))],
        out_specs=pl.BlockSpec((PROJ_TM, PROJ_TN), lambda j, i: (i, j)),
        out_shape=jax.ShapeDtypeStruct((m, n), F32),
        compiler_params=_cparams(("parallel", "parallel")),
        name="proj_in",
    )(x_bf, w_bf)


PREP_TM = 256


def _rwkv_prep_kernel(seq, u_ref, l_ref, up_ref, lp_ref, mu_ref, mul_ref, w0_ref, a0_ref, kk_ref, ka_ref,
                      rk_ref, ww_ref, wa_ref, wg_ref,
                      r_o, dec_o, kp_o, v_o, kn_o, beta_o, g_o, bonus_o):
    i = pl.program_id(0)
    tm = u_ref.shape[0]
    first = (i * tm) % seq == 0
    row = lax.broadcasted_iota(I32, (tm, 1), 0)

    def shift(u, prev8):
        prev_row = jnp.where(first, 0.0, prev8[SUBLANES - 1:SUBLANES, :])
        return jnp.where(row == 0, prev_row, pltpu.roll(u, 1, 0))

    u = u_ref[...]
    z = u + mu_ref[...] * (shift(u, up_ref[...]) - u)
    lo = l_ref[...]
    zl = lo + mul_ref[...] * (shift(lo, lp_ref[...]) - lo)

    r = z[:, 0:R_WIDTH]
    k = z[:, R_WIDTH:2 * R_WIDTH]
    v = z[:, 2 * R_WIDTH:3 * R_WIDTH]
    w_pre = w0_ref[...] + _dot(jnp.tanh(zl).astype(BF16), ww_ref[...])
    w = -_softplus(-w_pre) - 0.5
    dec = jnp.exp(-jnp.exp(w))
    a = _sigmoid(a0_ref[...] + _dot(zl.astype(BF16), wa_ref[...]))
    g = _dot(_sigmoid(zl).astype(BF16), wg_ref[...])

    kk = k * kk_ref[...]
    nrm = jnp.sqrt(_seg_sum(kk * kk, R_HEAD))
    kn = kk / jnp.maximum(nrm, 1e-12)
    kp = k * (1.0 + (a - 1.0) * ka_ref[...])
    bonus = _seg_sum(r * kp * rk_ref[...], R_HEAD) * v

    r_o[...] = r
    dec_o[...] = dec
    kp_o[...] = kp
    v_o[...] = v
    kn_o[...] = kn
    beta_o[...] = kn * a
    g_o[...] = g
    bonus_o[...] = bonus


def _rwkv_prep(u, seq, mu, w0, w_w2, a0, w_a2, w_g2, k_k, k_a, r_k):
    n = u.shape[0]
    tm = PREP_TM
    mu_rkv = mu[None, :3 * R_WIDTH]
    mu_lora = jnp.zeros((1, LORA_PAD), F32).at[0, :LORA_COLS].set(mu[3 * R_WIDTH:])
    ww = jnp.zeros((LORA_PAD, R_WIDTH), BF16).at[0:DECAY_LORA].set(w_w2.astype(BF16))
    wa = jnp.zeros((LORA_PAD, R_WIDTH), BF16).at[DECAY_LORA:DECAY_LORA + AAA_LORA].set(w_a2.astype(BF16))
    wg = jnp.zeros((LORA_PAD, R_WIDTH), BF16).at[DECAY_LORA + AAA_LORA:LORA_COLS].set(w_g2.astype(BF16))
    row = lambda t: t.reshape(1, R_WIDTH)
    prev_blk = lambda i: jnp.maximum(i * (tm // SUBLANES) - 1, 0)
    const = lambda shape: pl.BlockSpec(shape, lambda i: (0, 0))
    nat = jax.ShapeDtypeStruct((n, R_WIDTH), F32)
    return pl.pallas_call(
        functools.partial(_rwkv_prep_kernel, seq),
        grid=(n // tm,),
        in_specs=[pl.BlockSpec((tm, 3 * R_WIDTH), lambda i: (i, C_RKV // (3 * R_WIDTH))),
                  pl.BlockSpec((tm, LORA_PAD), lambda i: (i, C_LORA // LORA_PAD)),
                  pl.BlockSpec((SUBLANES, 3 * R_WIDTH), lambda i: (prev_blk(i), C_RKV // (3 * R_WIDTH))),
                  pl.BlockSpec((SUBLANES, LORA_PAD), lambda i: (prev_blk(i), C_LORA // LORA_PAD)),
                  const((1, 3 * R_WIDTH)), const((1, LORA_PAD)),
                  const((1, R_WIDTH)), const((1, R_WIDTH)), const((1, R_WIDTH)), const((1, R_WIDTH)),
                  const((1, R_WIDTH)),
                  const((LORA_PAD, R_WIDTH)), const((LORA_PAD, R_WIDTH)), const((LORA_PAD, R_WIDTH))],
        out_specs=[pl.BlockSpec((tm, R_WIDTH), lambda i: (i, 0))] * 8,
        out_shape=[nat] * 8,
        compiler_params=_cparams(("parallel",)),
        name="rwkv_prep",
    )(u, u, u, u, mu_rkv, mu_lora, row(w0), row(a0), row(k_k), row(k_a), row(r_k), ww, wa, wg)


SCAN_TB = LANES // 2
SCAN_OUT = 2
SCAN_V = R_HEAD // 2
SCAN_ACC = 1
SCAN_KOPS = 5
SCAN_PAIRS = ((0, 1), (2, 3), (4, 5))
SCAN_HP = R_WIDTH // LANES
SCAN_STEPS1 = 2
SCAN_STEPS2 = 7
SCAN_KEYS_PER_TRIP = 8
SCAN_TRIPS2 = R_HEAD // SCAN_KEYS_PER_TRIP
Z_PITCH = R_HEAD + SUBLANES
V_PITCH = SCAN_V + SUBLANES
RELAYOUT_UNROLL = 8


def _rwkv_scan_kernel(kn_hbm, dec_hbm, kp_hbm, beta_hbm, r_hbm, v_hbm, y_ref,
                      s_ref, stage, z_ref, xs_ref, vs_ref, ys_ref, sem):
    i = pl.program_id(0)
    nblk = pl.num_programs(0)
    nb = stage.shape[2]
    tb = SCAN_TB
    slab = R_HEADS * Z_PITCH
    zrows = nb * slab
    cur = i % 2
    srcs = (kn_hbm, dec_hbm, kp_hbm, beta_hbm, r_hbm, v_hbm)

    def fetch(op, blk, par):
        return pltpu.make_async_copy(srcs[op].at[:, pl.ds(blk * tb, tb), :], stage.at[par, op], sem.at[par])

    def fetch_start(blk, par):
        for op in range(len(srcs)):
            fetch(op, blk, par).start()

    def fetch_wait(blk, par):
        for op in range(len(srcs)):
            fetch(op, blk, par).wait()

    def to_time_on_lanes(par, b):
        for p, (oa, ob) in enumerate(SCAN_PAIRS):
            for hp in range(SCAN_HP):
                cols = slice(hp * LANES, (hp + 1) * LANES)
                tt = jnp.concatenate([stage[par, oa, b, :, cols], stage[par, ob, b, :, cols]], axis=0).T
                for hh in range(2):
                    row0 = pl.multiple_of(p * zrows + (b * R_HEADS + 2 * hp + hh) * Z_PITCH, SUBLANES)
                    z_ref[pl.ds(row0, R_HEAD), :] = tt[hh * R_HEAD:(hh + 1) * R_HEAD]

    def head_rows(p, b, c):
        return z_ref[pl.ds(p * zrows + b * slab + c, R_HEADS, stride=Z_PITCH), :]

    def key_unit(buf, p, k):
        oa, ob = SCAN_PAIRS[p]
        rows = [head_rows(p, b, k) for b in range(nb)]
        tt = jnp.concatenate(rows + rows, axis=0).T
        xs_ref[buf, oa, k] = tt[0:tb]
        if ob < SCAN_KOPS:
            xs_ref[buf, ob, k] = tt[tb:2 * tb]

    def value_unit(buf, vp):
        rows = [head_rows(len(SCAN_PAIRS) - 1, b, vh * SCAN_V + vp) for vh in range(2) for b in range(nb)]
        tt = jnp.concatenate(rows, axis=0).T
        vs_ref[pl.ds(buf * tb * V_PITCH + vp, tb, stride=V_PITCH), :] = tt[tb:2 * tb]

    def chain_units(buf, q):
        units = [functools.partial(key_unit, buf, p, SCAN_KEYS_PER_TRIP * q + kk)
                 for kk in range(SCAN_KEYS_PER_TRIP) for p in range(len(SCAN_PAIRS))]
        units += [functools.partial(value_unit, buf, (SCAN_KEYS_PER_TRIP // 2) * q + vv)
                  for vv in range(SCAN_KEYS_PER_TRIP // 2)]
        return units

    @pl.when(i == 0)
    def _():
        s_ref[...] = jnp.zeros(s_ref.shape, F32)
        fetch_start(0, 0)
        fetch_wait(0, 0)

        @pl.when(nblk > 1)
        def _():
            fetch_start(1, 1)

        _each(nb, lambda b: to_time_on_lanes(0, b))
        _each(SCAN_TRIPS2, lambda q: [unit() for unit in chain_units(0, q)])

    @pl.when(i + 1 < nblk)
    def _():
        fetch_wait(i + 1, 1 - cur)

    @pl.when(i + 2 < nblk)
    def _():
        fetch_start(i + 2, cur)

    def total(acc):
        while len(acc) > 1:
            acc = [acc[j] + acc[j + 1] for j in range(0, len(acc), 2)]
        return acc[0]

    def add_term(acc, k, term):
        acc[k % SCAN_ACC] = term if acc[k % SCAN_ACC] is None else acc[k % SCAN_ACC] + term

    def key_row(op, k, t):
        return xs_ref[cur, op, k, pl.ds(t, 1), :]

    acc = [None] * SCAN_ACC
    for k in range(R_HEAD):
        add_term(acc, k, s_ref[k] * xs_ref[cur, 0, k, 0:1, :])

    def step(t, s_kk):
        v_t = vs_ref[pl.ds(pl.multiple_of((cur * tb + t) * V_PITCH, SUBLANES), SCAN_V), :]
        t_next = jnp.minimum(t + 1, tb - 1)
        acc_y = [None] * SCAN_ACC
        acc_s = [None] * SCAN_ACC
        for k in range(R_HEAD):
            s_new = (s_ref[k] * key_row(1, k, t)
                     + (v_t * key_row(2, k, t) - s_kk * key_row(3, k, t)))
            s_ref[k] = s_new
            add_term(acc_y, k, s_new * key_row(4, k, t))
            add_term(acc_s, k, s_new * key_row(0, k, t_next))
        out_row = pl.multiple_of(((i % SCAN_OUT) * tb + t) * V_PITCH, SUBLANES)
        ys_ref[pl.ds(out_row, SCAN_V), :] = total(acc_y)
        return total(acc_s)

    def first_part(b, s_kk):
        for j in range(SCAN_STEPS1):
            s_kk = step(b * SCAN_STEPS1 + j, s_kk)
        to_time_on_lanes(1 - cur, b)
        return s_kk

    def second_part(q, s_kk):
        for j in range(SCAN_STEPS2):
            s_kk = step(nb * SCAN_STEPS1 + q * SCAN_STEPS2 + j, s_kk)
        for unit in chain_units(1 - cur, q):
            unit()
        return s_kk

    s_kk = lax.fori_loop(0, nb, first_part, total(acc))
    lax.fori_loop(0, SCAN_TRIPS2, second_part, s_kk)

    @pl.when(i % SCAN_OUT == SCAN_OUT - 1)
    def _():
        steps = SCAN_OUT * tb

        def out_rows(vp, carry):
            tt = ys_ref[pl.ds(vp, steps, stride=V_PITCH), :].T
            for vh in range(2):
                for b in range(nb):
                    g = vh * nb + b
                    z_ref[pl.ds(b * slab + vh * SCAN_V + vp, R_HEADS, stride=Z_PITCH), :] = \
                        tt[g * R_HEADS:(g + 1) * R_HEADS]
            return carry
        lax.fori_loop(0, SCAN_V, out_rows, 0, unroll=RELAYOUT_UNROLL)
        for b in range(nb):
            for hp in range(SCAN_HP):
                pair = [z_ref[pl.ds((b * R_HEADS + 2 * hp + hh) * Z_PITCH, R_HEAD), :] for hh in range(2)]
                y_ref[b, :, hp * LANES:(hp + 1) * LANES] = jnp.concatenate(pair, axis=0).T


def _rwkv_scan(kn, dec, kp, beta, r, v, bsz, seq):
    shape3 = (bsz, seq, R_WIDTH)
    ops = [a.reshape(shape3) for a in (kn, dec, kp, beta, r, v)]
    assert SCAN_TB == bsz * SCAN_STEPS1 + SCAN_TRIPS2 * SCAN_STEPS2 and seq % (SCAN_OUT * SCAN_TB) == 0
    y = pl.pallas_call(
        _rwkv_scan_kernel,
        grid=(seq // SCAN_TB,),
        in_specs=[pl.BlockSpec(memory_space=pl.ANY)] * len(ops),
        out_specs=pl.BlockSpec((bsz, SCAN_OUT * SCAN_TB, R_WIDTH), lambda i: (0, i // SCAN_OUT, 0)),
        out_shape=jax.ShapeDtypeStruct(shape3, F32),
        scratch_shapes=[pltpu.VMEM((R_HEAD, SCAN_V, LANES), F32),
                        pltpu.VMEM((2, len(ops), bsz, SCAN_TB, R_WIDTH), F32),
                        pltpu.VMEM((len(SCAN_PAIRS) * bsz * R_HEADS * Z_PITCH, LANES), F32),
                        pltpu.VMEM((2, SCAN_KOPS, R_HEAD, SCAN_TB, LANES), F32),
                        pltpu.VMEM((2 * SCAN_TB * V_PITCH, LANES), F32),
                        pltpu.VMEM((SCAN_OUT * SCAN_TB * V_PITCH, LANES), F32),
                        pltpu.SemaphoreType.DMA((2,))],
        compiler_params=_cparams(("arbitrary",), vmem=SCAN_VMEM_LIMIT),
        name="rwkv_scan",
    )(*ops)
    return y.reshape(bsz * seq, R_WIDTH)


M_AUG = 2 * M_V


def _mlstm_kernel(qk_ref, v_ref, o_ref, g_ref, cw_ref, cb_ref, gb_ref, mh_ref, y_ref,
                  ext_ref, c_ref, m_ref):
    c_idx = pl.program_id(0)
    L = CHUNK
    nb = qk_ref.shape[0]

    @pl.when(c_idx == 0)
    def _():
        ext_ref[:, 0:SUBLANES, :] = jnp.zeros((nb, SUBLANES, ext_ref.shape[2]), F32)
        c_ref[...] = jnp.zeros(c_ref.shape, F32)
        m_ref[...] = jnp.full(m_ref.shape, -jnp.inf, F32)

    lane = lax.broadcasted_iota(I32, (L, LANES), 1)
    ti = lax.broadcasted_iota(I32, (L, L), 0)
    si = lax.broadcasted_iota(I32, (L, L), 1)
    causal = si <= ti
    ltri = jnp.where(causal, 1.0, 0.0).astype(BF16)
    utri = jnp.where(ti <= si, 1.0, 0.0).astype(BF16)
    ones_col = jnp.where(lane == 0, 1.0, 0.0).astype(BF16)

    q_all, k_all, gcols, grows, acols, arows = [], [], [], [], [], []
    for b in range(nb):
        ext_ref[b, SUBLANES:SUBLANES + L, :] = qk_ref[b]
        conv = cb_ref[...]
        for j in range(CONV_K):
            off = SUBLANES - (CONV_K - 1) + j
            conv = conv + cw_ref[j:j + 1, :] * ext_ref[b, off:off + L, :]
        ext_ref[b, 0:SUBLANES, :] = ext_ref[b, L:L + SUBLANES, :]
        qk = conv * _sigmoid(conv)
        q_all.append(qk[:, :M_HEADS * M_QK].astype(BF16))
        k_all.append(qk[:, M_HEADS * M_QK:] * (M_QK ** -0.5))
        gpre = g_ref[b] + gb_ref[...]
        gcols.append(jnp.where(lane < M_HEADS, gpre, -_softplus(-gpre)))
        grows.append(gcols[b].T)
        acols.append(_dot_exact_rhs_left(ltri, gcols[b]))
        arows.append(_dot_exact_rhs(grows[b], utri))

    chains = [(b, h) for b in range(nb) for h in range(M_HEADS)]
    every = range(len(chains))
    a_col = [acols[b][:, M_HEADS + h:M_HEADS + h + 1] for b, h in chains]
    i_col = [gcols[b][:, h:h + 1] for b, h in chains]
    m_st = [m_ref[u:u + 1, 0:1] for u in every]
    d = [jnp.where(causal, a_col[u] - arows[b][M_HEADS + h:M_HEADS + h + 1, :] + grows[b][h:h + 1, :], -jnp.inf)
         for u, (b, h) in enumerate(chains)]
    dmax = [jnp.max(d[u], axis=-1, keepdims=True) for u in every]
    inter = [a_col[u] + m_st[u] for u in every]
    m_t = [jnp.maximum(inter[u], dmax[u]) for u in every]
    q = [q_all[b][:, h * M_QK:(h + 1) * M_QK] for b, h in chains]
    k = [k_all[b][:, h * M_QK:(h + 1) * M_QK] for b, h in chains]
    v_aug = [jnp.concatenate([v_ref[b, :, h * M_V:(h + 1) * M_V].astype(BF16), ones_col], axis=1)
             for b, h in chains]
    s = [_dot_nt(q[u], k[u].astype(BF16)) * jnp.exp(d[u] - m_t[u]) for u in every]
    ie = [jnp.exp(inter[u] - m_t[u]) for u in every]
    tot = [ie[u] * _dot(q[u], c_ref[u].astype(BF16)) + _dot(s[u].astype(BF16), v_aug[u]) for u in every]
    h_c = [tot[u][:, :M_V] / jnp.maximum(jnp.abs(tot[u][:, M_V:M_V + 1]), jnp.exp(-m_t[u])) for u in every]

    a_tot = [a_col[u][L - 1:L, :] for u in every]
    gl = [a_tot[u] - a_col[u] + i_col[u] for u in every]
    m_new = [jnp.maximum(a_tot[u] + m_st[u], jnp.max(gl[u], axis=0, keepdims=True)) for u in every]
    kg = [(k[u] * jnp.exp(gl[u] - m_new[u])).astype(BF16) for u in every]
    for u in every:
        c_ref[u] = jnp.exp(a_tot[u] + m_st[u] - m_new[u]) * c_ref[u] + _dot_tn(kg[u], v_aug[u])
        m_ref[u:u + 1, :] = jnp.broadcast_to(m_new[u], (1, LANES))

    mu = [jnp.mean(h_c[u], axis=-1, keepdims=True) for u in every]
    hc = [h_c[u] - mu[u] for u in every]
    var = [jnp.mean(hc[u] * hc[u], axis=-1, keepdims=True) for u in every]
    for u, (b, h) in enumerate(chains):
        hn = hc[u] * lax.rsqrt(var[u] + M_NORM_EPS) * mh_ref[:, h * M_V:(h + 1) * M_V]
        y_ref[b, :, h * M_V:(h + 1) * M_V] = _sigmoid(o_ref[b, :, h * M_V:(h + 1) * M_V]) * hn


def _dot_exact_rhs_left(ones_bf16, x):
    hi, mid, lo = _split3(x)
    return _dot(ones_bf16, hi) + _dot(ones_bf16, mid) + _dot(ones_bf16, lo)


def _mlstm(u, bsz, seq, conv_w, conv_b, i_bias, f_bias, mh_w):
    nc = seq // CHUNK
    w = M_WIDTH
    gbias = jnp.zeros((1, LANES), F32).at[0, :M_HEADS].set(i_bias).at[0, M_HEADS:2 * M_HEADS].set(f_bias)
    u3 = u.reshape(bsz, seq, u.shape[1])
    blk = lambda col: pl.BlockSpec((bsz, CHUNK, w), lambda c: (0, c, col // w))
    const = lambda shape: pl.BlockSpec(shape, lambda c: (0, 0))
    y = pl.pallas_call(
        _mlstm_kernel,
        grid=(nc,),
        in_specs=[blk(C_MQK), blk(C_MV), blk(C_MO),
                  pl.BlockSpec((bsz, CHUNK, LANES), lambda c: (0, c, C_MG // LANES)),
                  const((CONV_K, w)), const((1, w)), const((1, LANES)), const((1, w))],
        out_specs=pl.BlockSpec((bsz, CHUNK, w), lambda c: (0, c, 0)),
        out_shape=jax.ShapeDtypeStruct((bsz, seq, w), F32),
        scratch_shapes=[pltpu.VMEM((bsz, CHUNK + SUBLANES, w), F32),
                        pltpu.VMEM((bsz * M_HEADS, M_QK, M_AUG), F32),
                        pltpu.VMEM((bsz * M_HEADS, LANES), F32)],
        compiler_params=_cparams(("arbitrary",)),
        name="mlstm",
    )(u3, u3, u3, u3, conv_w, conv_b[None, :], gbias, mh_w[None, :])
    return y.reshape(bsz * seq, w)


MERGE_TM = 256
C_ROUTE_G = N_EXPERTS


def _merge_kernel(alpha, ys_ref, bonus_ref, g_ref, ym_ref, gr_ref, gm_ref, x_ref,
                  lnxw_ref, lnxb_ref, bgr_ref, bgm_ref, wbr_ref, wbm_ref, wout_ref, l1w_ref, l1b_ref,
                  wrh_ref, wrl_ref, br_ref, x1_o, ids_o, wts_o, cnt_o):
    ys = ys_ref[...]
    mu = _seg_sum(ys, R_HEAD) * (1.0 / R_HEAD)
    yc = ys - mu
    var = _seg_sum(yc * yc, R_HEAD) * (1.0 / R_HEAD)
    y = yc * lax.rsqrt(var + R_GN_EPS) * lnxw_ref[...] + lnxb_ref[...]
    y_r = (y + bonus_ref[...]) * g_ref[...]
    br = _dot(y_r.astype(BF16), wbr_ref[...])
    bm = _dot(ym_ref[...].astype(BF16), wbm_ref[...])
    mix_in = _sigmoid(gr_ref[...] + bgr_ref[...]) * br + _sigmoid(gm_ref[...] + bgm_ref[...]) * bm
    mix = _dot(mix_in.astype(BF16), wout_ref[...])
    x1 = _layer_norm(alpha * x_ref[...] + mix, l1w_ref[...], l1b_ref[...])
    x1_o[...] = x1

    xh = x1.astype(BF16)
    xl = (x1 - xh.astype(F32)).astype(BF16)
    logits = (_dot(xh, wrh_ref[...]) + (_dot(xh, wrl_ref[...]) + _dot(xl, wrh_ref[...]))) + br_ref[...]
    tm = logits.shape[0]
    lane_i = lax.broadcasted_iota(I32, (tm, LANES), 1)
    lane = lane_i.astype(F32)
    group_of_lane = (lane_i // EXPERTS_PER_GROUP).astype(F32)
    big = float(LANES)
    neg = -jnp.inf
    lg = jnp.where((lane_i >= C_ROUTE_G) & (lane_i < C_ROUTE_G + N_GROUPS), logits, neg)
    gmax = jnp.max(lg, axis=-1, keepdims=True)
    gsel = jnp.min(jnp.where(lg == gmax, lane - C_ROUTE_G, big), axis=-1, keepdims=True)
    g_w = 1.0 / jnp.sum(jnp.exp(lg - gmax), axis=-1, keepdims=True)
    le = jnp.where((lane_i < N_EXPERTS) & (group_of_lane == gsel), logits, neg)
    m1 = jnp.max(le, axis=-1, keepdims=True)
    i1 = jnp.min(jnp.where(le == m1, lane, big), axis=-1, keepdims=True)
    le2 = jnp.where(lane == i1, neg, le)
    m2 = jnp.max(le2, axis=-1, keepdims=True)
    i2 = jnp.min(jnp.where(le2 == m2, lane, big), axis=-1, keepdims=True)
    e2 = jnp.exp(m2 - m1)
    w1 = g_w / (1.0 + e2)
    w2 = g_w * e2 / (1.0 + e2)
    wts_o[...] = jnp.where(lane_i == 0, w1, jnp.where(lane_i == 1, w2, 0.0))

    oh1 = jnp.where(lane == i1, 1.0, 0.0)
    oh2 = jnp.where(lane == i2, 1.0, 0.0)
    ri = lax.broadcasted_iota(I32, (tm, tm), 0)
    ci = lax.broadcasted_iota(I32, (tm, tm), 1)
    lstrict = jnp.where(ci < ri, 1.0, 0.0).astype(BF16)
    tot1 = jnp.sum(oh1, axis=0, keepdims=True)
    tot2 = jnp.sum(oh2, axis=0, keepdims=True)
    rank1 = jnp.sum(_dot(lstrict, oh1.astype(BF16)) * oh1, axis=-1, keepdims=True)
    rank2 = jnp.sum((_dot(lstrict, oh2.astype(BF16)) + tot1) * oh2, axis=-1, keepdims=True)
    ids = jnp.where(lane_i == 0, i1, jnp.where(lane_i == 1, i2, jnp.where(lane_i == 2, rank1,
                                                                          jnp.where(lane_i == 3, rank2, 0.0))))
    ids_o[...] = ids.astype(I32)
    cnt_o[...] = jnp.broadcast_to(tot1 + tot2, cnt_o.shape).astype(I32)


def _merge(alpha, ys, bonus, g, ym, u, x, lnx_w, lnx_b, b_gate, w_br, w_bm, w_out, ln1_w, ln1_b,
           w_rg, b_rg, w_re, b_re):
    n, d = x.shape
    tm = MERGE_TM
    wr = jnp.zeros((d, LANES), F32).at[:, :N_EXPERTS].set(w_re).at[:, C_ROUTE_G:C_ROUTE_G + N_GROUPS].set(w_rg)
    wr_hi = wr.astype(BF16)
    wr_lo = (wr - wr_hi.astype(F32)).astype(BF16)
    b_r = jnp.zeros((1, LANES), F32).at[0, :N_EXPERTS].set(b_re).at[0, C_ROUTE_G:C_ROUTE_G + N_GROUPS].set(b_rg)
    tile = lambda w_: pl.BlockSpec((tm, w_), lambda i: (i, 0))
    const = lambda shape: pl.BlockSpec(shape, lambda i: (0, 0))
    return pl.pallas_call(
        functools.partial(_merge_kernel, alpha),
        grid=(n // tm,),
        in_specs=[tile(R_WIDTH), tile(R_WIDTH), tile(R_WIDTH), tile(M_WIDTH),
                  pl.BlockSpec((tm, d), lambda i: (i, C_GR // d)),
                  pl.BlockSpec((tm, d), lambda i: (i, C_GM // d)),
                  tile(d),
                  const((1, R_WIDTH)), const((1, R_WIDTH)), const((1, d)), const((1, d)),
                  const((R_WIDTH, d)), const((M_WIDTH, d)), const((d, d)), const((1, d)), const((1, d)),
                  const((d, LANES)), const((d, LANES)), const((1, LANES))],
        out_specs=[tile(d), tile(LANES), tile(LANES), pl.BlockSpec((SUBLANES, LANES), lambda i: (i, 0))],
        out_shape=[jax.ShapeDtypeStruct((n, d), F32), jax.ShapeDtypeStruct((n, LANES), I32),
                   jax.ShapeDtypeStruct((n, LANES), F32),
                   jax.ShapeDtypeStruct((n // tm * SUBLANES, LANES), I32)],
        compiler_params=_cparams(("parallel",)),
        name="merge_ln1_router",
    )(ys, bonus, g, ym, u, u, x, lnx_w[None, :], lnx_b[None, :], b_gate[None, :d], b_gate[None, d:],
      w_br.astype(BF16), w_bm.astype(BF16), w_out.astype(BF16), ln1_w[None, :], ln1_b[None, :],
      wr_hi, wr_lo, b_r)


MOE_CH = 256
DISPATCH_TM = 256


ROW_DMA_UNROLL = 8


def _each(count, fn, unroll=1):
    def body(r, carry):
        fn(r)
        return carry
    lax.fori_loop(0, count, body, 0, unroll=unroll)


def _each_choice(fn):
    def both(r):
        for j in range(TOP_K):
            fn(r, j)
    return both


def _zero_unused_tail(pstart, nchunks, zeros_ref, out_hbm, sem):
    ch = zeros_ref.shape[0]
    used = (pstart[N_EXPERTS - 1] + nchunks[N_EXPERTS - 1] * ch) // ch
    total = out_hbm.shape[0] // ch

    def chunk(c):
        return pltpu.make_async_copy(zeros_ref, out_hbm.at[pl.ds(pl.multiple_of(c * ch, ch), ch)], sem)

    def over_tail(fn):
        def body(c, carry):
            fn(c)
            return carry
        lax.fori_loop(used, total, body, 0)

    over_tail(lambda c: chunk(c).start())
    over_tail(lambda c: chunk(c).wait())


def _moe_dispatch_kernel(dest, pstart, nchunks, x_ref, xs_hbm, zbuf, sem_z, sem):
    i = pl.program_id(0)
    tm = x_ref.shape[0]
    ch = zbuf.shape[0]

    @pl.when(i == 0)
    def _():
        zbuf[...] = jnp.zeros(zbuf.shape, F32)

        def tail(e):
            row0 = pl.multiple_of(pstart[e] + (nchunks[e] - 1) * ch, ch)
            return pltpu.make_async_copy(zbuf, xs_hbm.at[pl.ds(row0, ch)], sem_z)

        def start(e):
            @pl.when(nchunks[e] > 0)
            def _():
                tail(e).start()

        def wait(e):
            @pl.when(nchunks[e] > 0)
            def _():
                tail(e).wait()

        _each(N_EXPERTS, start)
        _each(N_EXPERTS, wait)
        _zero_unused_tail(pstart, nchunks, zbuf, xs_hbm, sem_z)

    def row(r, j):
        return pltpu.make_async_copy(x_ref.at[pl.ds(r, 1)],
                                     xs_hbm.at[pl.ds(dest[(i * tm + r) * TOP_K + j], 1)], sem)

    _each(tm, _each_choice(lambda r, j: row(r, j).start(priority=j)), ROW_DMA_UNROLL)
    _each(tm, _each_choice(lambda r, j: row(r, j).wait()), ROW_DMA_UNROLL)


def _moe_dispatch(x1, dest, pstart, nchunks, rows_pad):
    n, d = x1.shape
    tm = DISPATCH_TM
    return pl.pallas_call(
        _moe_dispatch_kernel,
        grid_spec=pltpu.PrefetchScalarGridSpec(
            num_scalar_prefetch=3,
            grid=(n // tm,),
            in_specs=[pl.BlockSpec((tm, d), lambda i, *_: (i, 0))],
            out_specs=pl.BlockSpec(memory_space=pl.ANY),
            scratch_shapes=[pltpu.VMEM((MOE_CH, d), F32), pltpu.SemaphoreType.DMA, pltpu.SemaphoreType.DMA]),
        out_shape=jax.ShapeDtypeStruct((rows_pad, d), F32),
        compiler_params=_cparams(("arbitrary",)),
        name="moe_dispatch",
    )(dest, pstart, nchunks, x1)


def _moe_expert_kernel(pstart, nchunks, xs_hbm, wg_ref, wu_ref, wd_ref, ys_hbm,
                       xbuf, ybuf, wgb, wub, wdb, sem_in, sem_out):
    e = pl.program_id(0)
    ch = xbuf.shape[1]
    nc = nchunks[e]
    g0 = pstart[e] // ch
    total = (pstart[N_EXPERTS - 1] + nchunks[N_EXPERTS - 1] * ch) // ch

    def rows(g):
        return pl.ds(pl.multiple_of(g * ch, ch), ch)

    def load(g):
        return pltpu.make_async_copy(xs_hbm.at[rows(g)], xbuf.at[g % 2], sem_in.at[g % 2])

    def store(g):
        return pltpu.make_async_copy(ybuf.at[g % 2], ys_hbm.at[rows(g)], sem_out.at[g % 2])

    @pl.when(nc > 0)
    def _():
        @pl.when(g0 == 0)
        def _():
            load(0).start()

        wgb[...] = wg_ref[0].astype(BF16)
        wub[...] = wu_ref[0].astype(BF16)
        wdb[...] = wd_ref[0].astype(BF16)

        def chunk(c, carry):
            g = g0 + c
            load(g).wait()

            @pl.when(g + 1 < total)
            def _():
                load(g + 1).start()

            @pl.when(g >= 2)
            def _():
                store(g - 2).wait()

            xb = xbuf[g % 2].astype(BF16)
            gate = _dot(xb, wgb[...])
            hb = gate * _sigmoid(gate) * _dot(xb, wub[...])
            ybuf[g % 2] = _dot(hb.astype(BF16), wdb[...])
            store(g).start()
            return carry

        lax.fori_loop(0, nc, chunk, 0)

    @pl.when(e == pl.num_programs(0) - 1)
    def _():
        @pl.when(total >= 2)
        def _():
            store(total - 2).wait()

        @pl.when(total >= 1)
        def _():
            store(total - 1).wait()

        ybuf[0] = jnp.zeros(ybuf.shape[1:], F32)
        _zero_unused_tail(pstart, nchunks, ybuf.at[0], ys_hbm, sem_out.at[0])


def _moe_experts(xs, pstart, nchunks, w_gate, w_up, w_down):
    rows_pad, d = xs.shape
    de = w_gate.shape[-1]
    wspec = lambda shape: pl.BlockSpec(shape, lambda e, *_: (e, 0, 0))
    return pl.pallas_call(
        _moe_expert_kernel,
        grid_spec=pltpu.PrefetchScalarGridSpec(
            num_scalar_prefetch=2,
            grid=(N_EXPERTS,),
            in_specs=[pl.BlockSpec(memory_space=pl.ANY),
                      wspec((1, d, de)), wspec((1, d, de)), wspec((1, de, d))],
            out_specs=pl.BlockSpec(memory_space=pl.ANY),
            scratch_shapes=[pltpu.VMEM((2, MOE_CH, d), F32), pltpu.VMEM((2, MOE_CH, d), F32),
                            pltpu.VMEM((d, de), BF16), pltpu.VMEM((d, de), BF16), pltpu.VMEM((de, d), BF16),
                            pltpu.SemaphoreType.DMA((2,)), pltpu.SemaphoreType.DMA((2,))]),
        out_shape=jax.ShapeDtypeStruct((rows_pad, d), F32),
        compiler_params=_cparams(("arbitrary",)),
        name="moe_experts",
    )(pstart, nchunks, xs, w_gate, w_up, w_down)


def _moe_plan(ids, cnt, tm):
    n = ids.shape[0]
    tile_cnt = cnt[::SUBLANES, :N_EXPERTS]
    counts = jnp.sum(tile_cnt, axis=0)
    padded = ((counts + MOE_CH - 1) // MOE_CH) * MOE_CH
    pstart = jnp.cumsum(padded) - padded
    tile_base = pstart[None, :] + jnp.cumsum(tile_cnt, axis=0) - tile_cnt
    eid = ids[:, 0:TOP_K]
    rank = ids[:, TOP_K:2 * TOP_K]
    base_of_tok = jnp.repeat(tile_base, tm, axis=0)[:, None, :]
    chosen = eid[:, :, None] == jnp.arange(N_EXPERTS, dtype=I32)[None, None, :]
    dest = jnp.sum(jnp.where(chosen, base_of_tok, 0), axis=-1) + rank
    return dest.reshape(-1).astype(I32), pstart.astype(I32), (padded // MOE_CH).astype(I32)


FINAL_TM = 256


def _final_kernel(alpha, dest, x1_ref, wts_ref, p_ref, wpg_ref, wple_ref, l2w_ref, l2b_ref, ys_hbm, o_ref,
                  ybuf, sem):
    i = pl.program_id(0)
    tm = x1_ref.shape[0]
    slot = i % 2

    def row(tile, buf, r, j):
        return pltpu.make_async_copy(ys_hbm.at[pl.ds(dest[(tile * tm + r) * TOP_K + j], 1)],
                                     ybuf.at[buf, j, pl.ds(r, 1)], sem.at[buf])

    def gather(tile, buf):
        _each(tm, _each_choice(lambda r, j: row(tile, buf, r, j).start(priority=j)), ROW_DMA_UNROLL)

    @pl.when(i == 0)
    def _():
        gather(0, 0)

    @pl.when(i + 1 < pl.num_programs(0))
    def _():
        gather(i + 1, 1 - slot)

    x1 = x1_ref[...]
    ple = _sigmoid(_dot(x1.astype(BF16), wpg_ref[...])) * _dot(p_ref[...], wple_ref[...])
    _each(tm, _each_choice(lambda r, j: row(i, slot, r, j).wait()), ROW_DMA_UNROLL)
    moe = ybuf[slot, 0] * wts_ref[:, 0:1] + ybuf[slot, 1] * wts_ref[:, 1:2]
    o_ref[...] = _layer_norm(alpha * x1 + moe + ple, l2w_ref[...], l2b_ref[...])


def _final(alpha, dest, x1, ys, wts, p_bf, w_pg, w_ple, ln2_w, ln2_b):
    n, d = x1.shape
    tm = FINAL_TM
    tile = lambda w_: pl.BlockSpec((tm, w_), lambda i, *_: (i, 0))
    const = lambda shape: pl.BlockSpec(shape, lambda i, *_: (0, 0))
    return pl.pallas_call(
        functools.partial(_final_kernel, alpha),
        grid_spec=pltpu.PrefetchScalarGridSpec(
            num_scalar_prefetch=1,
            grid=(n // tm,),
            in_specs=[tile(d), tile(LANES), tile(p_bf.shape[1]),
                      const((d, d)), const((p_bf.shape[1], d)), const((1, d)), const((1, d)),
                      pl.BlockSpec(memory_space=pl.ANY)],
            out_specs=tile(d),
            scratch_shapes=[pltpu.VMEM((2, TOP_K, tm, d), F32), pltpu.SemaphoreType.DMA((2,))]),
        out_shape=jax.ShapeDtypeStruct((n, d), F32),
        compiler_params=_cparams(("arbitrary",)),
        name="final_ln2",
    )(dest, x1, wts, p_bf, w_pg.astype(BF16), w_ple.astype(BF16), ln2_w[None, :], ln2_b[None, :], ys)


def _regroup_w_in(w):
    m0 = RWKV_COLS
    g0 = RWKV_COLS + MLSTM_COLS
    mqk = 2 * M_HEADS * M_QK
    pad = lambda c: jnp.zeros((w.shape[0], c), w.dtype)
    parts = [w[:, 0:3 * R_WIDTH],
             w[:, m0:m0 + mqk],
             w[:, m0 + mqk:m0 + mqk + M_WIDTH],
             w[:, m0 + mqk + M_WIDTH + 2 * M_HEADS:m0 + MLSTM_COLS],
             w[:, g0:],
             w[:, 3 * R_WIDTH:RWKV_COLS], pad(LORA_PAD - LORA_COLS),
             w[:, m0 + mqk + M_WIDTH:m0 + mqk + M_WIDTH + 2 * M_HEADS], pad(LANES - 2 * M_HEADS)]
    out = jnp.concatenate(parts, axis=1)
    assert out.shape[1] == C_TOTAL
    return out


def kernel(x, p, w_in, mu_shift, w0, w_w2, a0, w_a2, w_g2, k_k, k_a, r_k, lnx_w, lnx_b, conv_w, conv_b,
           i_bias, f_bias, mh_w, b_gate, w_br, w_bm, w_out, ln1_w, ln1_b, w_rg, b_rg, w_re, b_re,
           w_gate, w_up, w_down, w_pg, w_ple, ln2_w, ln2_b):
    bsz, seq, d = x.shape
    depth = w_in.shape[0]
    assert bsz * R_HEADS * 2 == LANES and seq % PROJ_TM == 0 and seq % CHUNK == 0
    alpha = (2 * depth) ** 0.25
    n = bsz * seq
    xf = x.reshape(n, d)
    for i in range(depth):
        u = _proj_in(xf, _regroup_w_in(w_in[i].astype(BF16)))
        r, dec, kp, v, kn, beta, g, bonus = _rwkv_prep(
            u, seq, mu_shift[i], w0[i], w_w2[i], a0[i], w_a2[i], w_g2[i], k_k[i], k_a[i], r_k[i])
        ys = _rwkv_scan(kn, dec, kp, beta, r, v, bsz, seq)
        ym = _mlstm(u, bsz, seq, conv_w[i], conv_b[i], i_bias[i], f_bias[i], mh_w[i])
        x1, ids, wts, cnt = _merge(alpha, ys, bonus, g, ym, u, xf, lnx_w[i], lnx_b[i], b_gate[i], w_br[i],
                                   w_bm[i], w_out[i], ln1_w[i], ln1_b[i], w_rg[i], b_rg[i], w_re[i], b_re[i])
        dest, pstart, nchunks = _moe_plan(ids, cnt, MERGE_TM)
        rows_pad = TOP_K * n + N_EXPERTS * MOE_CH
        xs = _moe_dispatch(x1, dest, pstart, nchunks, rows_pad)
        ys_sorted = _moe_experts(xs, pstart, nchunks, w_gate[i], w_up[i], w_down[i])
        xf = _final(alpha, dest, x1, ys_sorted, wts, p[i].reshape(n, -1).astype(BF16), w_pg[i], w_ple[i],
                    ln2_w[i], ln2_b[i])
    return xf.reshape(bsz, seq, d)
```

```python
import functools

import jax
import jax.numpy as jnp
from jax import lax
from jax.experimental import pallas as pl
from jax.experimental.pallas import tpu as pltpu

F32 = jnp.float32
BF16 = jnp.bfloat16
I32 = jnp.int32

R_HEADS, R_HEAD = 16, 64
R_WIDTH = R_HEADS * R_HEAD
DECAY_LORA, AAA_LORA, GATE_LORA = 64, 64, 160
LORA_COLS = DECAY_LORA + AAA_LORA + GATE_LORA
LORA_PAD = 512
R_GN_EPS = 64e-5
RWKV_COLS = 3 * R_WIDTH + LORA_COLS
M_HEADS, M_QK, M_V = 8, 64, 128
M_WIDTH = M_HEADS * M_V
CONV_K = 4
CHUNK = 128
M_NORM_EPS = 1e-6
MLSTM_COLS = 2 * M_HEADS * M_QK + 2 * M_WIDTH + 2 * M_HEADS
N_GROUPS, EXPERTS_PER_GROUP = 4, 8
N_EXPERTS = N_GROUPS * EXPERTS_PER_GROUP
TOP_K = 2
MOE_BLOCK = 128
LN_EPS = 1e-5

LANES = 128
SUBLANES = 8
MXU_DIM = 256
V7X_VMEM_BYTES = 64 * 1024 * 1024
VMEM_LIMIT = 56 * 1024 * 1024
SCAN_VMEM_LIMIT = 60 * 1024 * 1024

C_RKV = 0
C_MQK = 3072
C_MV = 4096
C_MO = 5120
C_GR = 6144
C_GM = 8192
C_LORA = 10240
C_MG = 10752
C_TOTAL = 10880
PROJ_TN = 2176
PROJ_TM = 512


def _cparams(sem, vmem=VMEM_LIMIT):
    return pltpu.CompilerParams(dimension_semantics=sem, vmem_limit_bytes=vmem)


def _sigmoid(x):
    return 1.0 / (1.0 + jnp.exp(-x))


def _softplus(x):
    return jnp.maximum(x, 0.0) + jnp.log1p(jnp.exp(-jnp.abs(x)))


def _split3(x):
    hi = x.astype(BF16)
    r1 = x - hi.astype(F32)
    mid = r1.astype(BF16)
    lo = (r1 - mid.astype(F32)).astype(BF16)
    return hi, mid, lo


def _dot(a, b):
    return jnp.dot(a, b, preferred_element_type=F32)


def _dot_nt(a, b):
    return lax.dot_general(a, b, (((1,), (1,)), ((), ())), preferred_element_type=F32)


def _dot_tn(a, b):
    return lax.dot_general(a, b, (((0,), (0,)), ((), ())), preferred_element_type=F32)


def _dot_exact_rhs(x, ones_bf16, terms=3):
    parts = _split3(x)[:terms]
    acc = _dot(parts[0], ones_bf16)
    for part in parts[1:]:
        acc = acc + _dot(part, ones_bf16)
    return acc


def _block_ones(n, group):
    r = lax.broadcasted_iota(I32, (n, n), 0) // group
    c = lax.broadcasted_iota(I32, (n, n), 1) // group
    return jnp.where(r == c, 1.0, 0.0).astype(BF16)


def _seg_sum(x, group):
    ones = _block_ones(MXU_DIM, group)
    slabs = [_dot_exact_rhs(x[:, p * MXU_DIM:(p + 1) * MXU_DIM], ones, terms=2)
             for p in range(x.shape[1] // MXU_DIM)]
    return jnp.concatenate(slabs, axis=1)


def _layer_norm(x, w, b):
    mu = jnp.mean(x, axis=-1, keepdims=True)
    xc = x - mu
    var = jnp.mean(xc * xc, axis=-1, keepdims=True)
    return xc * lax.rsqrt(var + LN_EPS) * w + b


def _proj_kernel(x_ref, w_ref, o_ref):
    o_ref[...] = _dot(x_ref[...].astype(BF16), w_ref[...])


def _proj_in(x_bf, w_bf):
    m, k = x_bf.shape
    n = w_bf.shape[1]
    return pl.pallas_call(
        _proj_kernel,
        grid=(n // PROJ_TN, m // PROJ_TM),
        in_specs=[pl.BlockSpec((PROJ_TM, k), lambda j, i: (i, 0)),
                  pl.BlockSpec((k, PROJ_TN), lambda j, i: (0, j))],
        out_specs=pl.BlockSpec((PROJ_TM, PROJ_TN), lambda j, i: (i, j)),
        out_shape=jax.ShapeDtypeStruct((m, n), F32),
        compiler_params=_cparams(("parallel", "parallel")),
        name="proj_in",
    )(x_bf, w_bf)


PREP_TM = 256


def _rwkv_prep_kernel(seq, u_ref, l_ref, up_ref, lp_ref, mu_ref, mul_ref, w0_ref, a0_ref, kk_ref, ka_ref,
                      rk_ref, ww_ref, wa_ref, wg_ref,
                      r_o, dec_o, kp_o, v_o, kn_o, beta_o, g_o, bonus_o):
    i = pl.program_id(0)
    tm = u_ref.shape[0]
    first = (i * tm) % seq == 0
    row = lax.broadcasted_iota(I32, (tm, 1), 0)

    def shift(u, prev8):
        prev_row = jnp.where(first, 0.0, prev8[SUBLANES - 1:SUBLANES, :])
        return jnp.where(row == 0, prev_row, pltpu.roll(u, 1, 0))

    u = u_ref[...]
    z = u + mu_ref[...] * (shift(u, up_ref[...]) - u)
    lo = l_ref[...]
    zl = lo + mul_ref[...] * (shift(lo, lp_ref[...]) - lo)

    r = z[:, 0:R_WIDTH]
    k = z[:, R_WIDTH:2 * R_WIDTH]
    v = z[:, 2 * R_WIDTH:3 * R_WIDTH]
    w_pre = w0_ref[...] + _dot(jnp.tanh(zl).astype(BF16), ww_ref[...])
    w = -_softplus(-w_pre) - 0.5
    dec = jnp.exp(-jnp.exp(w))
    a = _sigmoid(a0_ref[...] + _dot(zl.astype(BF16), wa_ref[...]))
    g = _dot(_sigmoid(zl).astype(BF16), wg_ref[...])

    kk = k * kk_ref[...]
    nrm = jnp.sqrt(_seg_sum(kk * kk, R_HEAD))
    kn = kk / jnp.maximum(nrm, 1e-12)
    kp = k * (1.0 + (a - 1.0) * ka_ref[...])
    bonus = _seg_sum(r * kp * rk_ref[...], R_HEAD) * v

    r_o[...] = r
    dec_o[...] = dec
    kp_o[...] = kp
    v_o[...] = v
    kn_o[...] = kn
    beta_o[...] = kn * a
    g_o[...] = g
    bonus_o[...] = bonus


def _rwkv_prep(u, seq, mu, w0, w_w2, a0, w_a2, w_g2, k_k, k_a, r_k):
    n = u.shape[0]
    tm = PREP_TM
    mu_rkv = mu[None, :3 * R_WIDTH]
    mu_lora = jnp.zeros((1, LORA_PAD), F32).at[0, :LORA_COLS].set(mu[3 * R_WIDTH:])
    ww = jnp.zeros((LORA_PAD, R_WIDTH), BF16).at[0:DECAY_LORA].set(w_w2.astype(BF16))
    wa = jnp.zeros((LORA_PAD, R_WIDTH), BF16).at[DECAY_LORA:DECAY_LORA + AAA_LORA].set(w_a2.astype(BF16))
    wg = jnp.zeros((LORA_PAD, R_WIDTH), BF16).at[DECAY_LORA + AAA_LORA:LORA_COLS].set(w_g2.astype(BF16))
    row = lambda t: t.reshape(1, R_WIDTH)
    prev_blk = lambda i: jnp.maximum(i * (tm // SUBLANES) - 1, 0)
    const = lambda shape: pl.BlockSpec(shape, lambda i: (0, 0))
    nat = jax.ShapeDtypeStruct((n, R_WIDTH), F32)
    return pl.pallas_call(
        functools.partial(_rwkv_prep_kernel, seq),
        grid=(n // tm,),
        in_specs=[pl.BlockSpec((tm, 3 * R_WIDTH), lambda i: (i, C_RKV // (3 * R_WIDTH))),
                  pl.BlockSpec((tm, LORA_PAD), lambda i: (i, C_LORA // LORA_PAD)),
                  pl.BlockSpec((SUBLANES, 3 * R_WIDTH), lambda i: (prev_blk(i), C_RKV // (3 * R_WIDTH))),
                  pl.BlockSpec((SUBLANES, LORA_PAD), lambda i: (prev_blk(i), C_LORA // LORA_PAD)),
                  const((1, 3 * R_WIDTH)), const((1, LORA_PAD)),
                  const((1, R_WIDTH)), const((1, R_WIDTH)), const((1, R_WIDTH)), const((1, R_WIDTH)),
                  const((1, R_WIDTH)),
                  const((LORA_PAD, R_WIDTH)), const((LORA_PAD, R_WIDTH)), const((LORA_PAD, R_WIDTH))],
        out_specs=[pl.BlockSpec((tm, R_WIDTH), lambda i: (i, 0))] * 8,
        out_shape=[nat] * 8,
        compiler_params=_cparams(("parallel",)),
        name="rwkv_prep",
    )(u, u, u, u, mu_rkv, mu_lora, row(w0), row(a0), row(k_k), row(k_a), row(r_k), ww, wa, wg)


SCAN_TB = LANES // 2
SCAN_OUT = 2
SCAN_V = R_HEAD // 2
SCAN_ACC = 1
SCAN_KOPS = 5
SCAN_PAIRS = ((0, 1), (2, 3), (4, 5))
SCAN_HP = R_WIDTH // LANES
SCAN_STEPS1 = 2
SCAN_STEPS2 = 7
SCAN_KEYS_PER_TRIP = 8
SCAN_TRIPS2 = R_HEAD // SCAN_KEYS_PER_TRIP
Z_PITCH = R_HEAD + SUBLANES
V_PITCH = SCAN_V + SUBLANES
RELAYOUT_UNROLL = 8


def _rwkv_scan_kernel(kn_hbm, dec_hbm, kp_hbm, beta_hbm, r_hbm, v_hbm, y_ref,
                      s_ref, stage, z_ref, xs_ref, vs_ref, ys_ref, sem):
    i = pl.program_id(0)
    nblk = pl.num_programs(0)
    nb = stage.shape[2]
    tb = SCAN_TB
    slab = R_HEADS * Z_PITCH
    zrows = nb * slab
    cur = i % 2
    srcs = (kn_hbm, dec_hbm, kp_hbm, beta_hbm, r_hbm, v_hbm)

    def fetch(op, blk, par):
        return pltpu.make_async_copy(srcs[op].at[:, pl.ds(blk * tb, tb), :], stage.at[par, op], sem.at[par])

    def fetch_start(blk, par):
        for op in range(len(srcs)):
            fetch(op, blk, par).start()

    def fetch_wait(blk, par):
        for op in range(len(srcs)):
            fetch(op, blk, par).wait()

    def to_time_on_lanes(par, b):
        for p, (oa, ob) in enumerate(SCAN_PAIRS):
            for hp in range(SCAN_HP):
                cols = slice(hp * LANES, (hp + 1) * LANES)
                tt = jnp.concatenate([stage[par, oa, b, :, cols], stage[par, ob, b, :, cols]], axis=0).T
                for hh in range(2):
                    row0 = pl.multiple_of(p * zrows + (b * R_HEADS + 2 * hp + hh) * Z_PITCH, SUBLANES)
                    z_ref[pl.ds(row0, R_HEAD), :] = tt[hh * R_HEAD:(hh + 1) * R_HEAD]

    def head_rows(p, b, c):
        return z_ref[pl.ds(p * zrows + b * slab + c, R_HEADS, stride=Z_PITCH), :]

    def key_unit(buf, p, k):
        oa, ob = SCAN_PAIRS[p]
        rows = [head_rows(p, b, k) for b in range(nb)]
        tt = jnp.concatenate(rows + rows, axis=0).T
        xs_ref[buf, oa, k] = tt[0:tb]
        if ob < SCAN_KOPS:
            xs_ref[buf, ob, k] = tt[tb:2 * tb]

    def value_unit(buf, vp):
        rows = [head_rows(len(SCAN_PAIRS) - 1, b, vh * SCAN_V + vp) for vh in range(2) for b in range(nb)]
        tt = jnp.concatenate(rows, axis=0).T
        vs_ref[pl.ds(buf * tb * V_PITCH + vp, tb, stride=V_PITCH), :] = tt[tb:2 * tb]

    def chain_units(buf, q):
        units = [functools.partial(key_unit, buf, p, SCAN_KEYS_PER_TRIP * q + kk)
                 for kk in range(SCAN_KEYS_PER_TRIP) for p in range(len(SCAN_PAIRS))]
        units += [functools.partial(value_unit, buf, (SCAN_KEYS_PER_TRIP // 2) * q + vv)
                  for vv in range(SCAN_KEYS_PER_TRIP // 2)]
        return units

    @pl.when(i == 0)
    def _():
        s_ref[...] = jnp.zeros(s_ref.shape, F32)
        fetch_start(0, 0)
        fetch_wait(0, 0)

        @pl.when(nblk > 1)
        def _():
            fetch_start(1, 1)

        _each(nb, lambda b: to_time_on_lanes(0, b))
        _each(SCAN_TRIPS2, lambda q: [unit() for unit in chain_units(0, q)])

    @pl.when(i + 1 < nblk)
    def _():
        fetch_wait(i + 1, 1 - cur)

    @pl.when(i + 2 < nblk)
    def _():
        fetch_start(i + 2, cur)

    def total(acc):
        while len(acc) > 1:
            acc = [acc[j] + acc[j + 1] for j in range(0, len(acc), 2)]
        return acc[0]

    def add_term(acc, k, term):
        acc[k % SCAN_ACC] = term if acc[k % SCAN_ACC] is None else acc[k % SCAN_ACC] + term

    def key_row(op, k, t):
        return xs_ref[cur, op, k, pl.ds(t, 1), :]

    acc = [None] * SCAN_ACC
    for k in range(R_HEAD):
        add_term(acc, k, s_ref[k] * xs_ref[cur, 0, k, 0:1, :])

    def step(t, s_kk):
        v_t = vs_ref[pl.ds(pl.multiple_of((cur * tb + t) * V_PITCH, SUBLANES), SCAN_V), :]
        t_next = jnp.minimum(t + 1, tb - 1)
        acc_y = [None] * SCAN_ACC
        acc_s = [None] * SCAN_ACC
        for k in range(R_HEAD):
            s_new = (s_ref[k] * key_row(1, k, t)
                     + (v_t * key_row(2, k, t) - s_kk * key_row(3, k, t)))
            s_ref[k] = s_new
            add_term(acc_y, k, s_new * key_row(4, k, t))
            add_term(acc_s, k, s_new * key_row(0, k, t_next))
        out_row = pl.multiple_of(((i % SCAN_OUT) * tb + t) * V_PITCH, SUBLANES)
        ys_ref[pl.ds(out_row, SCAN_V), :] = total(acc_y)
        return total(acc_s)

    def first_part(b, s_kk):
        for j in range(SCAN_STEPS1):
            s_kk = step(b * SCAN_STEPS1 + j, s_kk)
        to_time_on_lanes(1 - cur, b)
        return s_kk

    def second_part(q, s_kk):
        for j in range(SCAN_STEPS2):
            s_kk = step(nb * SCAN_STEPS1 + q * SCAN_STEPS2 + j, s_kk)
        for unit in chain_units(1 - cur, q):
            unit()
        return s_kk

    s_kk = lax.fori_loop(0, nb, first_part, total(acc))
    lax.fori_loop(0, SCAN_TRIPS2, second_part, s_kk)

    @pl.when(i % SCAN_OUT == SCAN_OUT - 1)
    def _():
        steps = SCAN_OUT * tb

        def out_rows(vp, carry):
            tt = ys_ref[pl.ds(vp, steps, stride=V_PITCH), :].T
            for vh in range(2):
                for b in range(nb):
                    g = vh * nb + b
                    z_ref[pl.ds(b * slab + vh * SCAN_V + vp, R_HEADS, stride=Z_PITCH), :] = \
                        tt[g * R_HEADS:(g + 1) * R_HEADS]
            return carry
        lax.fori_loop(0, SCAN_V, out_rows, 0, unroll=RELAYOUT_UNROLL)
        for b in range(nb):
            for hp in range(SCAN_HP):
                pair = [z_ref[pl.ds((b * R_HEADS + 2 * hp + hh) * Z_PITCH, R_HEAD), :] for hh in range(2)]
                y_ref[b, :, hp * LANES:(hp + 1) * LANES] = jnp.concatenate(pair, axis=0).T


def _rwkv_scan(kn, dec, kp, beta, r, v, bsz, seq):
    shape3 = (bsz, seq, R_WIDTH)
    ops = [a.reshape(shape3) for a in (kn, dec, kp, beta, r, v)]
    assert SCAN_TB == bsz * SCAN_STEPS1 + SCAN_TRIPS2 * SCAN_STEPS2 and seq % (SCAN_OUT * SCAN_TB) == 0
    y = pl.pallas_call(
        _rwkv_scan_kernel,
        grid=(seq // SCAN_TB,),
        in_specs=[pl.BlockSpec(memory_space=pl.ANY)] * len(ops),
        out_specs=pl.BlockSpec((bsz, SCAN_OUT * SCAN_TB, R_WIDTH), lambda i: (0, i // SCAN_OUT, 0)),
        out_shape=jax.ShapeDtypeStruct(shape3, F32),
        scratch_shapes=[pltpu.VMEM((R_HEAD, SCAN_V, LANES), F32),
                        pltpu.VMEM((2, len(ops), bsz, SCAN_TB, R_WIDTH), F32),
                        pltpu.VMEM((len(SCAN_PAIRS) * bsz * R_HEADS * Z_PITCH, LANES), F32),
                        pltpu.VMEM((2, SCAN_KOPS, R_HEAD, SCAN_TB, LANES), F32),
                        pltpu.VMEM((2 * SCAN_TB * V_PITCH, LANES), F32),
                        pltpu.VMEM((SCAN_OUT * SCAN_TB * V_PITCH, LANES), F32),
                        pltpu.SemaphoreType.DMA((2,))],
        compiler_params=_cparams(("arbitrary",), vmem=SCAN_VMEM_LIMIT),
        name="rwkv_scan",
    )(*ops)
    return y.reshape(bsz * seq, R_WIDTH)


M_AUG = 2 * M_V


def _mlstm_kernel(qk_ref, v_ref, o_ref, g_ref, cw_ref, cb_ref, gb_ref, mh_ref, y_ref,
                  ext_ref, c_ref, m_ref):
    c_idx = pl.program_id(0)
    L = CHUNK
    nb = qk_ref.shape[0]

    @pl.when(c_idx == 0)
    def _():
        ext_ref[:, 0:SUBLANES, :] = jnp.zeros((nb, SUBLANES, ext_ref.shape[2]), F32)
        c_ref[...] = jnp.zeros(c_ref.shape, F32)
        m_ref[...] = jnp.full(m_ref.shape, -jnp.inf, F32)

    lane = lax.broadcasted_iota(I32, (L, LANES), 1)
    ti = lax.broadcasted_iota(I32, (L, L), 0)
    si = lax.broadcasted_iota(I32, (L, L), 1)
    causal = si <= ti
    ltri = jnp.where(causal, 1.0, 0.0).astype(BF16)
    utri = jnp.where(ti <= si, 1.0, 0.0).astype(BF16)
    ones_col = jnp.where(lane == 0, 1.0, 0.0).astype(BF16)

    q_all, k_all, gcols, grows, acols, arows = [], [], [], [], [], []
    for b in range(nb):
        ext_ref[b, SUBLANES:SUBLANES + L, :] = qk_ref[b]
        conv = cb_ref[...]
        for j in range(CONV_K):
            off = SUBLANES - (CONV_K - 1) + j
            conv = conv + cw_ref[j:j + 1, :] * ext_ref[b, off:off + L, :]
        ext_ref[b, 0:SUBLANES, :] = ext_ref[b, L:L + SUBLANES, :]
        qk = conv * _sigmoid(conv)
        q_all.append(qk[:, :M_HEADS * M_QK].astype(BF16))
        k_all.append(qk[:, M_HEADS * M_QK:] * (M_QK ** -0.5))
        gpre = g_ref[b] + gb_ref[...]
        gcols.append(jnp.where(lane < M_HEADS, gpre, -_softplus(-gpre)))
        grows.append(gcols[b].T)
        acols.append(_dot_exact_rhs_left(ltri, gcols[b]))
        arows.append(_dot_exact_rhs(grows[b], utri))

    chains = [(b, h) for b in range(nb) for h in range(M_HEADS)]
    every = range(len(chains))
    a_col = [acols[b][:, M_HEADS + h:M_HEADS + h + 1] for b, h in chains]
    i_col = [gcols[b][:, h:h + 1] for b, h in chains]
    m_st = [m_ref[u:u + 1, 0:1] for u in every]
    d = [jnp.where(causal, a_col[u] - arows[b][M_HEADS + h:M_HEADS + h + 1, :] + grows[b][h:h + 1, :], -jnp.inf)
         for u, (b, h) in enumerate(chains)]
    dmax = [jnp.max(d[u], axis=-1, keepdims=True) for u in every]
    inter = [a_col[u] + m_st[u] for u in every]
    m_t = [jnp.maximum(inter[u], dmax[u]) for u in every]
    q = [q_all[b][:, h * M_QK:(h + 1) * M_QK] for b, h in chains]
    k = [k_all[b][:, h * M_QK:(h + 1) * M_QK] for b, h in chains]
    v_aug = [jnp.concatenate([v_ref[b, :, h * M_V:(h + 1) * M_V].astype(BF16), ones_col], axis=1)
             for b, h in chains]
    s = [_dot_nt(q[u], k[u].astype(BF16)) * jnp.exp(d[u] - m_t[u]) for u in every]
    ie = [jnp.exp(inter[u] - m_t[u]) for u in every]
    tot = [ie[u] * _dot(q[u], c_ref[u].astype(BF16)) + _dot(s[u].astype(BF16), v_aug[u]) for u in every]
    h_c = [tot[u][:, :M_V] / jnp.maximum(jnp.abs(tot[u][:, M_V:M_V + 1]), jnp.exp(-m_t[u])) for u in every]

    a_tot = [a_col[u][L - 1:L, :] for u in every]
    gl = [a_tot[u] - a_col[u] + i_col[u] for u in every]
    m_new = [jnp.maximum(a_tot[u] + m_st[u], jnp.max(gl[u], axis=0, keepdims=True)) for u in every]
    kg = [(k[u] * jnp.exp(gl[u] - m_new[u])).astype(BF16) for u in every]
    for u in every:
        c_ref[u] = jnp.exp(a_tot[u] + m_st[u] - m_new[u]) * c_ref[u] + _dot_tn(kg[u], v_aug[u])
        m_ref[u:u + 1, :] = jnp.broadcast_to(m_new[u], (1, LANES))

    mu = [jnp.mean(h_c[u], axis=-1, keepdims=True) for u in every]
    hc = [h_c[u] - mu[u] for u in every]
    var = [jnp.mean(hc[u] * hc[u], axis=-1, keepdims=True) for u in every]
    for u, (b, h) in enumerate(chains):
        hn = hc[u] * lax.rsqrt(var[u] + M_NORM_EPS) * mh_ref[:, h * M_V:(h + 1) * M_V]
        y_ref[b, :, h * M_V:(h + 1) * M_V] = _sigmoid(o_ref[b, :, h * M_V:(h + 1) * M_V]) * hn


def _dot_exact_rhs_left(ones_bf16, x):
    hi, mid, lo = _split3(x)
    return _dot(ones_bf16, hi) + _dot(ones_bf16, mid) + _dot(ones_bf16, lo)


def _mlstm(u, bsz, seq, conv_w, conv_b, i_bias, f_bias, mh_w):
    nc = seq // CHUNK
    w = M_WIDTH
    gbias = jnp.zeros((1, LANES), F32).at[0, :M_HEADS].set(i_bias).at[0, M_HEADS:2 * M_HEADS].set(f_bias)
    u3 = u.reshape(bsz, seq, u.shape[1])
    blk = lambda col: pl.BlockSpec((bsz, CHUNK, w), lambda c: (0, c, col // w))
    const = lambda shape: pl.BlockSpec(shape, lambda c: (0, 0))
    y = pl.pallas_call(
        _mlstm_kernel,
        grid=(nc,),
        in_specs=[blk(C_MQK), blk(C_MV), blk(C_MO),
                  pl.BlockSpec((bsz, CHUNK, LANES), lambda c: (0, c, C_MG // LANES)),
                  const((CONV_K, w)), const((1, w)), const((1, LANES)), const((1, w))],
        out_specs=pl.BlockSpec((bsz, CHUNK, w), lambda c: (0, c, 0)),
        out_shape=jax.ShapeDtypeStruct((bsz, seq, w), F32),
        scratch_shapes=[pltpu.VMEM((bsz, CHUNK + SUBLANES, w), F32),
                        pltpu.VMEM((bsz * M_HEADS, M_QK, M_AUG), F32),
                        pltpu.VMEM((bsz * M_HEADS, LANES), F32)],
        compiler_params=_cparams(("arbitrary",)),
        name="mlstm",
    )(u3, u3, u3, u3, conv_w, conv_b[None, :], gbias, mh_w[None, :])
    return y.reshape(bsz * seq, w)


MERGE_TM = 256
C_ROUTE_G = N_EXPERTS


def _merge_kernel(alpha, ys_ref, bonus_ref, g_ref, ym_ref, gr_ref, gm_ref, x_ref,
                  lnxw_ref, lnxb_ref, bgr_ref, bgm_ref, wbr_ref, wbm_ref, wout_ref, l1w_ref, l1b_ref,
                  wrh_ref, wrl_ref, br_ref, x1_o, ids_o, wts_o, cnt_o):
    ys = ys_ref[...]
    mu = _seg_sum(ys, R_HEAD) * (1.0 / R_HEAD)
    yc = ys - mu
    var = _seg_sum(yc * yc, R_HEAD) * (1.0 / R_HEAD)
    y = yc * lax.rsqrt(var + R_GN_EPS) * lnxw_ref[...] + lnxb_ref[...]
    y_r = (y + bonus_ref[...]) * g_ref[...]
    br = _dot(y_r.astype(BF16), wbr_ref[...])
    bm = _dot(ym_ref[...].astype(BF16), wbm_ref[...])
    mix_in = _sigmoid(gr_ref[...] + bgr_ref[...]) * br + _sigmoid(gm_ref[...] + bgm_ref[...]) * bm
    mix = _dot(mix_in.astype(BF16), wout_ref[...])
    x1 = _layer_norm(alpha * x_ref[...] + mix, l1w_ref[...], l1b_ref[...])
    x1_o[...] = x1

    xh = x1.astype(BF16)
    xl = (x1 - xh.astype(F32)).astype(BF16)
    logits = (_dot(xh, wrh_ref[...]) + (_dot(xh, wrl_ref[...]) + _dot(xl, wrh_ref[...]))) + br_ref[...]
    tm = logits.shape[0]
    lane_i = lax.broadcasted_iota(I32, (tm, LANES), 1)
    lane = lane_i.astype(F32)
    group_of_lane = (lane_i // EXPERTS_PER_GROUP).astype(F32)
    big = float(LANES)
    neg = -jnp.inf
    lg = jnp.where((lane_i >= C_ROUTE_G) & (lane_i < C_ROUTE_G + N_GROUPS), logits, neg)
    gmax = jnp.max(lg, axis=-1, keepdims=True)
    gsel = jnp.min(jnp.where(lg == gmax, lane - C_ROUTE_G, big), axis=-1, keepdims=True)
    g_w = 1.0 / jnp.sum(jnp.exp(lg - gmax), axis=-1, keepdims=True)
    le = jnp.where((lane_i < N_EXPERTS) & (group_of_lane == gsel), logits, neg)
    m1 = jnp.max(le, axis=-1, keepdims=True)
    i1 = jnp.min(jnp.where(le == m1, lane, big), axis=-1, keepdims=True)
    le2 = jnp.where(lane == i1, neg, le)
    m2 = jnp.max(le2, axis=-1, keepdims=True)
    i2 = jnp.min(jnp.where(le2 == m2, lane, big), axis=-1, keepdims=True)
    e2 = jnp.exp(m2 - m1)
    w1 = g_w / (1.0 + e2)
    w2 = g_w * e2 / (1.0 + e2)
    wts_o[...] = jnp.where(lane_i == 0, w1, jnp.where(lane_i == 1, w2, 0.0))

    oh1 = jnp.where(lane == i1, 1.0, 0.0)
    oh2 = jnp.where(lane == i2, 1.0, 0.0)
    ri = lax.broadcasted_iota(I32, (tm, tm), 0)
    ci = lax.broadcasted_iota(I32, (tm, tm), 1)
    lstrict = jnp.where(ci < ri, 1.0, 0.0).astype(BF16)
    tot1 = jnp.sum(oh1, axis=0, keepdims=True)
    tot2 = jnp.sum(oh2, axis=0, keepdims=True)
    rank1 = jnp.sum(_dot(lstrict, oh1.astype(BF16)) * oh1, axis=-1, keepdims=True)
    rank2 = jnp.sum((_dot(lstrict, oh2.astype(BF16)) + tot1) * oh2, axis=-1, keepdims=True)
    ids = jnp.where(lane_i == 0, i1, jnp.where(lane_i == 1, i2, jnp.where(lane_i == 2, rank1,
                                                                          jnp.where(lane_i == 3, rank2, 0.0))))
    ids_o[...] = ids.astype(I32)
    cnt_o[...] = jnp.broadcast_to(tot1 + tot2, cnt_o.shape).astype(I32)


def _merge(alpha, ys, bonus, g, ym, u, x, lnx_w, lnx_b, b_gate, w_br, w_bm, w_out, ln1_w, ln1_b,
           w_rg, b_rg, w_re, b_re):
    n, d = x.shape
    tm = MERGE_TM
    wr = jnp.zeros((d, LANES), F32).at[:, :N_EXPERTS].set(w_re).at[:, C_ROUTE_G:C_ROUTE_G + N_GROUPS].set(w_rg)
    wr_hi = wr.astype(BF16)
    wr_lo = (wr - wr_hi.astype(F32)).astype(BF16)
    b_r = jnp.zeros((1, LANES), F32).at[0, :N_EXPERTS].set(b_re).at[0, C_ROUTE_G:C_ROUTE_G + N_GROUPS].set(b_rg)
    tile = lambda w_: pl.BlockSpec((tm, w_), lambda i: (i, 0))
    const = lambda shape: pl.BlockSpec(shape, lambda i: (0, 0))
    return pl.pallas_call(
        functools.partial(_merge_kernel, alpha),
        grid=(n // tm,),
        in_specs=[tile(R_WIDTH), tile(R_WIDTH), tile(R_WIDTH), tile(M_WIDTH),
                  pl.BlockSpec((tm, d), lambda i: (i, C_GR // d)),
                  pl.BlockSpec((tm, d), lambda i: (i, C_GM // d)),
                  tile(d),
                  const((1, R_WIDTH)), const((1, R_WIDTH)), const((1, d)), const((1, d)),
                  const((R_WIDTH, d)), const((M_WIDTH, d)), const((d, d)), const((1, d)), const((1, d)),
                  const((d, LANES)), const((d, LANES)), const((1, LANES))],
        out_specs=[tile(d), tile(LANES), tile(LANES), pl.BlockSpec((SUBLANES, LANES), lambda i: (i, 0))],
        out_shape=[jax.ShapeDtypeStruct((n, d), F32), jax.ShapeDtypeStruct((n, LANES), I32),
                   jax.ShapeDtypeStruct((n, LANES), F32),
                   jax.ShapeDtypeStruct((n // tm * SUBLANES, LANES), I32)],
        compiler_params=_cparams(("parallel",)),
        name="merge_ln1_router",
    )(ys, bonus, g, ym, u, u, x, lnx_w[None, :], lnx_b[None, :], b_gate[None, :d], b_gate[None, d:],
      w_br.astype(BF16), w_bm.astype(BF16), w_out.astype(BF16), ln1_w[None, :], ln1_b[None, :],
      wr_hi, wr_lo, b_r)


MOE_CH = 256
DISPATCH_TM = 256


ROW_DMA_UNROLL = 8


def _each(count, fn, unroll=1):
    def body(r, carry):
        fn(r)
        return carry
    lax.fori_loop(0, count, body, 0, unroll=unroll)


def _each_choice(fn):
    def both(r):
        for j in range(TOP_K):
            fn(r, j)
    return both


def _zero_unused_tail(pstart, nchunks, zeros_ref, out_hbm, sem):
    ch = zeros_ref.shape[0]
    used = (pstart[N_EXPERTS - 1] + nchunks[N_EXPERTS - 1] * ch) // ch
    total = out_hbm.shape[0] // ch

    def chunk(c):
        return pltpu.make_async_copy(zeros_ref, out_hbm.at[pl.ds(pl.multiple_of(c * ch, ch), ch)], sem)

    def over_tail(fn):
        def body(c, carry):
            fn(c)
            return carry
        lax.fori_loop(used, total, body, 0)

    over_tail(lambda c: chunk(c).start())
    over_tail(lambda c: chunk(c).wait())


def _moe_dispatch_kernel(dest, pstart, nchunks, x_ref, xs_hbm, zbuf, sem_z, sem):
    i = pl.program_id(0)
    tm = x_ref.shape[0]
    ch = zbuf.shape[0]

    @pl.when(i == 0)
    def _():
        zbuf[...] = jnp.zeros(zbuf.shape, F32)

        def tail(e):
            row0 = pl.multiple_of(pstart[e] + (nchunks[e] - 1) * ch, ch)
            return pltpu.make_async_copy(zbuf, xs_hbm.at[pl.ds(row0, ch)], sem_z)

        def start(e):
            @pl.when(nchunks[e] > 0)
            def _():
                tail(e).start()

        def wait(e):
            @pl.when(nchunks[e] > 0)
            def _():
                tail(e).wait()

        _each(N_EXPERTS, start)
        _each(N_EXPERTS, wait)
        _zero_unused_tail(pstart, nchunks, zbuf, xs_hbm, sem_z)

    def row(r, j):
        return pltpu.make_async_copy(x_ref.at[pl.ds(r, 1)],
                                     xs_hbm.at[pl.ds(dest[(i * tm + r) * TOP_K + j], 1)], sem)

    _each(tm, _each_choice(lambda r, j: row(r, j).start(priority=j)), ROW_DMA_UNROLL)
    _each(tm, _each_choice(lambda r, j: row(r, j).wait()), ROW_DMA_UNROLL)


def _moe_dispatch(x1, dest, pstart, nchunks, rows_pad):
    n, d = x1.shape
    tm = DISPATCH_TM
    return pl.pallas_call(
        _moe_dispatch_kernel,
        grid_spec=pltpu.PrefetchScalarGridSpec(
            num_scalar_prefetch=3,
            grid=(n // tm,),
            in_specs=[pl.BlockSpec((tm, d), lambda i, *_: (i, 0))],
            out_specs=pl.BlockSpec(memory_space=pl.ANY),
            scratch_shapes=[pltpu.VMEM((MOE_CH, d), F32), pltpu.SemaphoreType.DMA, pltpu.SemaphoreType.DMA]),
        out_shape=jax.ShapeDtypeStruct((rows_pad, d), F32),
        compiler_params=_cparams(("arbitrary",)),
        name="moe_dispatch",
    )(dest, pstart, nchunks, x1)


def _moe_expert_kernel(pstart, nchunks, xs_hbm, wg_ref, wu_ref, wd_ref, ys_hbm,
                       xbuf, ybuf, wgb, wub, wdb, sem_in, sem_out):
    e = pl.program_id(0)
    ch = xbuf.shape[1]
    nc = nchunks[e]
    g0 = pstart[e] // ch
    total = (pstart[N_EXPERTS - 1] + nchunks[N_EXPERTS - 1] * ch) // ch

    def rows(g):
        return pl.ds(pl.multiple_of(g * ch, ch), ch)

    def load(g):
        return pltpu.make_async_copy(xs_hbm.at[rows(g)], xbuf.at[g % 2], sem_in.at[g % 2])

    def store(g):
        return pltpu.make_async_copy(ybuf.at[g % 2], ys_hbm.at[rows(g)], sem_out.at[g % 2])

    @pl.when(nc > 0)
    def _():
        @pl.when(g0 == 0)
        def _():
            load(0).start()

        wgb[...] = wg_ref[0].astype(BF16)
        wub[...] = wu_ref[0].astype(BF16)
        wdb[...] = wd_ref[0].astype(BF16)

        def chunk(c, carry):
            g = g0 + c
            load(g).wait()

            @pl.when(g + 1 < total)
            def _():
                load(g + 1).start()

            @pl.when(g >= 2)
            def _():
                store(g - 2).wait()

            xb = xbuf[g % 2].astype(BF16)
            gate = _dot(xb, wgb[...])
            hb = gate * _sigmoid(gate) * _dot(xb, wub[...])
            ybuf[g % 2] = _dot(hb.astype(BF16), wdb[...])
            store(g).start()
            return carry

        lax.fori_loop(0, nc, chunk, 0)

    @pl.when(e == pl.num_programs(0) - 1)
    def _():
        @pl.when(total >= 2)
        def _():
            store(total - 2).wait()

        @pl.when(total >= 1)
        def _():
            store(total - 1).wait()

        ybuf[0] = jnp.zeros(ybuf.shape[1:], F32)
        _zero_unused_tail(pstart, nchunks, ybuf.at[0], ys_hbm, sem_out.at[0])


def _moe_experts(xs, pstart, nchunks, w_gate, w_up, w_down):
    rows_pad, d = xs.shape
    de = w_gate.shape[-1]
    wspec = lambda shape: pl.BlockSpec(shape, lambda e, *_: (e, 0, 0))
    return pl.pallas_call(
        _moe_expert_kernel,
        grid_spec=pltpu.PrefetchScalarGridSpec(
            num_scalar_prefetch=2,
            grid=(N_EXPERTS,),
            in_specs=[pl.BlockSpec(memory_space=pl.ANY),
                      wspec((1, d, de)), wspec((1, d, de)), wspec((1, de, d))],
            out_specs=pl.BlockSpec(memory_space=pl.ANY),
            scratch_shapes=[pltpu.VMEM((2, MOE_CH, d), F32), pltpu.VMEM((2, MOE_CH, d), F32),
                            pltpu.VMEM((d, de), BF16), pltpu.VMEM((d, de), BF16), pltpu.VMEM((de, d), BF16),
                            pltpu.SemaphoreType.DMA((2,)), pltpu.SemaphoreType.DMA((2,))]),
        out_shape=jax.ShapeDtypeStruct((rows_pad, d), F32),
        compiler_params=_cparams(("arbitrary",)),
        name="moe_experts",
    )(pstart, nchunks, xs, w_gate, w_up, w_down)


def _moe_plan(ids, cnt, tm):
    n = ids.shape[0]
    tile_cnt = cnt[::SUBLANES, :N_EXPERTS]
    counts = jnp.sum(tile_cnt, axis=0)
    padded = ((counts + MOE_CH - 1) // MOE_CH) * MOE_CH
    pstart = jnp.cumsum(padded) - padded
    tile_base = pstart[None, :] + jnp.cumsum(tile_cnt, axis=0) - tile_cnt
    eid = ids[:, 0:TOP_K]
    rank = ids[:, TOP_K:2 * TOP_K]
    base_of_tok = jnp.repeat(tile_base, tm, axis=0)[:, None, :]
    chosen = eid[:, :, None] == jnp.arange(N_EXPERTS, dtype=I32)[None, None, :]
    dest = jnp.sum(jnp.where(chosen, base_of_tok, 0), axis=-1) + rank
    return dest.reshape(-1).astype(I32), pstart.astype(I32), (padded // MOE_CH).astype(I32)


FINAL_TM = 256


def _final_kernel(alpha, dest, x1_ref, wts_ref, p_ref, wpg_ref, wple_ref, l2w_ref, l2b_ref, ys_hbm, o_ref,
                  ybuf, sem):
    i = pl.program_id(0)
    tm = x1_ref.shape[0]
    slot = i % 2

    def row(tile, buf, r, j):
        return pltpu.make_async_copy(ys_hbm.at[pl.ds(dest[(tile * tm + r) * TOP_K + j], 1)],
                                     ybuf.at[buf, j, pl.ds(r, 1)], sem.at[buf])

    def gather(tile, buf):
        _each(tm, _each_choice(lambda r, j: row(tile, buf, r, j).start(priority=j)), ROW_DMA_UNROLL)

    @pl.when(i == 0)
    def _():
        gather(0, 0)

    @pl.when(i + 1 < pl.num_programs(0))
    def _():
        gather(i + 1, 1 - slot)

    x1 = x1_ref[...]
    ple = _sigmoid(_dot(x1.astype(BF16), wpg_ref[...])) * _dot(p_ref[...], wple_ref[...])
    _each(tm, _each_choice(lambda r, j: row(i, slot, r, j).wait()), ROW_DMA_UNROLL)
    moe = ybuf[slot, 0] * wts_ref[:, 0:1] + ybuf[slot, 1] * wts_ref[:, 1:2]
    o_ref[...] = _layer_norm(alpha * x1 + moe + ple, l2w_ref[...], l2b_ref[...])


def _final(alpha, dest, x1, ys, wts, p_bf, w_pg, w_ple, ln2_w, ln2_b):
    n, d = x1.shape
    tm = FINAL_TM
    tile = lambda w_: pl.BlockSpec((tm, w_), lambda i, *_: (i, 0))
    const = lambda shape: pl.BlockSpec(shape, lambda i, *_: (0, 0))
    return pl.pallas_call(
        functools.partial(_final_kernel, alpha),
        grid_spec=pltpu.PrefetchScalarGridSpec(
            num_scalar_prefetch=1,
            grid=(n // tm,),
            in_specs=[tile(d), tile(LANES), tile(p_bf.shape[1]),
                      const((d, d)), const((p_bf.shape[1], d)), const((1, d)), const((1, d)),
                      pl.BlockSpec(memory_space=pl.ANY)],
            out_specs=tile(d),
            scratch_shapes=[pltpu.VMEM((2, TOP_K, tm, d), F32), pltpu.SemaphoreType.DMA((2,))]),
        out_shape=jax.ShapeDtypeStruct((n, d), F32),
        compiler_params=_cparams(("arbitrary",)),
        name="final_ln2",
    )(dest, x1, wts, p_bf, w_pg.astype(BF16), w_ple.astype(BF16), ln2_w[None, :], ln2_b[None, :], ys)


REGROUP_TM = 256


def _regroup_kernel(w_ref, o_ref):
    tm = w_ref.shape[0]
    m0 = RWKV_COLS
    mg = m0 + 2 * M_HEADS * M_QK + M_WIDTH
    cols = lambda a, n: w_ref[:, a:a + n].astype(BF16)
    zeros = lambda n: jnp.zeros((tm, n), BF16)
    o_ref[:, C_RKV:C_MQK] = cols(0, 3 * R_WIDTH)
    o_ref[:, C_MQK:C_MO] = cols(m0, C_MO - C_MQK)
    o_ref[:, C_MO:C_LORA] = cols(mg + 2 * M_HEADS, C_LORA - C_MO)
    o_ref[:, C_LORA:C_MG] = jnp.concatenate([cols(3 * R_WIDTH, LORA_COLS), zeros(LORA_PAD - LORA_COLS)], axis=1)
    o_ref[:, C_MG:C_TOTAL] = jnp.concatenate([cols(mg, 2 * M_HEADS), zeros(LANES - 2 * M_HEADS)], axis=1)


def _regroup_w_in(w):
    k, n = w.shape
    assert n == RWKV_COLS + MLSTM_COLS + 2 * k and C_LORA - C_MO == M_WIDTH + 2 * k
    return pl.pallas_call(
        _regroup_kernel,
        grid=(k // REGROUP_TM,),
        in_specs=[pl.BlockSpec((REGROUP_TM, n), lambda i: (i, 0))],
        out_specs=pl.BlockSpec((REGROUP_TM, C_TOTAL), lambda i: (i, 0)),
        out_shape=jax.ShapeDtypeStruct((k, C_TOTAL), BF16),
        compiler_params=_cparams(("parallel",)),
        name="regroup_w_in",
    )(w)


def kernel(x, p, w_in, mu_shift, w0, w_w2, a0, w_a2, w_g2, k_k, k_a, r_k, lnx_w, lnx_b, conv_w, conv_b,
           i_bias, f_bias, mh_w, b_gate, w_br, w_bm, w_out, ln1_w, ln1_b, w_rg, b_rg, w_re, b_re,
           w_gate, w_up, w_down, w_pg, w_ple, ln2_w, ln2_b):
    bsz, seq, d = x.shape
    depth = w_in.shape[0]
    assert bsz * R_HEADS * 2 == LANES and seq % PROJ_TM == 0 and seq % CHUNK == 0
    alpha = (2 * depth) ** 0.25
    n = bsz * seq
    xf = x.reshape(n, d)
    for i in range(depth):
        u = _proj_in(xf, _regroup_w_in(w_in[i]))
        r, dec, kp, v, kn, beta, g, bonus = _rwkv_prep(
            u, seq, mu_shift[i], w0[i], w_w2[i], a0[i], w_a2[i], w_g2[i], k_k[i], k_a[i], r_k[i])
        ys = _rwkv_scan(kn, dec, kp, beta, r, v, bsz, seq)
        ym = _mlstm(u, bsz, seq, conv_w[i], conv_b[i], i_bias[i], f_bias[i], mh_w[i])
        x1, ids, wts, cnt = _merge(alpha, ys, bonus, g, ym, u, xf, lnx_w[i], lnx_b[i], b_gate[i], w_br[i],
                                   w_bm[i], w_out[i], ln1_w[i], ln1_b[i], w_rg[i], b_rg[i], w_re[i], b_re[i])
        dest, pstart, nchunks = _moe_plan(ids, cnt, MERGE_TM)
        rows_pad = TOP_K * n + N_EXPERTS * MOE_CH
        xs = _moe_dispatch(x1, dest, pstart, nchunks, rows_pad)
        ys_sorted = _moe_experts(xs, pstart, nchunks, w_gate[i], w_up[i], w_down[i])
        xf = _final(alpha, dest, x1, ys_sorted, wts, p[i].reshape(n, -1).astype(BF16), w_pg[i], w_ple[i],
                    ln2_w[i], ln2_b[i])
    return xf.reshape(bsz, seq, d)
```

```python
import functools

import jax
import jax.numpy as jnp
from jax import lax
from jax.experimental import pallas as pl
from jax.experimental.pallas import tpu as pltpu

F32 = jnp.float32
BF16 = jnp.bfloat16
I32 = jnp.int32

R_HEADS, R_HEAD = 16, 64
R_WIDTH = R_HEADS * R_HEAD
DECAY_LORA, AAA_LORA, GATE_LORA = 64, 64, 160
LORA_COLS = DECAY_LORA + AAA_LORA + GATE_LORA
LORA_PAD = 512
R_GN_EPS = 64e-5
RWKV_COLS = 3 * R_WIDTH + LORA_COLS
M_HEADS, M_QK, M_V = 8, 64, 128
M_WIDTH = M_HEADS * M_V
CONV_K = 4
CHUNK = 128
M_NORM_EPS = 1e-6
MLSTM_COLS = 2 * M_HEADS * M_QK + 2 * M_WIDTH + 2 * M_HEADS
N_GROUPS, EXPERTS_PER_GROUP = 4, 8
N_EXPERTS = N_GROUPS * EXPERTS_PER_GROUP
TOP_K = 2
MOE_BLOCK = 128
LN_EPS = 1e-5

LANES = 128
SUBLANES = 8
MXU_DIM = 256
V7X_VMEM_BYTES = 64 * 1024 * 1024
VMEM_LIMIT = 56 * 1024 * 1024
SCAN_VMEM_LIMIT = 60 * 1024 * 1024

C_RKV = 0
C_MQK = 3072
C_MV = 4096
C_MO = 5120
C_GR = 6144
C_GM = 8192
C_LORA = 10240
C_MG = 10752
C_TOTAL = 10880
PROJ_TN = 2176
PROJ_TM = 512


def _cparams(sem, vmem=VMEM_LIMIT):
    return pltpu.CompilerParams(dimension_semantics=sem, vmem_limit_bytes=vmem)


def _sigmoid(x):
    return 1.0 / (1.0 + jnp.exp(-x))


def _softplus(x):
    return jnp.maximum(x, 0.0) + jnp.log1p(jnp.exp(-jnp.abs(x)))


def _split3(x):
    hi = x.astype(BF16)
    r1 = x - hi.astype(F32)
    mid = r1.astype(BF16)
    lo = (r1 - mid.astype(F32)).astype(BF16)
    return hi, mid, lo


def _dot(a, b):
    return jnp.dot(a, b, preferred_element_type=F32)


def _dot_nt(a, b):
    return lax.dot_general(a, b, (((1,), (1,)), ((), ())), preferred_element_type=F32)


def _dot_tn(a, b):
    return lax.dot_general(a, b, (((0,), (0,)), ((), ())), preferred_element_type=F32)


def _dot_exact_rhs(x, ones_bf16, terms=3):
    parts = _split3(x)[:terms]
    acc = _dot(parts[0], ones_bf16)
    for part in parts[1:]:
        acc = acc + _dot(part, ones_bf16)
    return acc


def _block_ones(n, group):
    r = lax.broadcasted_iota(I32, (n, n), 0) // group
    c = lax.broadcasted_iota(I32, (n, n), 1) // group
    return jnp.where(r == c, 1.0, 0.0).astype(BF16)


def _seg_sum(x, group):
    ones = _block_ones(MXU_DIM, group)
    slabs = [_dot_exact_rhs(x[:, p * MXU_DIM:(p + 1) * MXU_DIM], ones, terms=2)
             for p in range(x.shape[1] // MXU_DIM)]
    return jnp.concatenate(slabs, axis=1)


def _layer_norm(x, w, b):
    mu = jnp.mean(x, axis=-1, keepdims=True)
    xc = x - mu
    var = jnp.mean(xc * xc, axis=-1, keepdims=True)
    return xc * lax.rsqrt(var + LN_EPS) * w + b


def _proj_kernel(x_ref, w_ref, o_ref):
    o_ref[...] = _dot(x_ref[...].astype(BF16), w_ref[...])


def _proj_in(x_bf, w_bf):
    m, k = x_bf.shape
    n = w_bf.shape[1]
    return pl.pallas_call(
        _proj_kernel,
        grid=(n // PROJ_TN, m // PROJ_TM),
        in_specs=[pl.BlockSpec((PROJ_TM, k), lambda j, i: (i, 0)),
                  pl.BlockSpec((k, PROJ_TN), lambda j, i: (0, j))],
        out_specs=pl.BlockSpec((PROJ_TM, PROJ_TN), lambda j, i: (i, j)),
        out_shape=jax.ShapeDtypeStruct((m, n), F32),
        compiler_params=_cparams(("parallel", "parallel")),
        name="proj_in",
    )(x_bf, w_bf)


PREP_TM = 256


def _rwkv_prep_kernel(seq, u_ref, l_ref, up_ref, lp_ref, mu_ref, mul_ref, w0_ref, a0_ref, kk_ref, ka_ref,
                      rk_ref, ww_ref, wa_ref, wg_ref,
                      r_o, dec_o, kp_o, v_o, kn_o, beta_o, g_o, bonus_o):
    i = pl.program_id(0)
    tm = u_ref.shape[0]
    first = (i * tm) % seq == 0
    row = lax.broadcasted_iota(I32, (tm, 1), 0)

    def shift(u, prev8):
        prev_row = jnp.where(first, 0.0, prev8[SUBLANES - 1:SUBLANES, :])
        return jnp.where(row == 0, prev_row, pltpu.roll(u, 1, 0))

    u = u_ref[...]
    z = u + mu_ref[...] * (shift(u, up_ref[...]) - u)
    lo = l_ref[...]
    zl = lo + mul_ref[...] * (shift(lo, lp_ref[...]) - lo)

    r = z[:, 0:R_WIDTH]
    k = z[:, R_WIDTH:2 * R_WIDTH]
    v = z[:, 2 * R_WIDTH:3 * R_WIDTH]
    w_pre = w0_ref[...] + _dot(jnp.tanh(zl).astype(BF16), ww_ref[...])
    w = -_softplus(-w_pre) - 0.5
    dec = jnp.exp(-jnp.exp(w))
    a = _sigmoid(a0_ref[...] + _dot(zl.astype(BF16), wa_ref[...]))
    g = _dot(_sigmoid(zl).astype(BF16), wg_ref[...])

    kk = k * kk_ref[...]
    nrm = jnp.sqrt(_seg_sum(kk * kk, R_HEAD))
    kn = kk / jnp.maximum(nrm, 1e-12)
    kp = k * (1.0 + (a - 1.0) * ka_ref[...])
    bonus = _seg_sum(r * kp * rk_ref[...], R_HEAD) * v

    r_o[...] = r
    dec_o[...] = dec
    kp_o[...] = kp
    v_o[...] = v
    kn_o[...] = kn
    beta_o[...] = kn * a
    g_o[...] = g
    bonus_o[...] = bonus


def _rwkv_prep(u, seq, mu, w0, w_w2, a0, w_a2, w_g2, k_k, k_a, r_k):
    n = u.shape[0]
    tm = PREP_TM
    mu_rkv = mu[None, :3 * R_WIDTH]
    mu_lora = jnp.zeros((1, LORA_PAD), F32).at[0, :LORA_COLS].set(mu[3 * R_WIDTH:])
    ww = jnp.zeros((LORA_PAD, R_WIDTH), BF16).at[0:DECAY_LORA].set(w_w2.astype(BF16))
    wa = jnp.zeros((LORA_PAD, R_WIDTH), BF16).at[DECAY_LORA:DECAY_LORA + AAA_LORA].set(w_a2.astype(BF16))
    wg = jnp.zeros((LORA_PAD, R_WIDTH), BF16).at[DECAY_LORA + AAA_LORA:LORA_COLS].set(w_g2.astype(BF16))
    row = lambda t: t.reshape(1, R_WIDTH)
    prev_blk = lambda i: jnp.maximum(i * (tm // SUBLANES) - 1, 0)
    const = lambda shape: pl.BlockSpec(shape, lambda i: (0, 0))
    nat = jax.ShapeDtypeStruct((n, R_WIDTH), F32)
    return pl.pallas_call(
        functools.partial(_rwkv_prep_kernel, seq),
        grid=(n // tm,),
        in_specs=[pl.BlockSpec((tm, 3 * R_WIDTH), lambda i: (i, C_RKV // (3 * R_WIDTH))),
                  pl.BlockSpec((tm, LORA_PAD), lambda i: (i, C_LORA // LORA_PAD)),
                  pl.BlockSpec((SUBLANES, 3 * R_WIDTH), lambda i: (prev_blk(i), C_RKV // (3 * R_WIDTH))),
                  pl.BlockSpec((SUBLANES, LORA_PAD), lambda i: (prev_blk(i), C_LORA // LORA_PAD)),
                  const((1, 3 * R_WIDTH)), const((1, LORA_PAD)),
                  const((1, R_WIDTH)), const((1, R_WIDTH)), const((1, R_WIDTH)), const((1, R_WIDTH)),
                  const((1, R_WIDTH)),
                  const((LORA_PAD, R_WIDTH)), const((LORA_PAD, R_WIDTH)), const((LORA_PAD, R_WIDTH))],
        out_specs=[pl.BlockSpec((tm, R_WIDTH), lambda i: (i, 0))] * 8,
        out_shape=[nat] * 8,
        compiler_params=_cparams(("parallel",)),
        name="rwkv_prep",
    )(u, u, u, u, mu_rkv, mu_lora, row(w0), row(a0), row(k_k), row(k_a), row(r_k), ww, wa, wg)


SCAN_TB = LANES // 2
SCAN_OUT = 2
SCAN_V = R_HEAD // 2
SCAN_ACC = 1
SCAN_KOPS = 5
SCAN_PAIRS = ((0, 1), (2, 3), (4, 5))
SCAN_HP = R_WIDTH // LANES
SCAN_STEPS1 = 2
SCAN_STEPS2 = 7
SCAN_KEYS_PER_TRIP = 8
SCAN_TRIPS2 = R_HEAD // SCAN_KEYS_PER_TRIP
Z_PITCH = R_HEAD + SUBLANES
V_PITCH = SCAN_V + SUBLANES
RELAYOUT_UNROLL = 8


def _rwkv_scan_kernel(kn_hbm, dec_hbm, kp_hbm, beta_hbm, r_hbm, v_hbm, y_ref,
                      s_ref, stage, z_ref, xs_ref, vs_ref, ys_ref, sem):
    i = pl.program_id(0)
    nblk = pl.num_programs(0)
    nb = stage.shape[2]
    tb = SCAN_TB
    slab = R_HEADS * Z_PITCH
    zrows = nb * slab
    cur = i % 2
    srcs = (kn_hbm, dec_hbm, kp_hbm, beta_hbm, r_hbm, v_hbm)

    def fetch(op, blk, par):
        return pltpu.make_async_copy(srcs[op].at[:, pl.ds(blk * tb, tb), :], stage.at[par, op], sem.at[par])

    def fetch_start(blk, par):
        for op in range(len(srcs)):
            fetch(op, blk, par).start()

    def fetch_wait(blk, par):
        for op in range(len(srcs)):
            fetch(op, blk, par).wait()

    def to_time_on_lanes(par, b):
        for p, (oa, ob) in enumerate(SCAN_PAIRS):
            for hp in range(SCAN_HP):
                cols = slice(hp * LANES, (hp + 1) * LANES)
                tt = jnp.concatenate([stage[par, oa, b, :, cols], stage[par, ob, b, :, cols]], axis=0).T
                for hh in range(2):
                    row0 = pl.multiple_of(p * zrows + (b * R_HEADS + 2 * hp + hh) * Z_PITCH, SUBLANES)
                    z_ref[pl.ds(row0, R_HEAD), :] = tt[hh * R_HEAD:(hh + 1) * R_HEAD]

    def head_rows(p, b, c):
        return z_ref[pl.ds(p * zrows + b * slab + c, R_HEADS, stride=Z_PITCH), :]

    def key_unit(buf, p, k):
        oa, ob = SCAN_PAIRS[p]
        rows = [head_rows(p, b, k) for b in range(nb)]
        tt = jnp.concatenate(rows + rows, axis=0).T
        xs_ref[buf, oa, k] = tt[0:tb]
        if ob < SCAN_KOPS:
            xs_ref[buf, ob, k] = tt[tb:2 * tb]

    def value_unit(buf, vp):
        rows = [head_rows(len(SCAN_PAIRS) - 1, b, vh * SCAN_V + vp) for vh in range(2) for b in range(nb)]
        tt = jnp.concatenate(rows, axis=0).T
        vs_ref[pl.ds(buf * tb * V_PITCH + vp, tb, stride=V_PITCH), :] = tt[tb:2 * tb]

    def chain_units(buf, q):
        units = [functools.partial(key_unit, buf, p, SCAN_KEYS_PER_TRIP * q + kk)
                 for kk in range(SCAN_KEYS_PER_TRIP) for p in range(len(SCAN_PAIRS))]
        units += [functools.partial(value_unit, buf, (SCAN_KEYS_PER_TRIP // 2) * q + vv)
                  for vv in range(SCAN_KEYS_PER_TRIP // 2)]
        return units

    @pl.when(i == 0)
    def _():
        s_ref[...] = jnp.zeros(s_ref.shape, F32)
        fetch_start(0, 0)
        fetch_wait(0, 0)

        @pl.when(nblk > 1)
        def _():
            fetch_start(1, 1)

        _each(nb, lambda b: to_time_on_lanes(0, b))
        _each(SCAN_TRIPS2, lambda q: [unit() for unit in chain_units(0, q)])

    @pl.when(i + 1 < nblk)
    def _():
        fetch_wait(i + 1, 1 - cur)

    @pl.when(i + 2 < nblk)
    def _():
        fetch_start(i + 2, cur)

    def total(acc):
        while len(acc) > 1:
            acc = [acc[j] + acc[j + 1] for j in range(0, len(acc), 2)]
        return acc[0]

    def add_term(acc, k, term):
        acc[k % SCAN_ACC] = term if acc[k % SCAN_ACC] is None else acc[k % SCAN_ACC] + term

    def key_row(op, k, t):
        return xs_ref[cur, op, k, pl.ds(t, 1), :]

    acc = [None] * SCAN_ACC
    for k in range(R_HEAD):
        add_term(acc, k, s_ref[k] * xs_ref[cur, 0, k, 0:1, :])

    def step(t, s_kk):
        v_t = vs_ref[pl.ds(pl.multiple_of((cur * tb + t) * V_PITCH, SUBLANES), SCAN_V), :]
        t_next = jnp.minimum(t + 1, tb - 1)
        acc_y = [None] * SCAN_ACC
        acc_s = [None] * SCAN_ACC
        for k in range(R_HEAD):
            s_new = (s_ref[k] * key_row(1, k, t)
                     + (v_t * key_row(2, k, t) - s_kk * key_row(3, k, t)))
            s_ref[k] = s_new
            add_term(acc_y, k, s_new * key_row(4, k, t))
            add_term(acc_s, k, s_new * key_row(0, k, t_next))
        out_row = pl.multiple_of(((i % SCAN_OUT) * tb + t) * V_PITCH, SUBLANES)
        ys_ref[pl.ds(out_row, SCAN_V), :] = total(acc_y)
        return total(acc_s)

    def first_part(b, s_kk):
        for j in range(SCAN_STEPS1):
            s_kk = step(b * SCAN_STEPS1 + j, s_kk)
        to_time_on_lanes(1 - cur, b)
        return s_kk

    def second_part(q, s_kk):
        for j in range(SCAN_STEPS2):
            s_kk = step(nb * SCAN_STEPS1 + q * SCAN_STEPS2 + j, s_kk)
        for unit in chain_units(1 - cur, q):
            unit()
        return s_kk

    s_kk = lax.fori_loop(0, nb, first_part, total(acc))
    lax.fori_loop(0, SCAN_TRIPS2, second_part, s_kk)

    @pl.when(i % SCAN_OUT == SCAN_OUT - 1)
    def _():
        steps = SCAN_OUT * tb

        def out_rows(vp, carry):
            tt = ys_ref[pl.ds(vp, steps, stride=V_PITCH), :].T
            for vh in range(2):
                for b in range(nb):
                    g = vh * nb + b
                    z_ref[pl.ds(b * slab + vh * SCAN_V + vp, R_HEADS, stride=Z_PITCH), :] = \
                        tt[g * R_HEADS:(g + 1) * R_HEADS]
            return carry
        lax.fori_loop(0, SCAN_V, out_rows, 0, unroll=RELAYOUT_UNROLL)
        for b in range(nb):
            for hp in range(SCAN_HP):
                pair = [z_ref[pl.ds((b * R_HEADS + 2 * hp + hh) * Z_PITCH, R_HEAD), :] for hh in range(2)]
                y_ref[b, :, hp * LANES:(hp + 1) * LANES] = jnp.concatenate(pair, axis=0).T


def _rwkv_scan(kn, dec, kp, beta, r, v, bsz, seq):
    shape3 = (bsz, seq, R_WIDTH)
    ops = [a.reshape(shape3) for a in (kn, dec, kp, beta, r, v)]
    assert SCAN_TB == bsz * SCAN_STEPS1 + SCAN_TRIPS2 * SCAN_STEPS2 and seq % (SCAN_OUT * SCAN_TB) == 0
    y = pl.pallas_call(
        _rwkv_scan_kernel,
        grid=(seq // SCAN_TB,),
        in_specs=[pl.BlockSpec(memory_space=pl.ANY)] * len(ops),
        out_specs=pl.BlockSpec((bsz, SCAN_OUT * SCAN_TB, R_WIDTH), lambda i: (0, i // SCAN_OUT, 0)),
        out_shape=jax.ShapeDtypeStruct(shape3, F32),
        scratch_shapes=[pltpu.VMEM((R_HEAD, SCAN_V, LANES), F32),
                        pltpu.VMEM((2, len(ops), bsz, SCAN_TB, R_WIDTH), F32),
                        pltpu.VMEM((len(SCAN_PAIRS) * bsz * R_HEADS * Z_PITCH, LANES), F32),
                        pltpu.VMEM((2, SCAN_KOPS, R_HEAD, SCAN_TB, LANES), F32),
                        pltpu.VMEM((2 * SCAN_TB * V_PITCH, LANES), F32),
                        pltpu.VMEM((SCAN_OUT * SCAN_TB * V_PITCH, LANES), F32),
                        pltpu.SemaphoreType.DMA((2,))],
        compiler_params=_cparams(("arbitrary",), vmem=SCAN_VMEM_LIMIT),
        name="rwkv_scan",
    )(*ops)
    return y.reshape(bsz * seq, R_WIDTH)


M_AUG = 2 * M_V


def _mlstm_kernel(qk_ref, v_ref, o_ref, g_ref, cw_ref, cb_ref, gb_ref, mh_ref, y_ref,
                  ext_ref, c_ref, m_ref):
    c_idx = pl.program_id(0)
    L = CHUNK
    nb = qk_ref.shape[0]

    @pl.when(c_idx == 0)
    def _():
        ext_ref[:, 0:SUBLANES, :] = jnp.zeros((nb, SUBLANES, ext_ref.shape[2]), F32)
        c_ref[...] = jnp.zeros(c_ref.shape, F32)
        m_ref[...] = jnp.full(m_ref.shape, -jnp.inf, F32)

    lane = lax.broadcasted_iota(I32, (L, LANES), 1)
    ti = lax.broadcasted_iota(I32, (L, L), 0)
    si = lax.broadcasted_iota(I32, (L, L), 1)
    causal = si <= ti
    ltri = jnp.where(causal, 1.0, 0.0).astype(BF16)
    utri = jnp.where(ti <= si, 1.0, 0.0).astype(BF16)
    ones_col = jnp.where(lane == 0, 1.0, 0.0).astype(BF16)

    q_all, k_all, gcols, grows, acols, arows = [], [], [], [], [], []
    for b in range(nb):
        ext_ref[b, SUBLANES:SUBLANES + L, :] = qk_ref[b]
        conv = cb_ref[...]
        for j in range(CONV_K):
            off = SUBLANES - (CONV_K - 1) + j
            conv = conv + cw_ref[j:j + 1, :] * ext_ref[b, off:off + L, :]
        ext_ref[b, 0:SUBLANES, :] = ext_ref[b, L:L + SUBLANES, :]
        qk = conv * _sigmoid(conv)
        q_all.append(qk[:, :M_HEADS * M_QK].astype(BF16))
        k_all.append(qk[:, M_HEADS * M_QK:] * (M_QK ** -0.5))
        gpre = g_ref[b] + gb_ref[...]
        gcols.append(jnp.where(lane < M_HEADS, gpre, -_softplus(-gpre)))
        grows.append(gcols[b].T)
        acols.append(_dot_exact_rhs_left(ltri, gcols[b]))
        arows.append(_dot_exact_rhs(grows[b], utri))

    chains = [(b, h) for b in range(nb) for h in range(M_HEADS)]
    every = range(len(chains))
    a_col = [acols[b][:, M_HEADS + h:M_HEADS + h + 1] for b, h in chains]
    i_col = [gcols[b][:, h:h + 1] for b, h in chains]
    m_st = [m_ref[u:u + 1, 0:1] for u in every]
    d = [jnp.where(causal, a_col[u] - arows[b][M_HEADS + h:M_HEADS + h + 1, :] + grows[b][h:h + 1, :], -jnp.inf)
         for u, (b, h) in enumerate(chains)]
    dmax = [jnp.max(d[u], axis=-1, keepdims=True) for u in every]
    inter = [a_col[u] + m_st[u] for u in every]
    m_t = [jnp.maximum(inter[u], dmax[u]) for u in every]
    q = [q_all[b][:, h * M_QK:(h + 1) * M_QK] for b, h in chains]
    k = [k_all[b][:, h * M_QK:(h + 1) * M_QK] for b, h in chains]
    v_aug = [jnp.concatenate([v_ref[b, :, h * M_V:(h + 1) * M_V].astype(BF16), ones_col], axis=1)
             for b, h in chains]
    s = [_dot_nt(q[u], k[u].astype(BF16)) * jnp.exp(d[u] - m_t[u]) for u in every]
    ie = [jnp.exp(inter[u] - m_t[u]) for u in every]
    tot = [ie[u] * _dot(q[u], c_ref[u].astype(BF16)) + _dot(s[u].astype(BF16), v_aug[u]) for u in every]
    h_c = [tot[u][:, :M_V] / jnp.maximum(jnp.abs(tot[u][:, M_V:M_V + 1]), jnp.exp(-m_t[u])) for u in every]

    a_tot = [a_col[u][L - 1:L, :] for u in every]
    gl = [a_tot[u] - a_col[u] + i_col[u] for u in every]
    m_new = [jnp.maximum(a_tot[u] + m_st[u], jnp.max(gl[u], axis=0, keepdims=True)) for u in every]
    kg = [(k[u] * jnp.exp(gl[u] - m_new[u])).astype(BF16) for u in every]
    for u in every:
        c_ref[u] = jnp.exp(a_tot[u] + m_st[u] - m_new[u]) * c_ref[u] + _dot_tn(kg[u], v_aug[u])
        m_ref[u:u + 1, :] = jnp.broadcast_to(m_new[u], (1, LANES))

    mu = [jnp.mean(h_c[u], axis=-1, keepdims=True) for u in every]
    hc = [h_c[u] - mu[u] for u in every]
    var = [jnp.mean(hc[u] * hc[u], axis=-1, keepdims=True) for u in every]
    for u, (b, h) in enumerate(chains):
        hn = hc[u] * lax.rsqrt(var[u] + M_NORM_EPS) * mh_ref[:, h * M_V:(h + 1) * M_V]
        y_ref[b, :, h * M_V:(h + 1) * M_V] = _sigmoid(o_ref[b, :, h * M_V:(h + 1) * M_V]) * hn


def _dot_exact_rhs_left(ones_bf16, x):
    hi, mid, lo = _split3(x)
    return _dot(ones_bf16, hi) + _dot(ones_bf16, mid) + _dot(ones_bf16, lo)


def _mlstm(u, bsz, seq, conv_w, conv_b, i_bias, f_bias, mh_w):
    nc = seq // CHUNK
    w = M_WIDTH
    gbias = jnp.zeros((1, LANES), F32).at[0, :M_HEADS].set(i_bias).at[0, M_HEADS:2 * M_HEADS].set(f_bias)
    u3 = u.reshape(bsz, seq, u.shape[1])
    blk = lambda col: pl.BlockSpec((bsz, CHUNK, w), lambda c: (0, c, col // w))
    const = lambda shape: pl.BlockSpec(shape, lambda c: (0, 0))
    y = pl.pallas_call(
        _mlstm_kernel,
        grid=(nc,),
        in_specs=[blk(C_MQK), blk(C_MV), blk(C_MO),
                  pl.BlockSpec((bsz, CHUNK, LANES), lambda c: (0, c, C_MG // LANES)),
                  const((CONV_K, w)), const((1, w)), const((1, LANES)), const((1, w))],
        out_specs=pl.BlockSpec((bsz, CHUNK, w), lambda c: (0, c, 0)),
        out_shape=jax.ShapeDtypeStruct((bsz, seq, w), F32),
        scratch_shapes=[pltpu.VMEM((bsz, CHUNK + SUBLANES, w), F32),
                        pltpu.VMEM((bsz * M_HEADS, M_QK, M_AUG), F32),
                        pltpu.VMEM((bsz * M_HEADS, LANES), F32)],
        compiler_params=_cparams(("arbitrary",)),
        name="mlstm",
    )(u3, u3, u3, u3, conv_w, conv_b[None, :], gbias, mh_w[None, :])
    return y.reshape(bsz * seq, w)


MERGE_TM = 256
C_ROUTE_G = N_EXPERTS


def _merge_kernel(alpha, ys_ref, bonus_ref, g_ref, ym_ref, gr_ref, gm_ref, x_ref,
                  lnxw_ref, lnxb_ref, bgr_ref, bgm_ref, wbr_ref, wbm_ref, wout_ref, l1w_ref, l1b_ref,
                  wrh_ref, wrl_ref, br_ref, x1_o, ids_o, wts_o, cnt_o):
    ys = ys_ref[...]
    mu = _seg_sum(ys, R_HEAD) * (1.0 / R_HEAD)
    yc = ys - mu
    var = _seg_sum(yc * yc, R_HEAD) * (1.0 / R_HEAD)
    y = yc * lax.rsqrt(var + R_GN_EPS) * lnxw_ref[...] + lnxb_ref[...]
    y_r = (y + bonus_ref[...]) * g_ref[...]
    br = _dot(y_r.astype(BF16), wbr_ref[...])
    bm = _dot(ym_ref[...].astype(BF16), wbm_ref[...])
    mix_in = _sigmoid(gr_ref[...] + bgr_ref[...]) * br + _sigmoid(gm_ref[...] + bgm_ref[...]) * bm
    mix = _dot(mix_in.astype(BF16), wout_ref[...])
    x1 = _layer_norm(alpha * x_ref[...] + mix, l1w_ref[...], l1b_ref[...])
    x1_o[...] = x1

    xh = x1.astype(BF16)
    xl = (x1 - xh.astype(F32)).astype(BF16)
    logits = (_dot(xh, wrh_ref[...]) + (_dot(xh, wrl_ref[...]) + _dot(xl, wrh_ref[...]))) + br_ref[...]
    tm = logits.shape[0]
    lane_i = lax.broadcasted_iota(I32, (tm, LANES), 1)
    lane = lane_i.astype(F32)
    group_of_lane = (lane_i // EXPERTS_PER_GROUP).astype(F32)
    big = float(LANES)
    neg = -jnp.inf
    lg = jnp.where((lane_i >= C_ROUTE_G) & (lane_i < C_ROUTE_G + N_GROUPS), logits, neg)
    gmax = jnp.max(lg, axis=-1, keepdims=True)
    gsel = jnp.min(jnp.where(lg == gmax, lane - C_ROUTE_G, big), axis=-1, keepdims=True)
    g_w = 1.0 / jnp.sum(jnp.exp(lg - gmax), axis=-1, keepdims=True)
    le = jnp.where((lane_i < N_EXPERTS) & (group_of_lane == gsel), logits, neg)
    m1 = jnp.max(le, axis=-1, keepdims=True)
    i1 = jnp.min(jnp.where(le == m1, lane, big), axis=-1, keepdims=True)
    le2 = jnp.where(lane == i1, neg, le)
    m2 = jnp.max(le2, axis=-1, keepdims=True)
    i2 = jnp.min(jnp.where(le2 == m2, lane, big), axis=-1, keepdims=True)
    e2 = jnp.exp(m2 - m1)
    w1 = g_w / (1.0 + e2)
    w2 = g_w * e2 / (1.0 + e2)
    wts_o[...] = jnp.where(lane_i == 0, w1, jnp.where(lane_i == 1, w2, 0.0))

    oh1 = jnp.where(lane == i1, 1.0, 0.0)
    oh2 = jnp.where(lane == i2, 1.0, 0.0)
    ri = lax.broadcasted_iota(I32, (tm, tm), 0)
    ci = lax.broadcasted_iota(I32, (tm, tm), 1)
    lstrict = jnp.where(ci < ri, 1.0, 0.0).astype(BF16)
    tot1 = jnp.sum(oh1, axis=0, keepdims=True)
    tot2 = jnp.sum(oh2, axis=0, keepdims=True)
    rank1 = jnp.sum(_dot(lstrict, oh1.astype(BF16)) * oh1, axis=-1, keepdims=True)
    rank2 = jnp.sum((_dot(lstrict, oh2.astype(BF16)) + tot1) * oh2, axis=-1, keepdims=True)
    ids = jnp.where(lane_i == 0, i1, jnp.where(lane_i == 1, i2, jnp.where(lane_i == 2, rank1,
                                                                          jnp.where(lane_i == 3, rank2, 0.0))))
    ids_o[...] = ids.astype(I32)
    cnt_o[...] = jnp.broadcast_to(tot1 + tot2, cnt_o.shape).astype(I32)


def _merge(alpha, ys, bonus, g, ym, u, x, lnx_w, lnx_b, b_gate, w_br, w_bm, w_out, ln1_w, ln1_b,
           w_rg, b_rg, w_re, b_re):
    n, d = x.shape
    tm = MERGE_TM
    wr = jnp.zeros((d, LANES), F32).at[:, :N_EXPERTS].set(w_re).at[:, C_ROUTE_G:C_ROUTE_G + N_GROUPS].set(w_rg)
    wr_hi = wr.astype(BF16)
    wr_lo = (wr - wr_hi.astype(F32)).astype(BF16)
    b_r = jnp.zeros((1, LANES), F32).at[0, :N_EXPERTS].set(b_re).at[0, C_ROUTE_G:C_ROUTE_G + N_GROUPS].set(b_rg)
    tile = lambda w_: pl.BlockSpec((tm, w_), lambda i: (i, 0))
    const = lambda shape: pl.BlockSpec(shape, lambda i: (0, 0))
    return pl.pallas_call(
        functools.partial(_merge_kernel, alpha),
        grid=(n // tm,),
        in_specs=[tile(R_WIDTH), tile(R_WIDTH), tile(R_WIDTH), tile(M_WIDTH),
                  pl.BlockSpec((tm, d), lambda i: (i, C_GR // d)),
                  pl.BlockSpec((tm, d), lambda i: (i, C_GM // d)),
                  tile(d),
                  const((1, R_WIDTH)), const((1, R_WIDTH)), const((1, d)), const((1, d)),
                  const((R_WIDTH, d)), const((M_WIDTH, d)), const((d, d)), const((1, d)), const((1, d)),
                  const((d, LANES)), const((d, LANES)), const((1, LANES))],
        out_specs=[tile(d), tile(LANES), tile(LANES), pl.BlockSpec((SUBLANES, LANES), lambda i: (i, 0))],
        out_shape=[jax.ShapeDtypeStruct((n, d), F32), jax.ShapeDtypeStruct((n, LANES), I32),
                   jax.ShapeDtypeStruct((n, LANES), F32),
                   jax.ShapeDtypeStruct((n // tm * SUBLANES, LANES), I32)],
        compiler_params=_cparams(("parallel",)),
        name="merge_ln1_router",
    )(ys, bonus, g, ym, u, u, x, lnx_w[None, :], lnx_b[None, :], b_gate[None, :d], b_gate[None, d:],
      w_br.astype(BF16), w_bm.astype(BF16), w_out.astype(BF16), ln1_w[None, :], ln1_b[None, :],
      wr_hi, wr_lo, b_r)


MOE_CH = 256
DISPATCH_TM = 256


ROW_DMA_UNROLL = 8


def _each(count, fn, unroll=1):
    def body(r, carry):
        fn(r)
        return carry
    lax.fori_loop(0, count, body, 0, unroll=unroll)


def _each_choice(fn):
    def both(r):
        for j in range(TOP_K):
            fn(r, j)
    return both


def _zero_unused_tail(pstart, nchunks, zeros_ref, out_hbm, sem):
    ch = zeros_ref.shape[0]
    used = (pstart[N_EXPERTS - 1] + nchunks[N_EXPERTS - 1] * ch) // ch
    total = out_hbm.shape[0] // ch

    def chunk(c):
        return pltpu.make_async_copy(zeros_ref, out_hbm.at[pl.ds(pl.multiple_of(c * ch, ch), ch)], sem)

    def over_tail(fn):
        def body(c, carry):
            fn(c)
            return carry
        lax.fori_loop(used, total, body, 0)

    over_tail(lambda c: chunk(c).start())
    over_tail(lambda c: chunk(c).wait())


def _moe_dispatch_kernel(dest, pstart, nchunks, x_ref, xs_hbm, zbuf, sem_z, sem):
    i = pl.program_id(0)
    tm = x_ref.shape[0]
    ch = zbuf.shape[0]

    @pl.when(i == 0)
    def _():
        zbuf[...] = jnp.zeros(zbuf.shape, F32)

        def tail(e):
            row0 = pl.multiple_of(pstart[e] + (nchunks[e] - 1) * ch, ch)
            return pltpu.make_async_copy(zbuf, xs_hbm.at[pl.ds(row0, ch)], sem_z)

        def start(e):
            @pl.when(nchunks[e] > 0)
            def _():
                tail(e).start()

        def wait(e):
            @pl.when(nchunks[e] > 0)
            def _():
                tail(e).wait()

        _each(N_EXPERTS, start)
        _each(N_EXPERTS, wait)
        _zero_unused_tail(pstart, nchunks, zbuf, xs_hbm, sem_z)

    def row(r, j):
        return pltpu.make_async_copy(x_ref.at[pl.ds(r, 1)],
                                     xs_hbm.at[pl.ds(dest[(i * tm + r) * TOP_K + j], 1)], sem)

    _each(tm, _each_choice(lambda r, j: row(r, j).start(priority=j)), ROW_DMA_UNROLL)
    _each(tm, _each_choice(lambda r, j: row(r, j).wait()), ROW_DMA_UNROLL)


def _moe_dispatch(x1, dest, pstart, nchunks, rows_pad):
    n, d = x1.shape
    tm = DISPATCH_TM
    return pl.pallas_call(
        _moe_dispatch_kernel,
        grid_spec=pltpu.PrefetchScalarGridSpec(
            num_scalar_prefetch=3,
            grid=(n // tm,),
            in_specs=[pl.BlockSpec((tm, d), lambda i, *_: (i, 0))],
            out_specs=pl.BlockSpec(memory_space=pl.ANY),
            scratch_shapes=[pltpu.VMEM((MOE_CH, d), F32), pltpu.SemaphoreType.DMA, pltpu.SemaphoreType.DMA]),
        out_shape=jax.ShapeDtypeStruct((rows_pad, d), F32),
        compiler_params=_cparams(("arbitrary",)),
        name="moe_dispatch",
    )(dest, pstart, nchunks, x1)


def _moe_expert_kernel(pstart, nchunks, xs_hbm, wg_ref, wu_ref, wd_ref, ys_hbm,
                       xbuf, ybuf, wgb, wub, wdb, sem_in, sem_out):
    e = pl.program_id(0)
    ch = xbuf.shape[1]
    nc = nchunks[e]
    g0 = pstart[e] // ch
    total = (pstart[N_EXPERTS - 1] + nchunks[N_EXPERTS - 1] * ch) // ch

    def rows(g):
        return pl.ds(pl.multiple_of(g * ch, ch), ch)

    def load(g):
        return pltpu.make_async_copy(xs_hbm.at[rows(g)], xbuf.at[g % 2], sem_in.at[g % 2])

    def store(g):
        return pltpu.make_async_copy(ybuf.at[g % 2], ys_hbm.at[rows(g)], sem_out.at[g % 2])

    @pl.when(nc > 0)
    def _():
        @pl.when(g0 == 0)
        def _():
            load(0).start()

        wgb[...] = wg_ref[0].astype(BF16)
        wub[...] = wu_ref[0].astype(BF16)
        wdb[...] = wd_ref[0].astype(BF16)

        def chunk(c, carry):
            g = g0 + c
            load(g).wait()

            @pl.when(g + 1 < total)
            def _():
                load(g + 1).start()

            @pl.when(g >= 2)
            def _():
                store(g - 2).wait()

            xb = xbuf[g % 2].astype(BF16)
            gate = _dot(xb, wgb[...])
            hb = gate * _sigmoid(gate) * _dot(xb, wub[...])
            ybuf[g % 2] = _dot(hb.astype(BF16), wdb[...])
            store(g).start()
            return carry

        lax.fori_loop(0, nc, chunk, 0)

    @pl.when(e == pl.num_programs(0) - 1)
    def _():
        @pl.when(total >= 2)
        def _():
            store(total - 2).wait()

        @pl.when(total >= 1)
        def _():
            store(total - 1).wait()

        ybuf[0] = jnp.zeros(ybuf.shape[1:], F32)
        _zero_unused_tail(pstart, nchunks, ybuf.at[0], ys_hbm, sem_out.at[0])


def _moe_experts(xs, pstart, nchunks, w_gate, w_up, w_down):
    rows_pad, d = xs.shape
    de = w_gate.shape[-1]
    wspec = lambda shape: pl.BlockSpec(shape, lambda e, *_: (e, 0, 0))
    return pl.pallas_call(
        _moe_expert_kernel,
        grid_spec=pltpu.PrefetchScalarGridSpec(
            num_scalar_prefetch=2,
            grid=(N_EXPERTS,),
            in_specs=[pl.BlockSpec(memory_space=pl.ANY),
                      wspec((1, d, de)), wspec((1, d, de)), wspec((1, de, d))],
            out_specs=pl.BlockSpec(memory_space=pl.ANY),
            scratch_shapes=[pltpu.VMEM((2, MOE_CH, d), F32), pltpu.VMEM((2, MOE_CH, d), F32),
                            pltpu.VMEM((d, de), BF16), pltpu.VMEM((d, de), BF16), pltpu.VMEM((de, d), BF16),
                            pltpu.SemaphoreType.DMA((2,)), pltpu.SemaphoreType.DMA((2,))]),
        out_shape=jax.ShapeDtypeStruct((rows_pad, d), F32),
        compiler_params=_cparams(("arbitrary",)),
        name="moe_experts",
    )(pstart, nchunks, xs, w_gate, w_up, w_down)


def _moe_plan(ids, cnt, tm):
    n = ids.shape[0]
    tile_cnt = cnt[::SUBLANES, :N_EXPERTS]
    counts = jnp.sum(tile_cnt, axis=0)
    padded = ((counts + MOE_CH - 1) // MOE_CH) * MOE_CH
    pstart = jnp.cumsum(padded) - padded
    tile_base = pstart[None, :] + jnp.cumsum(tile_cnt, axis=0) - tile_cnt
    eid = ids[:, 0:TOP_K]
    rank = ids[:, TOP_K:2 * TOP_K]
    base_of_tok = jnp.repeat(tile_base, tm, axis=0)[:, None, :]
    chosen = eid[:, :, None] == jnp.arange(N_EXPERTS, dtype=I32)[None, None, :]
    dest = jnp.sum(jnp.where(chosen, base_of_tok, 0), axis=-1) + rank
    return dest.reshape(-1).astype(I32), pstart.astype(I32), (padded // MOE_CH).astype(I32)


FINAL_TM = 256


def _final_kernel(alpha, dest, x1_ref, wts_ref, p_ref, wpg_ref, wple_ref, l2w_ref, l2b_ref, ys_hbm, o_ref,
                  ybuf, sem):
    i = pl.program_id(0)
    tm = x1_ref.shape[0]
    slot = i % 2

    def row(tile, buf, r, j):
        return pltpu.make_async_copy(ys_hbm.at[pl.ds(dest[(tile * tm + r) * TOP_K + j], 1)],
                                     ybuf.at[buf, j, pl.ds(r, 1)], sem.at[buf])

    def gather(tile, buf):
        _each(tm, _each_choice(lambda r, j: row(tile, buf, r, j).start(priority=j)), ROW_DMA_UNROLL)

    @pl.when(i == 0)
    def _():
        gather(0, 0)

    @pl.when(i + 1 < pl.num_programs(0))
    def _():
        gather(i + 1, 1 - slot)

    x1 = x1_ref[...]
    ple = _sigmoid(_dot(x1.astype(BF16), wpg_ref[...])) * _dot(p_ref[...], wple_ref[...])
    _each(tm, _each_choice(lambda r, j: row(i, slot, r, j).wait()), ROW_DMA_UNROLL)
    moe = ybuf[slot, 0] * wts_ref[:, 0:1] + ybuf[slot, 1] * wts_ref[:, 1:2]
    o_ref[...] = _layer_norm(alpha * x1 + moe + ple, l2w_ref[...], l2b_ref[...])


def _final(alpha, dest, x1, ys, wts, p_bf, w_pg, w_ple, ln2_w, ln2_b):
    n, d = x1.shape
    tm = FINAL_TM
    tile = lambda w_: pl.BlockSpec((tm, w_), lambda i, *_: (i, 0))
    const = lambda shape: pl.BlockSpec(shape, lambda i, *_: (0, 0))
    return pl.pallas_call(
        functools.partial(_final_kernel, alpha),
        grid_spec=pltpu.PrefetchScalarGridSpec(
            num_scalar_prefetch=1,
            grid=(n // tm,),
            in_specs=[tile(d), tile(LANES), tile(p_bf.shape[1]),
                      const((d, d)), const((p_bf.shape[1], d)), const((1, d)), const((1, d)),
                      pl.BlockSpec(memory_space=pl.ANY)],
            out_specs=tile(d),
            scratch_shapes=[pltpu.VMEM((2, TOP_K, tm, d), F32), pltpu.SemaphoreType.DMA((2,))]),
        out_shape=jax.ShapeDtypeStruct((n, d), F32),
        compiler_params=_cparams(("arbitrary",)),
        name="final_ln2",
    )(dest, x1, wts, p_bf, w_pg.astype(BF16), w_ple.astype(BF16), ln2_w[None, :], ln2_b[None, :], ys)


REGROUP_TM = 256


def _regroup_kernel(w_ref, o_ref):
    tm = w_ref.shape[0]
    m0 = RWKV_COLS
    mg = m0 + 2 * M_HEADS * M_QK + M_WIDTH
    cols = lambda a, n: w_ref[:, a:a + n].astype(BF16)
    zeros = lambda n: jnp.zeros((tm, n), BF16)
    o_ref[:, C_RKV:C_MQK] = cols(0, 3 * R_WIDTH)
    o_ref[:, C_MQK:C_MO] = cols(m0, C_MO - C_MQK)
    o_ref[:, C_MO:C_LORA] = cols(mg + 2 * M_HEADS, C_LORA - C_MO)
    o_ref[:, C_LORA:C_MG] = jnp.concatenate([cols(3 * R_WIDTH, LORA_COLS), zeros(LORA_PAD - LORA_COLS)], axis=1)
    o_ref[:, C_MG:C_TOTAL] = jnp.concatenate([cols(mg, 2 * M_HEADS), zeros(LANES - 2 * M_HEADS)], axis=1)


def _regroup_w_in(w_layers, layer):
    _, k, n = w_layers.shape
    assert n == RWKV_COLS + MLSTM_COLS + 2 * k and C_LORA - C_MO == M_WIDTH + 2 * k
    return pl.pallas_call(
        _regroup_kernel,
        grid=(k // REGROUP_TM,),
        in_specs=[pl.BlockSpec((None, REGROUP_TM, n), lambda i: (layer, i, 0))],
        out_specs=pl.BlockSpec((REGROUP_TM, C_TOTAL), lambda i: (i, 0)),
        out_shape=jax.ShapeDtypeStruct((k, C_TOTAL), BF16),
        compiler_params=_cparams(("parallel",)),
        name="regroup_w_in",
    )(w_layers)


def kernel(x, p, w_in, mu_shift, w0, w_w2, a0, w_a2, w_g2, k_k, k_a, r_k, lnx_w, lnx_b, conv_w, conv_b,
           i_bias, f_bias, mh_w, b_gate, w_br, w_bm, w_out, ln1_w, ln1_b, w_rg, b_rg, w_re, b_re,
           w_gate, w_up, w_down, w_pg, w_ple, ln2_w, ln2_b):
    bsz, seq, d = x.shape
    depth = w_in.shape[0]
    assert bsz * R_HEADS * 2 == LANES and seq % PROJ_TM == 0 and seq % CHUNK == 0
    alpha = (2 * depth) ** 0.25
    n = bsz * seq
    xf = x.reshape(n, d)
    for i in range(depth):
        u = _proj_in(xf, _regroup_w_in(w_in, i))
        r, dec, kp, v, kn, beta, g, bonus = _rwkv_prep(
            u, seq, mu_shift[i], w0[i], w_w2[i], a0[i], w_a2[i], w_g2[i], k_k[i], k_a[i], r_k[i])
        ys = _rwkv_scan(kn, dec, kp, beta, r, v, bsz, seq)
        ym = _mlstm(u, bsz, seq, conv_w[i], conv_b[i], i_bias[i], f_bias[i], mh_w[i])
        x1, ids, wts, cnt = _merge(alpha, ys, bonus, g, ym, u, xf, lnx_w[i], lnx_b[i], b_gate[i], w_br[i],
                                   w_bm[i], w_out[i], ln1_w[i], ln1_b[i], w_rg[i], b_rg[i], w_re[i], b_re[i])
        dest, pstart, nchunks = _moe_plan(ids, cnt, MERGE_TM)
        rows_pad = TOP_K * n + N_EXPERTS * MOE_CH
        xs = _moe_dispatch(x1, dest, pstart, nchunks, rows_pad)
        ys_sorted = _moe_experts(xs, pstart, nchunks, w_gate[i], w_up[i], w_down[i])
        xf = _final(alpha, dest, x1, ys_sorted, wts, p[i].reshape(n, -1).astype(BF16), w_pg[i], w_ple[i],
                    ln2_w[i], ln2_b[i])
    return xf.reshape(bsz, seq, d)
```

```python
import functools

import jax
import jax.numpy as jnp
from jax import lax
from jax.experimental import pallas as pl
from jax.experimental.pallas import tpu as pltpu

F32 = jnp.float32
BF16 = jnp.bfloat16
I32 = jnp.int32

R_HEADS, R_HEAD = 16, 64
R_WIDTH = R_HEADS * R_HEAD
DECAY_LORA, AAA_LORA, GATE_LORA = 64, 64, 160
LORA_COLS = DECAY_LORA + AAA_LORA + GATE_LORA
LORA_PAD = 512
R_GN_EPS = 64e-5
RWKV_COLS = 3 * R_WIDTH + LORA_COLS
M_HEADS, M_QK, M_V = 8, 64, 128
M_WIDTH = M_HEADS * M_V
CONV_K = 4
CHUNK = 128
M_NORM_EPS = 1e-6
MLSTM_COLS = 2 * M_HEADS * M_QK + 2 * M_WIDTH + 2 * M_HEADS
N_GROUPS, EXPERTS_PER_GROUP = 4, 8
N_EXPERTS = N_GROUPS * EXPERTS_PER_GROUP
TOP_K = 2
MOE_BLOCK = 128
LN_EPS = 1e-5

LANES = 128
SUBLANES = 8
MXU_DIM = 256
V7X_VMEM_BYTES = 64 * 1024 * 1024
VMEM_LIMIT = 56 * 1024 * 1024
SCAN_VMEM_LIMIT = 60 * 1024 * 1024

C_RKV = 0
C_MQK = 3072
C_MV = 4096
C_MO = 5120
C_GR = 6144
C_GM = 8192
C_LORA = 10240
C_MG = 10752
C_TOTAL = 10880
PROJ_TN = 2176
PROJ_TM = 512


def _cparams(sem, vmem=VMEM_LIMIT):
    return pltpu.CompilerParams(dimension_semantics=sem, vmem_limit_bytes=vmem)


def _sigmoid(x):
    return 1.0 / (1.0 + jnp.exp(-x))


def _softplus(x):
    return jnp.maximum(x, 0.0) + jnp.log1p(jnp.exp(-jnp.abs(x)))


def _split3(x):
    hi = x.astype(BF16)
    r1 = x - hi.astype(F32)
    mid = r1.astype(BF16)
    lo = (r1 - mid.astype(F32)).astype(BF16)
    return hi, mid, lo


def _dot(a, b):
    return jnp.dot(a, b, preferred_element_type=F32)


def _dot_nt(a, b):
    return lax.dot_general(a, b, (((1,), (1,)), ((), ())), preferred_element_type=F32)


def _dot_tn(a, b):
    return lax.dot_general(a, b, (((0,), (0,)), ((), ())), preferred_element_type=F32)


def _dot_exact_rhs(x, ones_bf16, terms=3):
    parts = _split3(x)[:terms]
    acc = _dot(parts[0], ones_bf16)
    for part in parts[1:]:
        acc = acc + _dot(part, ones_bf16)
    return acc


def _block_ones(n, group):
    r = lax.broadcasted_iota(I32, (n, n), 0) // group
    c = lax.broadcasted_iota(I32, (n, n), 1) // group
    return jnp.where(r == c, 1.0, 0.0).astype(BF16)


def _seg_sum(x, group):
    ones = _block_ones(MXU_DIM, group)
    slabs = [_dot_exact_rhs(x[:, p * MXU_DIM:(p + 1) * MXU_DIM], ones, terms=2)
             for p in range(x.shape[1] // MXU_DIM)]
    return jnp.concatenate(slabs, axis=1)


def _layer_norm(x, w, b):
    mu = jnp.mean(x, axis=-1, keepdims=True)
    xc = x - mu
    var = jnp.mean(xc * xc, axis=-1, keepdims=True)
    return xc * lax.rsqrt(var + LN_EPS) * w + b


def _proj_kernel(x_ref, w_ref, o_ref):
    o_ref[...] = _dot(x_ref[...].astype(BF16), w_ref[...])


def _proj_in(x_bf, w_bf):
    m, k = x_bf.shape
    n = w_bf.shape[1]
    return pl.pallas_call(
        _proj_kernel,
        grid=(n // PROJ_TN, m // PROJ_TM),
        in_specs=[pl.BlockSpec((PROJ_TM, k), lambda j, i: (i, 0)),
                  pl.BlockSpec((k, PROJ_TN), lambda j, i: (0, j))],
        out_specs=pl.BlockSpec((PROJ_TM, PROJ_TN), lambda j, i: (i, j)),
        out_shape=jax.ShapeDtypeStruct((m, n), F32),
        compiler_params=_cparams(("parallel", "parallel")),
        name="proj_in",
    )(x_bf, w_bf)


PREP_TM = 256


def _rwkv_prep_kernel(seq, u_ref, l_ref, up_ref, lp_ref, mu_ref, mul_ref, w0_ref, a0_ref, kk_ref, ka_ref,
                      rk_ref, ww_ref, wa_ref, wg_ref,
                      r_o, dec_o, kp_o, v_o, kn_o, beta_o, g_o, bonus_o):
    i = pl.program_id(0)
    tm = u_ref.shape[0]
    first = (i * tm) % seq == 0
    row = lax.broadcasted_iota(I32, (tm, 1), 0)

    def shift(u, prev8):
        prev_row = jnp.where(first, 0.0, prev8[SUBLANES - 1:SUBLANES, :])
        return jnp.where(row == 0, prev_row, pltpu.roll(u, 1, 0))

    u = u_ref[...]
    z = u + mu_ref[...] * (shift(u, up_ref[...]) - u)
    lo = l_ref[...]
    zl = lo + mul_ref[...] * (shift(lo, lp_ref[...]) - lo)

    r = z[:, 0:R_WIDTH]
    k = z[:, R_WIDTH:2 * R_WIDTH]
    v = z[:, 2 * R_WIDTH:3 * R_WIDTH]
    w_pre = w0_ref[...] + _dot(jnp.tanh(zl).astype(BF16), ww_ref[...])
    w = -_softplus(-w_pre) - 0.5
    dec = jnp.exp(-jnp.exp(w))
    a = _sigmoid(a0_ref[...] + _dot(zl.astype(BF16), wa_ref[...]))
    g = _dot(_sigmoid(zl).astype(BF16), wg_ref[...])

    kk = k * kk_ref[...]
    nrm = jnp.sqrt(_seg_sum(kk * kk, R_HEAD))
    kn = kk / jnp.maximum(nrm, 1e-12)
    kp = k * (1.0 + (a - 1.0) * ka_ref[...])
    bonus = _seg_sum(r * kp * rk_ref[...], R_HEAD) * v

    r_o[...] = r
    dec_o[...] = dec
    kp_o[...] = kp
    v_o[...] = v
    kn_o[...] = kn
    beta_o[...] = kn * a
    g_o[...] = g
    bonus_o[...] = bonus


def _rwkv_prep(u, seq, mu, w0, w_w2, a0, w_a2, w_g2, k_k, k_a, r_k):
    n = u.shape[0]
    tm = PREP_TM
    mu_rkv = mu[None, :3 * R_WIDTH]
    mu_lora = jnp.zeros((1, LORA_PAD), F32).at[0, :LORA_COLS].set(mu[3 * R_WIDTH:])
    ww = jnp.zeros((LORA_PAD, R_WIDTH), BF16).at[0:DECAY_LORA].set(w_w2.astype(BF16))
    wa = jnp.zeros((LORA_PAD, R_WIDTH), BF16).at[DECAY_LORA:DECAY_LORA + AAA_LORA].set(w_a2.astype(BF16))
    wg = jnp.zeros((LORA_PAD, R_WIDTH), BF16).at[DECAY_LORA + AAA_LORA:LORA_COLS].set(w_g2.astype(BF16))
    row = lambda t: t.reshape(1, R_WIDTH)
    prev_blk = lambda i: jnp.maximum(i * (tm // SUBLANES) - 1, 0)
    const = lambda shape: pl.BlockSpec(shape, lambda i: (0, 0))
    nat = jax.ShapeDtypeStruct((n, R_WIDTH), F32)
    return pl.pallas_call(
        functools.partial(_rwkv_prep_kernel, seq),
        grid=(n // tm,),
        in_specs=[pl.BlockSpec((tm, 3 * R_WIDTH), lambda i: (i, C_RKV // (3 * R_WIDTH))),
                  pl.BlockSpec((tm, LORA_PAD), lambda i: (i, C_LORA // LORA_PAD)),
                  pl.BlockSpec((SUBLANES, 3 * R_WIDTH), lambda i: (prev_blk(i), C_RKV // (3 * R_WIDTH))),
                  pl.BlockSpec((SUBLANES, LORA_PAD), lambda i: (prev_blk(i), C_LORA // LORA_PAD)),
                  const((1, 3 * R_WIDTH)), const((1, LORA_PAD)),
                  const((1, R_WIDTH)), const((1, R_WIDTH)), const((1, R_WIDTH)), const((1, R_WIDTH)),
                  const((1, R_WIDTH)),
                  const((LORA_PAD, R_WIDTH)), const((LORA_PAD, R_WIDTH)), const((LORA_PAD, R_WIDTH))],
        out_specs=[pl.BlockSpec((tm, R_WIDTH), lambda i: (i, 0))] * 8,
        out_shape=[nat] * 8,
        compiler_params=_cparams(("parallel",)),
        name="rwkv_prep",
    )(u, u, u, u, mu_rkv, mu_lora, row(w0), row(a0), row(k_k), row(k_a), row(r_k), ww, wa, wg)


SCAN_TB = LANES // 2
SCAN_OUT = 2
SCAN_V = R_HEAD // 2
SCAN_ACC = 1
SCAN_KOPS = 5
SCAN_PAIRS = ((0, 1), (2, 3), (4, 5))
SCAN_HP = R_WIDTH // LANES
SCAN_STEPS1 = 2
SCAN_STEPS2 = 7
SCAN_KEYS_PER_TRIP = 8
SCAN_TRIPS2 = R_HEAD // SCAN_KEYS_PER_TRIP
Z_PITCH = R_HEAD + SUBLANES
V_PITCH = SCAN_V + SUBLANES
RELAYOUT_UNROLL = 8


def _rwkv_scan_kernel(kn_hbm, dec_hbm, kp_hbm, beta_hbm, r_hbm, v_hbm, y_ref,
                      s_ref, stage, z_ref, xs_ref, vs_ref, ys_ref, sem):
    i = pl.program_id(0)
    nblk = pl.num_programs(0)
    nb = stage.shape[2]
    tb = SCAN_TB
    slab = R_HEADS * Z_PITCH
    zrows = nb * slab
    cur = i % 2
    srcs = (kn_hbm, dec_hbm, kp_hbm, beta_hbm, r_hbm, v_hbm)

    def fetch(op, blk, par):
        return pltpu.make_async_copy(srcs[op].at[:, pl.ds(blk * tb, tb), :], stage.at[par, op], sem.at[par])

    def fetch_start(blk, par):
        for op in range(len(srcs)):
            fetch(op, blk, par).start()

    def fetch_wait(blk, par):
        for op in range(len(srcs)):
            fetch(op, blk, par).wait()

    def to_time_on_lanes(par, b):
        for p, (oa, ob) in enumerate(SCAN_PAIRS):
            for hp in range(SCAN_HP):
                cols = slice(hp * LANES, (hp + 1) * LANES)
                tt = jnp.concatenate([stage[par, oa, b, :, cols], stage[par, ob, b, :, cols]], axis=0).T
                for hh in range(2):
                    row0 = pl.multiple_of(p * zrows + (b * R_HEADS + 2 * hp + hh) * Z_PITCH, SUBLANES)
                    z_ref[pl.ds(row0, R_HEAD), :] = tt[hh * R_HEAD:(hh + 1) * R_HEAD]

    def head_rows(p, b, c):
        return z_ref[pl.ds(p * zrows + b * slab + c, R_HEADS, stride=Z_PITCH), :]

    def key_unit(buf, p, k):
        oa, ob = SCAN_PAIRS[p]
        rows = [head_rows(p, b, k) for b in range(nb)]
        tt = jnp.concatenate(rows + rows, axis=0).T
        xs_ref[buf, oa, k] = tt[0:tb]
        if ob < SCAN_KOPS:
            xs_ref[buf, ob, k] = tt[tb:2 * tb]

    def value_unit(buf, vp):
        rows = [head_rows(len(SCAN_PAIRS) - 1, b, vh * SCAN_V + vp) for vh in range(2) for b in range(nb)]
        tt = jnp.concatenate(rows, axis=0).T
        vs_ref[pl.ds(buf * tb * V_PITCH + vp, tb, stride=V_PITCH), :] = tt[tb:2 * tb]

    def chain_units(buf, q):
        units = [functools.partial(key_unit, buf, p, SCAN_KEYS_PER_TRIP * q + kk)
                 for kk in range(SCAN_KEYS_PER_TRIP) for p in range(len(SCAN_PAIRS))]
        units += [functools.partial(value_unit, buf, (SCAN_KEYS_PER_TRIP // 2) * q + vv)
                  for vv in range(SCAN_KEYS_PER_TRIP // 2)]
        return units

    @pl.when(i == 0)
    def _():
        s_ref[...] = jnp.zeros(s_ref.shape, F32)
        fetch_start(0, 0)
        fetch_wait(0, 0)

        @pl.when(nblk > 1)
        def _():
            fetch_start(1, 1)

        _each(nb, lambda b: to_time_on_lanes(0, b))
        _each(SCAN_TRIPS2, lambda q: [unit() for unit in chain_units(0, q)])

    @pl.when(i + 1 < nblk)
    def _():
        fetch_wait(i + 1, 1 - cur)

    @pl.when(i + 2 < nblk)
    def _():
        fetch_start(i + 2, cur)

    def total(acc):
        while len(acc) > 1:
            acc = [acc[j] + acc[j + 1] for j in range(0, len(acc), 2)]
        return acc[0]

    def add_term(acc, k, term):
        acc[k % SCAN_ACC] = term if acc[k % SCAN_ACC] is None else acc[k % SCAN_ACC] + term

    def key_row(op, k, t):
        return xs_ref[cur, op, k, pl.ds(t, 1), :]

    acc = [None] * SCAN_ACC
    for k in range(R_HEAD):
        add_term(acc, k, s_ref[k] * xs_ref[cur, 0, k, 0:1, :])

    def step(t, s_kk):
        v_t = vs_ref[pl.ds(pl.multiple_of((cur * tb + t) * V_PITCH, SUBLANES), SCAN_V), :]
        t_next = jnp.minimum(t + 1, tb - 1)
        acc_y = [None] * SCAN_ACC
        acc_s = [None] * SCAN_ACC
        for k in range(R_HEAD):
            s_new = (s_ref[k] * key_row(1, k, t)
                     + (v_t * key_row(2, k, t) - s_kk * key_row(3, k, t)))
            s_ref[k] = s_new
            add_term(acc_y, k, s_new * key_row(4, k, t))
            add_term(acc_s, k, s_new * key_row(0, k, t_next))
        out_row = pl.multiple_of(((i % SCAN_OUT) * tb + t) * V_PITCH, SUBLANES)
        ys_ref[pl.ds(out_row, SCAN_V), :] = total(acc_y)
        return total(acc_s)

    def first_part(b, s_kk):
        for j in range(SCAN_STEPS1):
            s_kk = step(b * SCAN_STEPS1 + j, s_kk)
        to_time_on_lanes(1 - cur, b)
        return s_kk

    def second_part(q, s_kk):
        for j in range(SCAN_STEPS2):
            s_kk = step(nb * SCAN_STEPS1 + q * SCAN_STEPS2 + j, s_kk)
        for unit in chain_units(1 - cur, q):
            unit()
        return s_kk

    s_kk = lax.fori_loop(0, nb, first_part, total(acc))
    lax.fori_loop(0, SCAN_TRIPS2, second_part, s_kk)

    @pl.when(i % SCAN_OUT == SCAN_OUT - 1)
    def _():
        steps = SCAN_OUT * tb

        def out_rows(vp, carry):
            tt = ys_ref[pl.ds(vp, steps, stride=V_PITCH), :].T
            for vh in range(2):
                for b in range(nb):
                    g = vh * nb + b
                    z_ref[pl.ds(b * slab + vh * SCAN_V + vp, R_HEADS, stride=Z_PITCH), :] = \
                        tt[g * R_HEADS:(g + 1) * R_HEADS]
            return carry
        lax.fori_loop(0, SCAN_V, out_rows, 0, unroll=RELAYOUT_UNROLL)
        for b in range(nb):
            for hp in range(SCAN_HP):
                pair = [z_ref[pl.ds((b * R_HEADS + 2 * hp + hh) * Z_PITCH, R_HEAD), :] for hh in range(2)]
                y_ref[b, :, hp * LANES:(hp + 1) * LANES] = jnp.concatenate(pair, axis=0).T


def _rwkv_scan(kn, dec, kp, beta, r, v, bsz, seq):
    shape3 = (bsz, seq, R_WIDTH)
    ops = [a.reshape(shape3) for a in (kn, dec, kp, beta, r, v)]
    assert SCAN_TB == bsz * SCAN_STEPS1 + SCAN_TRIPS2 * SCAN_STEPS2 and seq % (SCAN_OUT * SCAN_TB) == 0
    y = pl.pallas_call(
        _rwkv_scan_kernel,
        grid=(seq // SCAN_TB,),
        in_specs=[pl.BlockSpec(memory_space=pl.ANY)] * len(ops),
        out_specs=pl.BlockSpec((bsz, SCAN_OUT * SCAN_TB, R_WIDTH), lambda i: (0, i // SCAN_OUT, 0)),
        out_shape=jax.ShapeDtypeStruct(shape3, F32),
        scratch_shapes=[pltpu.VMEM((R_HEAD, SCAN_V, LANES), F32),
                        pltpu.VMEM((2, len(ops), bsz, SCAN_TB, R_WIDTH), F32),
                        pltpu.VMEM((len(SCAN_PAIRS) * bsz * R_HEADS * Z_PITCH, LANES), F32),
                        pltpu.VMEM((2, SCAN_KOPS, R_HEAD, SCAN_TB, LANES), F32),
                        pltpu.VMEM((2 * SCAN_TB * V_PITCH, LANES), F32),
                        pltpu.VMEM((SCAN_OUT * SCAN_TB * V_PITCH, LANES), F32),
                        pltpu.SemaphoreType.DMA((2,))],
        compiler_params=_cparams(("arbitrary",), vmem=SCAN_VMEM_LIMIT),
        name="rwkv_scan",
    )(*ops)
    return y.reshape(bsz * seq, R_WIDTH)


M_AUG = 2 * M_V


def _mlstm_kernel(qk_ref, v_ref, o_ref, g_ref, cw_ref, cb_ref, gb_ref, mh_ref, y_ref,
                  ext_ref, c_ref, m_ref):
    c_idx = pl.program_id(0)
    L = CHUNK
    nb = qk_ref.shape[0]

    @pl.when(c_idx == 0)
    def _():
        ext_ref[:, 0:SUBLANES, :] = jnp.zeros((nb, SUBLANES, ext_ref.shape[2]), F32)
        c_ref[...] = jnp.zeros(c_ref.shape, F32)
        m_ref[...] = jnp.full(m_ref.shape, -jnp.inf, F32)

    lane = lax.broadcasted_iota(I32, (L, LANES), 1)
    ti = lax.broadcasted_iota(I32, (L, L), 0)
    si = lax.broadcasted_iota(I32, (L, L), 1)
    causal = si <= ti
    ltri = jnp.where(causal, 1.0, 0.0).astype(BF16)
    utri = jnp.where(ti <= si, 1.0, 0.0).astype(BF16)
    ones_col = jnp.where(lane == 0, 1.0, 0.0).astype(BF16)

    q_all, k_all, gcols, grows, acols, arows = [], [], [], [], [], []
    for b in range(nb):
        ext_ref[b, SUBLANES:SUBLANES + L, :] = qk_ref[b]
        conv = cb_ref[...]
        for j in range(CONV_K):
            off = SUBLANES - (CONV_K - 1) + j
            conv = conv + cw_ref[j:j + 1, :] * ext_ref[b, off:off + L, :]
        ext_ref[b, 0:SUBLANES, :] = ext_ref[b, L:L + SUBLANES, :]
        qk = conv * _sigmoid(conv)
        q_all.append(qk[:, :M_HEADS * M_QK].astype(BF16))
        k_all.append(qk[:, M_HEADS * M_QK:] * (M_QK ** -0.5))
        gpre = g_ref[b] + gb_ref[...]
        gcols.append(jnp.where(lane < M_HEADS, gpre, -_softplus(-gpre)))
        grows.append(gcols[b].T)
        acols.append(_dot_exact_rhs_left(ltri, gcols[b]))
        arows.append(_dot_exact_rhs(grows[b], utri))

    chains = [(b, h) for b in range(nb) for h in range(M_HEADS)]
    every = range(len(chains))
    a_col = [acols[b][:, M_HEADS + h:M_HEADS + h + 1] for b, h in chains]
    i_col = [gcols[b][:, h:h + 1] for b, h in chains]
    m_st = [m_ref[u:u + 1, 0:1] for u in every]
    d = [jnp.where(causal, a_col[u] - arows[b][M_HEADS + h:M_HEADS + h + 1, :] + grows[b][h:h + 1, :], -jnp.inf)
         for u, (b, h) in enumerate(chains)]
    dmax = [jnp.max(d[u], axis=-1, keepdims=True) for u in every]
    inter = [a_col[u] + m_st[u] for u in every]
    m_t = [jnp.maximum(inter[u], dmax[u]) for u in every]
    q = [q_all[b][:, h * M_QK:(h + 1) * M_QK] for b, h in chains]
    k = [k_all[b][:, h * M_QK:(h + 1) * M_QK] for b, h in chains]
    v_aug = [jnp.concatenate([v_ref[b, :, h * M_V:(h + 1) * M_V].astype(BF16), ones_col], axis=1)
             for b, h in chains]
    s = [_dot_nt(q[u], k[u].astype(BF16)) * jnp.exp(d[u] - m_t[u]) for u in every]
    ie = [jnp.exp(inter[u] - m_t[u]) for u in every]
    tot = [ie[u] * _dot(q[u], c_ref[u].astype(BF16)) + _dot(s[u].astype(BF16), v_aug[u]) for u in every]
    h_c = [tot[u][:, :M_V] / jnp.maximum(jnp.abs(tot[u][:, M_V:M_V + 1]), jnp.exp(-m_t[u])) for u in every]

    a_tot = [a_col[u][L - 1:L, :] for u in every]
    gl = [a_tot[u] - a_col[u] + i_col[u] for u in every]
    m_new = [jnp.maximum(a_tot[u] + m_st[u], jnp.max(gl[u], axis=0, keepdims=True)) for u in every]
    kg = [(k[u] * jnp.exp(gl[u] - m_new[u])).astype(BF16) for u in every]
    for u in every:
        c_ref[u] = jnp.exp(a_tot[u] + m_st[u] - m_new[u]) * c_ref[u] + _dot_tn(kg[u], v_aug[u])
        m_ref[u:u + 1, :] = jnp.broadcast_to(m_new[u], (1, LANES))

    mu = [jnp.mean(h_c[u], axis=-1, keepdims=True) for u in every]
    hc = [h_c[u] - mu[u] for u in every]
    var = [jnp.mean(hc[u] * hc[u], axis=-1, keepdims=True) for u in every]
    for u, (b, h) in enumerate(chains):
        hn = hc[u] * lax.rsqrt(var[u] + M_NORM_EPS) * mh_ref[:, h * M_V:(h + 1) * M_V]
        y_ref[b, :, h * M_V:(h + 1) * M_V] = _sigmoid(o_ref[b, :, h * M_V:(h + 1) * M_V]) * hn


def _dot_exact_rhs_left(ones_bf16, x):
    hi, mid, lo = _split3(x)
    return _dot(ones_bf16, hi) + _dot(ones_bf16, mid) + _dot(ones_bf16, lo)


def _mlstm(u, bsz, seq, conv_w, conv_b, i_bias, f_bias, mh_w):
    nc = seq // CHUNK
    w = M_WIDTH
    gbias = jnp.zeros((1, LANES), F32).at[0, :M_HEADS].set(i_bias).at[0, M_HEADS:2 * M_HEADS].set(f_bias)
    u3 = u.reshape(bsz, seq, u.shape[1])
    blk = lambda col: pl.BlockSpec((bsz, CHUNK, w), lambda c: (0, c, col // w))
    const = lambda shape: pl.BlockSpec(shape, lambda c: (0, 0))
    y = pl.pallas_call(
        _mlstm_kernel,
        grid=(nc,),
        in_specs=[blk(C_MQK), blk(C_MV), blk(C_MO),
                  pl.BlockSpec((bsz, CHUNK, LANES), lambda c: (0, c, C_MG // LANES)),
                  const((CONV_K, w)), const((1, w)), const((1, LANES)), const((1, w))],
        out_specs=pl.BlockSpec((bsz, CHUNK, w), lambda c: (0, c, 0)),
        out_shape=jax.ShapeDtypeStruct((bsz, seq, w), F32),
        scratch_shapes=[pltpu.VMEM((bsz, CHUNK + SUBLANES, w), F32),
                        pltpu.VMEM((bsz * M_HEADS, M_QK, M_AUG), F32),
                        pltpu.VMEM((bsz * M_HEADS, LANES), F32)],
        compiler_params=_cparams(("arbitrary",)),
        name="mlstm",
    )(u3, u3, u3, u3, conv_w, conv_b[None, :], gbias, mh_w[None, :])
    return y.reshape(bsz * seq, w)


MERGE_TM = 256
C_ROUTE_G = N_EXPERTS


def _merge_kernel(alpha, ys_ref, bonus_ref, g_ref, ym_ref, gr_ref, gm_ref, x_ref,
                  lnxw_ref, lnxb_ref, bgr_ref, bgm_ref, wbr_ref, wbm_ref, wout_ref, l1w_ref, l1b_ref,
                  wrh_ref, wrl_ref, br_ref, x1_o, ids_o, wts_o, cnt_o):
    ys = ys_ref[...]
    mu = _seg_sum(ys, R_HEAD) * (1.0 / R_HEAD)
    yc = ys - mu
    var = _seg_sum(yc * yc, R_HEAD) * (1.0 / R_HEAD)
    y = yc * lax.rsqrt(var + R_GN_EPS) * lnxw_ref[...] + lnxb_ref[...]
    y_r = (y + bonus_ref[...]) * g_ref[...]
    br = _dot(y_r.astype(BF16), wbr_ref[...])
    bm = _dot(ym_ref[...].astype(BF16), wbm_ref[...])
    mix_in = _sigmoid(gr_ref[...] + bgr_ref[...]) * br + _sigmoid(gm_ref[...] + bgm_ref[...]) * bm
    mix = _dot(mix_in.astype(BF16), wout_ref[...])
    x1 = _layer_norm(alpha * x_ref[...] + mix, l1w_ref[...], l1b_ref[...])
    x1_o[...] = x1

    xh = x1.astype(BF16)
    xl = (x1 - xh.astype(F32)).astype(BF16)
    logits = (_dot(xh, wrh_ref[...]) + (_dot(xh, wrl_ref[...]) + _dot(xl, wrh_ref[...]))) + br_ref[...]
    tm = logits.shape[0]
    lane_i = lax.broadcasted_iota(I32, (tm, LANES), 1)
    lane = lane_i.astype(F32)
    group_of_lane = (lane_i // EXPERTS_PER_GROUP).astype(F32)
    big = float(LANES)
    neg = -jnp.inf
    lg = jnp.where((lane_i >= C_ROUTE_G) & (lane_i < C_ROUTE_G + N_GROUPS), logits, neg)
    gmax = jnp.max(lg, axis=-1, keepdims=True)
    gsel = jnp.min(jnp.where(lg == gmax, lane - C_ROUTE_G, big), axis=-1, keepdims=True)
    g_w = 1.0 / jnp.sum(jnp.exp(lg - gmax), axis=-1, keepdims=True)
    le = jnp.where((lane_i < N_EXPERTS) & (group_of_lane == gsel), logits, neg)
    m1 = jnp.max(le, axis=-1, keepdims=True)
    i1 = jnp.min(jnp.where(le == m1, lane, big), axis=-1, keepdims=True)
    le2 = jnp.where(lane == i1, neg, le)
    m2 = jnp.max(le2, axis=-1, keepdims=True)
    i2 = jnp.min(jnp.where(le2 == m2, lane, big), axis=-1, keepdims=True)
    e2 = jnp.exp(m2 - m1)
    w1 = g_w / (1.0 + e2)
    w2 = g_w * e2 / (1.0 + e2)
    wts_o[...] = jnp.where(lane_i == 0, w1, jnp.where(lane_i == 1, w2, 0.0))

    oh1 = jnp.where(lane == i1, 1.0, 0.0)
    oh2 = jnp.where(lane == i2, 1.0, 0.0)
    ri = lax.broadcasted_iota(I32, (tm, tm), 0)
    ci = lax.broadcasted_iota(I32, (tm, tm), 1)
    lstrict = jnp.where(ci < ri, 1.0, 0.0).astype(BF16)
    tot1 = jnp.sum(oh1, axis=0, keepdims=True)
    tot2 = jnp.sum(oh2, axis=0, keepdims=True)
    rank1 = jnp.sum(_dot(lstrict, oh1.astype(BF16)) * oh1, axis=-1, keepdims=True)
    rank2 = jnp.sum((_dot(lstrict, oh2.astype(BF16)) + tot1) * oh2, axis=-1, keepdims=True)
    ids = jnp.where(lane_i == 0, i1, jnp.where(lane_i == 1, i2, jnp.where(lane_i == 2, rank1,
                                                                          jnp.where(lane_i == 3, rank2, 0.0))))
    ids_o[...] = ids.astype(I32)
    cnt_o[...] = jnp.broadcast_to(tot1 + tot2, cnt_o.shape).astype(I32)


def _merge(alpha, ys, bonus, g, ym, u, x, lnx_w, lnx_b, b_gate, w_br, w_bm, w_out, ln1_w, ln1_b,
           w_rg, b_rg, w_re, b_re):
    n, d = x.shape
    tm = MERGE_TM
    wr = jnp.zeros((d, LANES), F32).at[:, :N_EXPERTS].set(w_re).at[:, C_ROUTE_G:C_ROUTE_G + N_GROUPS].set(w_rg)
    wr_hi = wr.astype(BF16)
    wr_lo = (wr - wr_hi.astype(F32)).astype(BF16)
    b_r = jnp.zeros((1, LANES), F32).at[0, :N_EXPERTS].set(b_re).at[0, C_ROUTE_G:C_ROUTE_G + N_GROUPS].set(b_rg)
    tile = lambda w_: pl.BlockSpec((tm, w_), lambda i: (i, 0))
    const = lambda shape: pl.BlockSpec(shape, lambda i: (0, 0))
    return pl.pallas_call(
        functools.partial(_merge_kernel, alpha),
        grid=(n // tm,),
        in_specs=[tile(R_WIDTH), tile(R_WIDTH), tile(R_WIDTH), tile(M_WIDTH),
                  pl.BlockSpec((tm, d), lambda i: (i, C_GR // d)),
                  pl.BlockSpec((tm, d), lambda i: (i, C_GM // d)),
                  tile(d),
                  const((1, R_WIDTH)), const((1, R_WIDTH)), const((1, d)), const((1, d)),
                  const((R_WIDTH, d)), const((M_WIDTH, d)), const((d, d)), const((1, d)), const((1, d)),
                  const((d, LANES)), const((d, LANES)), const((1, LANES))],
        out_specs=[tile(d), tile(LANES), tile(LANES), pl.BlockSpec((SUBLANES, LANES), lambda i: (i, 0))],
        out_shape=[jax.ShapeDtypeStruct((n, d), F32), jax.ShapeDtypeStruct((n, LANES), I32),
                   jax.ShapeDtypeStruct((n, LANES), F32),
                   jax.ShapeDtypeStruct((n // tm * SUBLANES, LANES), I32)],
        compiler_params=_cparams(("parallel",)),
        name="merge_ln1_router",
    )(ys, bonus, g, ym, u, u, x, lnx_w[None, :], lnx_b[None, :], b_gate[None, :d], b_gate[None, d:],
      w_br.astype(BF16), w_bm.astype(BF16), w_out.astype(BF16), ln1_w[None, :], ln1_b[None, :],
      wr_hi, wr_lo, b_r)


MOE_CH = 512
DISPATCH_TM = 256


ROW_DMA_UNROLL = 8


def _each(count, fn, unroll=1):
    def body(r, carry):
        fn(r)
        return carry
    lax.fori_loop(0, count, body, 0, unroll=unroll)


def _each_choice(fn):
    def both(r):
        for j in range(TOP_K):
            fn(r, j)
    return both


def _zero_unused_tail(pstart, nchunks, zeros_ref, out_hbm, sem):
    ch = zeros_ref.shape[0]
    used = (pstart[N_EXPERTS - 1] + nchunks[N_EXPERTS - 1] * ch) // ch
    total = out_hbm.shape[0] // ch

    def chunk(c):
        return pltpu.make_async_copy(zeros_ref, out_hbm.at[pl.ds(pl.multiple_of(c * ch, ch), ch)], sem)

    def over_tail(fn):
        def body(c, carry):
            fn(c)
            return carry
        lax.fori_loop(used, total, body, 0)

    over_tail(lambda c: chunk(c).start())
    over_tail(lambda c: chunk(c).wait())


def _moe_dispatch_kernel(dest, pstart, nchunks, x_ref, xs_hbm, zbuf, sem_z, sem):
    i = pl.program_id(0)
    tm = x_ref.shape[0]
    ch = zbuf.shape[0]

    @pl.when(i == 0)
    def _():
        zbuf[...] = jnp.zeros(zbuf.shape, F32)

        def tail(e):
            row0 = pl.multiple_of(pstart[e] + (nchunks[e] - 1) * ch, ch)
            return pltpu.make_async_copy(zbuf, xs_hbm.at[pl.ds(row0, ch)], sem_z)

        def start(e):
            @pl.when(nchunks[e] > 0)
            def _():
                tail(e).start()

        def wait(e):
            @pl.when(nchunks[e] > 0)
            def _():
                tail(e).wait()

        _each(N_EXPERTS, start)
        _each(N_EXPERTS, wait)
        _zero_unused_tail(pstart, nchunks, zbuf, xs_hbm, sem_z)

    def row(r, j):
        return pltpu.make_async_copy(x_ref.at[pl.ds(r, 1)],
                                     xs_hbm.at[pl.ds(dest[(i * tm + r) * TOP_K + j], 1)], sem)

    _each(tm, _each_choice(lambda r, j: row(r, j).start(priority=j)), ROW_DMA_UNROLL)
    _each(tm, _each_choice(lambda r, j: row(r, j).wait()), ROW_DMA_UNROLL)


def _moe_dispatch(x1, dest, pstart, nchunks, rows_pad):
    n, d = x1.shape
    tm = DISPATCH_TM
    return pl.pallas_call(
        _moe_dispatch_kernel,
        grid_spec=pltpu.PrefetchScalarGridSpec(
            num_scalar_prefetch=3,
            grid=(n // tm,),
            in_specs=[pl.BlockSpec((tm, d), lambda i, *_: (i, 0))],
            out_specs=pl.BlockSpec(memory_space=pl.ANY),
            scratch_shapes=[pltpu.VMEM((MOE_CH, d), F32), pltpu.SemaphoreType.DMA, pltpu.SemaphoreType.DMA]),
        out_shape=jax.ShapeDtypeStruct((rows_pad, d), F32),
        compiler_params=_cparams(("arbitrary",)),
        name="moe_dispatch",
    )(dest, pstart, nchunks, x1)


def _moe_expert_kernel(pstart, nchunks, xs_hbm, wg_ref, wu_ref, wd_ref, ys_hbm,
                       xbuf, ybuf, wgb, wub, wdb, sem_in, sem_out):
    e = pl.program_id(0)
    ch = xbuf.shape[1]
    nc = nchunks[e]
    g0 = pstart[e] // ch
    total = (pstart[N_EXPERTS - 1] + nchunks[N_EXPERTS - 1] * ch) // ch

    def rows(g):
        return pl.ds(pl.multiple_of(g * ch, ch), ch)

    def load(g):
        return pltpu.make_async_copy(xs_hbm.at[rows(g)], xbuf.at[g % 2], sem_in.at[g % 2])

    def store(g):
        return pltpu.make_async_copy(ybuf.at[g % 2], ys_hbm.at[rows(g)], sem_out.at[g % 2])

    @pl.when(nc > 0)
    def _():
        @pl.when(g0 == 0)
        def _():
            load(0).start()

        wgb[...] = wg_ref[0].astype(BF16)
        wub[...] = wu_ref[0].astype(BF16)
        wdb[...] = wd_ref[0].astype(BF16)

        def chunk(c, carry):
            g = g0 + c
            load(g).wait()

            @pl.when(g + 1 < total)
            def _():
                load(g + 1).start()

            @pl.when(g >= 2)
            def _():
                store(g - 2).wait()

            xb = xbuf[g % 2].astype(BF16)
            gate = _dot(xb, wgb[...])
            hb = gate * _sigmoid(gate) * _dot(xb, wub[...])
            ybuf[g % 2] = _dot(hb.astype(BF16), wdb[...])
            store(g).start()
            return carry

        lax.fori_loop(0, nc, chunk, 0)

    @pl.when(e == pl.num_programs(0) - 1)
    def _():
        @pl.when(total >= 2)
        def _():
            store(total - 2).wait()

        @pl.when(total >= 1)
        def _():
            store(total - 1).wait()

        ybuf[0] = jnp.zeros(ybuf.shape[1:], F32)
        _zero_unused_tail(pstart, nchunks, ybuf.at[0], ys_hbm, sem_out.at[0])


def _moe_experts(xs, pstart, nchunks, w_gate, w_up, w_down):
    rows_pad, d = xs.shape
    de = w_gate.shape[-1]
    wspec = lambda shape: pl.BlockSpec(shape, lambda e, *_: (e, 0, 0))
    return pl.pallas_call(
        _moe_expert_kernel,
        grid_spec=pltpu.PrefetchScalarGridSpec(
            num_scalar_prefetch=2,
            grid=(N_EXPERTS,),
            in_specs=[pl.BlockSpec(memory_space=pl.ANY),
                      wspec((1, d, de)), wspec((1, d, de)), wspec((1, de, d))],
            out_specs=pl.BlockSpec(memory_space=pl.ANY),
            scratch_shapes=[pltpu.VMEM((2, MOE_CH, d), F32), pltpu.VMEM((2, MOE_CH, d), F32),
                            pltpu.VMEM((d, de), BF16), pltpu.VMEM((d, de), BF16), pltpu.VMEM((de, d), BF16),
                            pltpu.SemaphoreType.DMA((2,)), pltpu.SemaphoreType.DMA((2,))]),
        out_shape=jax.ShapeDtypeStruct((rows_pad, d), F32),
        compiler_params=_cparams(("arbitrary",)),
        name="moe_experts",
    )(pstart, nchunks, xs, w_gate, w_up, w_down)


def _moe_plan(ids, cnt, tm):
    n = ids.shape[0]
    tile_cnt = cnt[::SUBLANES, :N_EXPERTS]
    counts = jnp.sum(tile_cnt, axis=0)
    padded = ((counts + MOE_CH - 1) // MOE_CH) * MOE_CH
    pstart = jnp.cumsum(padded) - padded
    tile_base = pstart[None, :] + jnp.cumsum(tile_cnt, axis=0) - tile_cnt
    eid = ids[:, 0:TOP_K]
    rank = ids[:, TOP_K:2 * TOP_K]
    base_of_tok = jnp.repeat(tile_base, tm, axis=0)[:, None, :]
    chosen = eid[:, :, None] == jnp.arange(N_EXPERTS, dtype=I32)[None, None, :]
    dest = jnp.sum(jnp.where(chosen, base_of_tok, 0), axis=-1) + rank
    return dest.reshape(-1).astype(I32), pstart.astype(I32), (padded // MOE_CH).astype(I32)


FINAL_TM = 256


def _final_kernel(alpha, dest, x1_ref, wts_ref, p_ref, wpg_ref, wple_ref, l2w_ref, l2b_ref, ys_hbm, o_ref,
                  ybuf, sem):
    i = pl.program_id(0)
    tm = x1_ref.shape[0]
    slot = i % 2

    def row(tile, buf, r, j):
        return pltpu.make_async_copy(ys_hbm.at[pl.ds(dest[(tile * tm + r) * TOP_K + j], 1)],
                                     ybuf.at[buf, j, pl.ds(r, 1)], sem.at[buf])

    def gather(tile, buf):
        _each(tm, _each_choice(lambda r, j: row(tile, buf, r, j).start(priority=j)), ROW_DMA_UNROLL)

    @pl.when(i == 0)
    def _():
        gather(0, 0)

    @pl.when(i + 1 < pl.num_programs(0))
    def _():
        gather(i + 1, 1 - slot)

    x1 = x1_ref[...]
    ple = _sigmoid(_dot(x1.astype(BF16), wpg_ref[...])) * _dot(p_ref[...], wple_ref[...])
    _each(tm, _each_choice(lambda r, j: row(i, slot, r, j).wait()), ROW_DMA_UNROLL)
    moe = ybuf[slot, 0] * wts_ref[:, 0:1] + ybuf[slot, 1] * wts_ref[:, 1:2]
    o_ref[...] = _layer_norm(alpha * x1 + moe + ple, l2w_ref[...], l2b_ref[...])


def _final(alpha, dest, x1, ys, wts, p_bf, w_pg, w_ple, ln2_w, ln2_b):
    n, d = x1.shape
    tm = FINAL_TM
    tile = lambda w_: pl.BlockSpec((tm, w_), lambda i, *_: (i, 0))
    const = lambda shape: pl.BlockSpec(shape, lambda i, *_: (0, 0))
    return pl.pallas_call(
        functools.partial(_final_kernel, alpha),
        grid_spec=pltpu.PrefetchScalarGridSpec(
            num_scalar_prefetch=1,
            grid=(n // tm,),
            in_specs=[tile(d), tile(LANES), tile(p_bf.shape[1]),
                      const((d, d)), const((p_bf.shape[1], d)), const((1, d)), const((1, d)),
                      pl.BlockSpec(memory_space=pl.ANY)],
            out_specs=tile(d),
            scratch_shapes=[pltpu.VMEM((2, TOP_K, tm, d), F32), pltpu.SemaphoreType.DMA((2,))]),
        out_shape=jax.ShapeDtypeStruct((n, d), F32),
        compiler_params=_cparams(("arbitrary",)),
        name="final_ln2",
    )(dest, x1, wts, p_bf, w_pg.astype(BF16), w_ple.astype(BF16), ln2_w[None, :], ln2_b[None, :], ys)


REGROUP_TM = 256


def _regroup_kernel(w_ref, o_ref):
    tm = w_ref.shape[0]
    m0 = RWKV_COLS
    mg = m0 + 2 * M_HEADS * M_QK + M_WIDTH
    cols = lambda a, n: w_ref[:, a:a + n].astype(BF16)
    zeros = lambda n: jnp.zeros((tm, n), BF16)
    o_ref[:, C_RKV:C_MQK] = cols(0, 3 * R_WIDTH)
    o_ref[:, C_MQK:C_MO] = cols(m0, C_MO - C_MQK)
    o_ref[:, C_MO:C_LORA] = cols(mg + 2 * M_HEADS, C_LORA - C_MO)
    o_ref[:, C_LORA:C_MG] = jnp.concatenate([cols(3 * R_WIDTH, LORA_COLS), zeros(LORA_PAD - LORA_COLS)], axis=1)
    o_ref[:, C_MG:C_TOTAL] = jnp.concatenate([cols(mg, 2 * M_HEADS), zeros(LANES - 2 * M_HEADS)], axis=1)


def _regroup_w_in(w):
    k, n = w.shape
    assert n == RWKV_COLS + MLSTM_COLS + 2 * k and C_LORA - C_MO == M_WIDTH + 2 * k
    return pl.pallas_call(
        _regroup_kernel,
        grid=(k // REGROUP_TM,),
        in_specs=[pl.BlockSpec((REGROUP_TM, n), lambda i: (i, 0))],
        out_specs=pl.BlockSpec((REGROUP_TM, C_TOTAL), lambda i: (i, 0)),
        out_shape=jax.ShapeDtypeStruct((k, C_TOTAL), BF16),
        compiler_params=_cparams(("parallel",)),
        name="regroup_w_in",
    )(w)


def kernel(x, p, w_in, mu_shift, w0, w_w2, a0, w_a2, w_g2, k_k, k_a, r_k, lnx_w, lnx_b, conv_w, conv_b,
           i_bias, f_bias, mh_w, b_gate, w_br, w_bm, w_out, ln1_w, ln1_b, w_rg, b_rg, w_re, b_re,
           w_gate, w_up, w_down, w_pg, w_ple, ln2_w, ln2_b):
    bsz, seq, d = x.shape
    depth = w_in.shape[0]
    assert bsz * R_HEADS * 2 == LANES and seq % PROJ_TM == 0 and seq % CHUNK == 0
    alpha = (2 * depth) ** 0.25
    n = bsz * seq
    xf = x.reshape(n, d)
    for i in range(depth):
        u = _proj_in(xf, _regroup_w_in(w_in[i]))
        r, dec, kp, v, kn, beta, g, bonus = _rwkv_prep(
            u, seq, mu_shift[i], w0[i], w_w2[i], a0[i], w_a2[i], w_g2[i], k_k[i], k_a[i], r_k[i])
        ys = _rwkv_scan(kn, dec, kp, beta, r, v, bsz, seq)
        ym = _mlstm(u, bsz, seq, conv_w[i], conv_b[i], i_bias[i], f_bias[i], mh_w[i])
        x1, ids, wts, cnt = _merge(alpha, ys, bonus, g, ym, u, xf, lnx_w[i], lnx_b[i], b_gate[i], w_br[i],
                                   w_bm[i], w_out[i], ln1_w[i], ln1_b[i], w_rg[i], b_rg[i], w_re[i], b_re[i])
        dest, pstart, nchunks = _moe_plan(ids, cnt, MERGE_TM)
        rows_pad = TOP_K * n + N_EXPERTS * MOE_CH
        xs = _moe_dispatch(x1, dest, pstart, nchunks, rows_pad)
        ys_sorted = _moe_experts(xs, pstart, nchunks, w_gate[i], w_up[i], w_down[i])
        xf = _final(alpha, dest, x1, ys_sorted, wts, p[i].reshape(n, -1).astype(BF16), w_pg[i], w_ple[i],
                    ln2_w[i], ln2_b[i])
    return xf.reshape(bsz, seq, d)
```
